```python
import math
import jax, jax.numpy as jnp
from jax import lax
import numpy as np

D_MODEL = 1024
BATCH = 8
SEQ = 4096
DEPTH = 2

CHUNK = 64
PLE_DIM = 256
D_MIX = D_MODEL
SSM_WIDTH = D_MIX // 2
SSM_GROUP = 16
SSM_GROUPS = SSM_WIDTH // SSM_GROUP
SSM_STATE = 64
ATTN_WIDTH = D_MIX - SSM_WIDTH
ATTN_HEADS = 8
HEAD_DIM = ATTN_WIDTH // ATTN_HEADS
Q_BLOCK = 128
RMS_EPS = 1e-6
DT_MIN = 1e-3
DT_MAX = 1e-1
IN_COLS = 2 * SSM_WIDTH + 4 * ATTN_WIDTH
SPLITS = [SSM_WIDTH, 2 * SSM_WIDTH, 2 * SSM_WIDTH + ATTN_WIDTH,
          2 * SSM_WIDTH + 2 * ATTN_WIDTH, 2 * SSM_WIDTH + 3 * ATTN_WIDTH]

kernel_name = "hymba_s5_stickbreaking_ple_block"


def rmsnorm(x, g):
    xf = x.astype(jnp.float32)
    xf = xf * lax.rsqrt(jnp.mean(xf * xf, axis=-1, keepdims=True) + RMS_EPS)
    return (xf * g.astype(jnp.float32)).astype(x.dtype)


def _complex_scan_combine(left, right):
    a1r, a1i, x1r, x1i = left
    a2r, a2i, x2r, x2i = right
    ar = a2r * a1r - a2i * a1i
    ai = a2r * a1i + a2i * a1r
    xr = a2r * x1r - a2i * x1i + x2r
    xi = a2r * x1i + a2i * x1r + x2i
    return ar, ai, xr, xi


def s5_branch(u, a_re, a_im, log_dt, b_re, b_im, c_re, c_im, d_skip, w_glu, b_glu):
    f32 = jnp.float32
    bsz, seq, _ = u.shape
    uf = u.astype(f32).reshape(bsz, seq, SSM_GROUPS, SSM_GROUP)
    dt = jnp.exp(log_dt.astype(f32))[:, None]
    lr = a_re.astype(f32)
    li = a_im.astype(f32)
    mag = jnp.exp(lr * dt)
    ab_re = mag * jnp.cos(li * dt)
    ab_im = mag * jnp.sin(li * dt)
    num_re = ab_re - 1.0
    num_im = ab_im
    den = lr * lr + li * li
    f_re = (num_re * lr + num_im * li) / den
    f_im = (num_im * lr - num_re * li) / den
    br = b_re.astype(f32)
    bi = b_im.astype(f32)
    bb_re = f_re[..., None] * br - f_im[..., None] * bi
    bb_im = f_re[..., None] * bi + f_im[..., None] * br
    cr = c_re.astype(f32)
    ci = c_im.astype(f32)

    n_chunks = seq // CHUNK
    u_chunks = uf.reshape(bsz, n_chunks, CHUNK, SSM_GROUPS, SSM_GROUP).transpose(1, 2, 0, 3, 4)
    a_re_l = jnp.broadcast_to(ab_re[None, None], (CHUNK, 1, SSM_GROUPS, SSM_STATE))
    a_im_l = jnp.broadcast_to(ab_im[None, None], (CHUNK, 1, SSM_GROUPS, SSM_STATE))

    def chunk_step(carry, u_c):
        h_re, h_im = carry
        bu_re = jnp.einsum('lbgh,gph->lbgp', u_c, bb_re)
        bu_im = jnp.einsum('lbgh,gph->lbgp', u_c, bb_im)
        p_re, p_im, x_re, x_im = lax.associative_scan(
            _complex_scan_combine, (a_re_l, a_im_l, bu_re, bu_im), axis=0)
        x_re = x_re + p_re * h_re - p_im * h_im
        x_im = x_im + p_re * h_im + p_im * h_re
        y_c = jnp.einsum('lbgp,ghp->lbgh', x_re, cr) - jnp.einsum('lbgp,ghp->lbgh', x_im, ci)
        return (x_re[-1], x_im[-1]), y_c

    h0 = jnp.zeros((bsz, SSM_GROUPS, SSM_STATE), f32)
    _, y = lax.scan(chunk_step, (h0, h0), u_chunks)
    y = y.transpose(2, 0, 1, 3, 4).reshape(bsz, seq, SSM_GROUPS, SSM_GROUP)
    y = (y + d_skip.astype(f32) * uf).reshape(bsz, seq, SSM_WIDTH)
    z = jax.nn.gelu(y)
    zz = z @ w_glu.astype(f32) + b_glu.astype(f32)
    val, gate = jnp.split(zz, 2, axis=-1)
    return (val * jax.nn.sigmoid(gate)).astype(u.dtype)


def stick_breaking_branch(q, k, v, q_g, k_g):
    f32 = jnp.float32
    bsz, seq, _ = q.shape

    def heads(t):
        return t.reshape(bsz, seq, ATTN_HEADS, HEAD_DIM).transpose(0, 2, 1, 3)

    qh = rmsnorm(heads(q), q_g).astype(f32)
    kh = rmsnorm(heads(k), k_g).astype(f32)
    vh = heads(v).astype(f32)
    scale = HEAD_DIM ** -0.5
    outs = []
    for blk in range(seq // Q_BLOCK):
        q0 = blk * Q_BLOCK
        kv_len = q0 + Q_BLOCK
        qb = qh[:, :, q0:kv_len]
        kb = kh[:, :, :kv_len]
        vb = vh[:, :, :kv_len]
        z = jnp.einsum('bhqd,bhkd->bhqk', qb, kb) * scale
        t_pos = q0 + jnp.arange(Q_BLOCK)[:, None]
        s_pos = jnp.arange(kv_len)[None, :]
        strict = s_pos < t_pos
        log_stay = jnp.where(strict, jax.nn.log_sigmoid(-z), 0.0)
        later = lax.cumsum(log_stay, axis=3, reverse=True) - log_stay
        weights = jnp.where(strict, jnp.exp(jax.nn.log_sigmoid(z) + later), 0.0)
        outs.append(jnp.einsum('bhqk,bhkd->bhqd', weights, vb))
    o = jnp.concatenate(outs, axis=2)
    return o.transpose(0, 2, 1, 3).reshape(bsz, seq, ATTN_WIDTH).astype(q.dtype)


def setup_inputs(seed: int = 0) -> dict:
    key = jax.random.key(seed)
    ks = jax.random.split(key, 20)
    f32 = jnp.float32
    G, P, H = SSM_GROUPS, SSM_STATE, SSM_GROUP
    n_idx = jnp.arange(P, dtype=f32)
    inputs = {
        "x": jax.random.normal(ks[0], (BATCH, SEQ, D_MODEL), f32),
        "p": jax.random.normal(ks[1], (DEPTH, BATCH, SEQ, PLE_DIM), f32),
        "mix_norm_g": 1.0 + 0.02 * jax.random.normal(ks[2], (DEPTH, D_MODEL), f32),
        "w_in": jax.random.normal(ks[3], (DEPTH, D_MODEL, IN_COLS), f32) * D_MODEL ** -0.5,
        "ssm_a_re": -0.5 + 0.01 * jax.random.normal(ks[4], (DEPTH, G, P), f32),
        "ssm_a_im": math.pi * n_idx + 0.01 * jax.random.normal(ks[5], (DEPTH, G, P), f32),
        "ssm_log_dt": jax.random.uniform(ks[6], (DEPTH, G), f32, math.log(DT_MIN), math.log(DT_MAX)),
        "ssm_b_re": jax.random.normal(ks[7], (DEPTH, G, P, H), f32) * (2.0 * H) ** -0.5,
        "ssm_b_im": jax.random.normal(ks[8], (DEPTH, G, P, H), f32) * (2.0 * H) ** -0.5,
        "ssm_c_re": jax.random.normal(ks[9], (DEPTH, G, H, P), f32) * (2.0 * P) ** -0.5,
        "ssm_c_im": jax.random.normal(ks[10], (DEPTH, G, H, P), f32) * (2.0 * P) ** -0.5,
        "ssm_d": jax.random.normal(ks[11], (DEPTH, G, H), f32),
        "ssm_w_glu": jax.random.normal(ks[12], (DEPTH, SSM_WIDTH, 2 * SSM_WIDTH), f32) * SSM_WIDTH ** -0.5,
        "ssm_b_glu": 0.01 * jax.random.normal(ks[13], (DEPTH, 2 * SSM_WIDTH), f32),
        "q_norm_g": 1.0 + 0.02 * jax.random.normal(ks[14], (DEPTH, HEAD_DIM), f32),
        "k_norm_g": 1.0 + 0.02 * jax.random.normal(ks[15], (DEPTH, HEAD_DIM), f32),
        "w_out": jax.random.normal(ks[16], (DEPTH, D_MIX, D_MODEL), f32) * D_MIX ** -0.5,
        "ple_norm_g": 1.0 + 0.02 * jax.random.normal(ks[17], (DEPTH, D_MODEL), f32),
        "w_ple_gate": jax.random.normal(ks[18], (DEPTH, D_MODEL, D_MODEL), f32) * D_MODEL ** -0.5,
        "w_ple_proj": jax.random.normal(ks[19], (DEPTH, PLE_DIM, D_MODEL), f32) * PLE_DIM ** -0.5,
    }
    return inputs


def reference(x, p, mix_norm_g, w_in, ssm_a_re, ssm_a_im, ssm_log_dt, ssm_b_re, ssm_b_im,
              ssm_c_re, ssm_c_im, ssm_d, ssm_w_glu, ssm_b_glu, q_norm_g, k_norm_g, w_out,
              ple_norm_g, w_ple_gate, w_ple_proj):
    h = x
    for i in range(DEPTH):
        hn = rmsnorm(h, mix_norm_g[i])
        proj = hn @ w_in[i]
        u, g_ssm, q, k, v, g_attn = jnp.split(proj, SPLITS, axis=-1)
        y_ssm = s5_branch(u, ssm_a_re[i], ssm_a_im[i], ssm_log_dt[i], ssm_b_re[i], ssm_b_im[i],
                          ssm_c_re[i], ssm_c_im[i], ssm_d[i], ssm_w_glu[i], ssm_b_glu[i])
        y_ssm = y_ssm * jax.nn.silu(g_ssm)
        y_att = stick_breaking_branch(q, k, v, q_norm_g[i], k_norm_g[i]) * jax.nn.silu(g_attn)
        h = h + jnp.concatenate([y_ssm, y_att], axis=-1) @ w_out[i]
        ple_gate = jax.nn.sigmoid(rmsnorm(h, ple_norm_g[i]) @ w_ple_gate[i])
        h = h + ple_gate * (p[i] @ w_ple_proj[i])
    return h
```

```python
import functools
import math

import jax
import jax.numpy as jnp
from jax import lax
from jax.experimental import pallas as pl
from jax.experimental.pallas import tpu as pltpu

F32 = jnp.float32
BF16 = jnp.bfloat16

D_MODEL = 1024
PLE_DIM = 256
SSM_WIDTH = 512
SSM_GROUP = 16
SSM_GROUPS = 32
SSM_PAIRS = SSM_GROUPS // 2
SSM_STATE = 64
SSM_CHUNK = 16
CHUNK_COLS = SSM_CHUNK * SSM_GROUP
ATTN_WIDTH = 512
HEAD_DIM = 64
HEAD_PAIRS = ATTN_WIDTH // 128
IN_COLS = 3072
RMS_EPS = 1e-6
LANES = 128
ATTN_BLOCK = 256
TOKEN_TILE = 512
VMEM_LIMIT = 56 * 1024 * 1024


def _sigmoid(x):
    return 1.0 / (1.0 + jnp.exp(-x))


def _dot(a, b):
    return jnp.dot(a, b, preferred_element_type=F32)


def _inproj_kernel(x_ref, g_ref, w_ref, o_ref):
    x = x_ref[...]
    ms = jnp.mean(x * x, axis=-1, keepdims=True)
    hn = (x * lax.rsqrt(ms + RMS_EPS)) * g_ref[...]
    o_ref[...] = _dot(hn.astype(BF16), w_ref[...])


def _inproj(h, g, w_bf):
    t = h.shape[0]
    tm = min(TOKEN_TILE, t)
    return pl.pallas_call(
        _inproj_kernel,
        grid=(t // tm,),
        in_specs=[
            pl.BlockSpec((tm, D_MODEL), lambda i: (i, 0)),
            pl.BlockSpec((1, D_MODEL), lambda i: (0, 0)),
            pl.BlockSpec((D_MODEL, IN_COLS), lambda i: (0, 0)),
        ],
        out_specs=pl.BlockSpec((tm, IN_COLS), lambda i: (i, 0)),
        out_shape=jax.ShapeDtypeStruct((t, IN_COLS), F32),
        compiler_params=pltpu.CompilerParams(
            dimension_semantics=("arbitrary",), vmem_limit_bytes=VMEM_LIMIT),
        name="inproj",
    )(h, g.reshape(1, D_MODEL), w_bf)


def _ssm_setup_kernel(are_r, aim_r, ldt_r, are_c, aim_c, ldt_c, br_ref, bi_ref, crt_ref, cit_ref,
                      min_ref, mout_ref, toep_ref, al_ref):
    lane = lax.broadcasted_iota(jnp.int32, (1, LANES), 1)
    is_a = lane < SSM_STATE

    a_re = are_r[...]
    a_im = aim_r[...]
    dt = jnp.exp(ldt_r[...])
    lam_r = a_re * dt
    lam_i = a_im * dt
    mag = jnp.exp(lam_r)
    ab_re = mag * jnp.cos(lam_i)
    ab_im = mag * jnp.sin(lam_i)
    num_re = ab_re - 1.0
    num_im = ab_im
    den = a_re * a_re + a_im * a_im
    f_re = (num_re * a_re + num_im * a_im) / den
    f_im = (num_im * a_re - num_re * a_im) / den
    b_re = br_ref[...]
    b_im = bi_ref[...]
    bb_re = f_re * b_re - f_im * b_im
    bb_im = f_re * b_im + f_im * b_re

    rows = lax.broadcasted_iota(jnp.int32, (CHUNK_COLS, 1), 0)
    e_in = (SSM_CHUNK - 1 - rows // SSM_GROUP).astype(F32)
    mag_in = jnp.exp(lam_r * e_in)
    p_re = mag_in * jnp.cos(lam_i * e_in)
    p_im = mag_in * jnp.sin(lam_i * e_in)
    bbr_t = jnp.tile(bb_re, (SSM_CHUNK, 1))
    bbi_t = jnp.tile(bb_im, (SSM_CHUNK, 1))
    m_re = p_re * bbr_t - p_im * bbi_t
    m_im = p_re * bbi_t + p_im * bbr_t
    zero = jnp.zeros_like(m_re)
    min_ref[0:CHUNK_COLS, 0:LANES] = jnp.where(is_a, m_re, zero).astype(BF16)
    min_ref[CHUNK_COLS:2 * CHUNK_COLS, 0:LANES] = jnp.where(is_a, zero, m_re).astype(BF16)
    min_ref[0:CHUNK_COLS, LANES:2 * LANES] = jnp.where(is_a, m_im, zero).astype(BF16)
    min_ref[CHUNK_COLS:2 * CHUNK_COLS, LANES:2 * LANES] = jnp.where(is_a, zero, m_im).astype(BF16)

    dt_c = jnp.exp(ldt_c[...])
    lam_rc = are_c[...] * dt_c
    lam_ic = aim_c[...] * dt_c
    cols = lax.broadcasted_iota(jnp.int32, (1, CHUNK_COLS), 1)
    lag = (cols // SSM_GROUP).astype(F32)
    c_re = crt_ref[...]
    c_im = cit_ref[...]
    row_a = lax.broadcasted_iota(jnp.int32, (LANES, 1), 0) < SSM_STATE

    def out_maps(expo):
        mg = jnp.exp(lam_rc * expo)
        q_re = mg * jnp.cos(lam_ic * expo)
        q_im = mg * jnp.sin(lam_ic * expo)
        return c_re * q_re - c_im * q_im, -c_re * q_im - c_im * q_re

    mo_re, mo_im = out_maps(lag + 1.0)
    zo = jnp.zeros_like(mo_re)
    mout_ref[0:LANES, 0:CHUNK_COLS] = jnp.where(row_a, mo_re, zo).astype(BF16)
    mout_ref[0:LANES, CHUNK_COLS:2 * CHUNK_COLS] = jnp.where(row_a, zo, mo_re).astype(BF16)
    mout_ref[LANES:2 * LANES, 0:CHUNK_COLS] = jnp.where(row_a, mo_im, zo).astype(BF16)
    mout_ref[LANES:2 * LANES, CHUNK_COLS:2 * CHUNK_COLS] = jnp.where(row_a, zo, mo_im).astype(BF16)

    k_re, k_im = out_maps(lag)
    zb = jnp.zeros_like(bb_re)
    for grp in range(2):
        sel_re = jnp.where(is_a, bb_re, zb) if grp == 0 else jnp.where(is_a, zb, bb_re)
        sel_im = jnp.where(is_a, bb_im, zb) if grp == 0 else jnp.where(is_a, zb, bb_im)
        kflat = (jnp.dot(sel_re, k_re, preferred_element_type=F32, precision=lax.Precision.HIGHEST)
                 + jnp.dot(sel_im, k_im, preferred_element_type=F32, precision=lax.Precision.HIGHEST))
        for lp in range(SSM_CHUNK):
            sh = lp * SSM_GROUP
            blk = kflat if lp == 0 else pltpu.roll(kflat, sh, axis=1)
            blk = jnp.where(cols >= sh, blk, jnp.zeros_like(blk))
            toep_ref[grp, sh:sh + SSM_GROUP, :] = blk.astype(BF16)

    big = float(SSM_CHUNK)
    mag_l = jnp.exp(lam_r * big)
    al_ref[0:1, :] = mag_l * jnp.cos(lam_i * big)
    al_ref[1:2, :] = mag_l * jnp.sin(lam_i * big)


def _ssm_setup(a_re, a_im, log_dt, b_re, b_im, c_re, c_im):
    depth = a_re.shape[0]
    g, p, h = SSM_GROUPS, SSM_STATE, SSM_GROUP
    ldt = jnp.broadcast_to(log_dt[:, :, None], (depth, g, p))

    def row(v):
        return v.reshape(depth, SSM_PAIRS, 1, LANES)

    def col(v):
        return v.reshape(depth, SSM_PAIRS, LANES, 1)

    def b_lay(v):
        return v.reshape(depth, SSM_PAIRS, 2, p, h).transpose(0, 1, 4, 2, 3).reshape(depth, SSM_PAIRS, h, LANES)

    def c_lay(v):
        vt = v.reshape(depth, SSM_PAIRS, 2, h, p).transpose(0, 1, 2, 4, 3).reshape(depth, SSM_PAIRS, LANES, h)
        return jnp.tile(vt, (1, 1, 1, SSM_CHUNK))

    def spec(shape):
        return pl.BlockSpec((None, None) + shape, lambda d, q: (d, q) + (0,) * len(shape))

    return pl.pallas_call(
        _ssm_setup_kernel,
        grid=(depth, SSM_PAIRS),
        in_specs=[spec((1, LANES))] * 3 + [spec((LANES, 1))] * 3
                 + [spec((h, LANES))] * 2 + [spec((LANES, CHUNK_COLS))] * 2,
        out_specs=[spec((2 * CHUNK_COLS, 2 * LANES)), spec((2 * LANES, 2 * CHUNK_COLS)),
                   spec((2, CHUNK_COLS, CHUNK_COLS)), spec((2, LANES))],
        out_shape=[
            jax.ShapeDtypeStruct((depth, SSM_PAIRS, 2 * CHUNK_COLS, 2 * LANES), BF16),
            jax.ShapeDtypeStruct((depth, SSM_PAIRS, 2 * LANES, 2 * CHUNK_COLS), BF16),
            jax.ShapeDtypeStruct((depth, SSM_PAIRS, 2, CHUNK_COLS, CHUNK_COLS), BF16),
            jax.ShapeDtypeStruct((depth, SSM_PAIRS, 2, LANES), F32),
        ],
        compiler_params=pltpu.CompilerParams(
            dimension_semantics=("arbitrary", "arbitrary"), vmem_limit_bytes=VMEM_LIMIT),
        name="ssm_setup",
    )(row(a_re), row(a_im), row(ldt), col(a_re), col(a_im), col(ldt),
      b_lay(b_re), b_lay(b_im), c_lay(c_re), c_lay(c_im))


def _ssm_main_kernel(u_ref, min_ref, mout_ref, toep_ref, al_ref, y_ref, z_ref, sp_ref, *, n_chunks, bsz):
    u = u_ref[...]
    z_ref[...] = _dot(u, min_ref[...])
    a_r = jnp.broadcast_to(al_ref[0:1, :], (bsz, LANES))
    a_i = jnp.broadcast_to(al_ref[1:2, :], (bsz, LANES))

    def step(c, carry):
        s_r, s_i = carry
        r = pl.multiple_of(c * bsz, bsz)
        sp_ref[pl.ds(r, bsz), 0:LANES] = s_r
        sp_ref[pl.ds(r, bsz), LANES:2 * LANES] = s_i
        z_r = z_ref[pl.ds(r, bsz), 0:LANES]
        z_i = z_ref[pl.ds(r, bsz), LANES:2 * LANES]
        return a_r * s_r - a_i * s_i + z_r, a_r * s_i + a_i * s_r + z_i

    zero = jnp.zeros((bsz, LANES), F32)
    lax.fori_loop(0, n_chunks, step, (zero, zero))
    sp = sp_ref[...].astype(BF16)
    y_ref[:, 0:CHUNK_COLS] = (_dot(u[:, 0:CHUNK_COLS], toep_ref[0])
                              + _dot(sp, mout_ref[:, 0:CHUNK_COLS]))
    y_ref[:, CHUNK_COLS:2 * CHUNK_COLS] = (_dot(u[:, CHUNK_COLS:2 * CHUNK_COLS], toep_ref[1])
                                           + _dot(sp, mout_ref[:, CHUNK_COLS:2 * CHUNK_COLS]))


def _ssm_main(u_t, m_in, m_out, toep, al, n_chunks, bsz):
    rows = n_chunks * bsz
    kern = functools.partial(_ssm_main_kernel, n_chunks=n_chunks, bsz=bsz)
    return pl.pallas_call(
        kern,
        grid=(SSM_PAIRS,),
        in_specs=[
            pl.BlockSpec((None, rows, 2 * CHUNK_COLS), lambda q: (q, 0, 0)),
            pl.BlockSpec((None, 2 * CHUNK_COLS, 2 * LANES), lambda q: (q, 0, 0)),
            pl.BlockSpec((None, 2 * LANES, 2 * CHUNK_COLS), lambda q: (q, 0, 0)),
            pl.BlockSpec((None, 2, CHUNK_COLS, CHUNK_COLS), lambda q: (q, 0, 0, 0)),
            pl.BlockSpec((None, 2, LANES), lambda q: (q, 0, 0)),
        ],
        out_specs=pl.BlockSpec((None, rows, 2 * CHUNK_COLS), lambda q: (q, 0, 0)),
        out_shape=jax.ShapeDtypeStruct((SSM_PAIRS, rows, 2 * CHUNK_COLS), F32),
        scratch_shapes=[pltpu.VMEM((rows, 2 * LANES), F32), pltpu.VMEM((rows, 2 * LANES), F32)],
        compiler_params=pltpu.CompilerParams(
            dimension_semantics=("arbitrary",), vmem_limit_bytes=VMEM_LIMIT),
        name="ssm_main",
    )(u_t, m_in, m_out, toep, al)


def _attn_kernel(q_ref, k_ref, v_ref, ga_ref, gq_ref, gk_ref, o_ref, kn_ref, vb_ref, *, blk, seq):
    i = pl.program_id(2)
    lane = lax.broadcasted_iota(jnp.int32, (1, LANES), 1)
    is_a = lane < HEAD_DIM

    def headnorm(x, g):
        x2 = x * x
        s_a = jnp.sum(jnp.where(is_a, x2, 0.0), axis=-1, keepdims=True)
        s_b = jnp.sum(jnp.where(is_a, 0.0, x2), axis=-1, keepdims=True)
        ms = jnp.where(is_a, s_a, s_b) * (1.0 / HEAD_DIM)
        return (x * lax.rsqrt(ms + RMS_EPS)) * g

    prep = min(512, seq)

    @pl.when(i == 0)
    def _():
        def body(c, carry):
            r = pl.multiple_of(c * prep, prep)
            kn_ref[pl.ds(r, prep), :] = headnorm(k_ref[pl.ds(r, prep), :], gk_ref[...]).astype(BF16)
            vb_ref[pl.ds(r, prep), :] = v_ref[pl.ds(r, prep), :].astype(BF16)
            return carry
        lax.fori_loop(0, seq // prep, body, 0)

    qn = headnorm(q_ref[...], gq_ref[...]) * (HEAD_DIM ** -0.5)
    q_heads = (jnp.where(is_a, qn, 0.0).astype(BF16), jnp.where(is_a, 0.0, qn).astype(BF16))
    row = lax.broadcasted_iota(jnp.int32, (blk, blk), 0)
    col = lax.broadcasted_iota(jnp.int32, (blk, blk), 1)
    strict = col < row
    tri = jnp.where(row > col, 1.0, 0.0).astype(BF16)

    def block(j, carry, masked):
        r = pl.multiple_of(j * blk, blk)
        kb = kn_ref[pl.ds(r, blk), :]
        vb = vb_ref[pl.ds(r, blk), :]
        new = []
        for hd in range(2):
            stay, acc = carry[2 * hd], carry[2 * hd + 1]
            z = lax.dot_general(q_heads[hd], kb, (((1,), (1,)), ((), ())), preferred_element_type=F32)
            ls = -(jnp.maximum(z, 0.0) + jnp.log1p(jnp.exp(-jnp.abs(z))))
            if masked:
                ls = jnp.where(strict, ls, 0.0)
            later = _dot(ls.astype(BF16), tri) + stay
            w = jnp.exp(z + ls + later)
            if masked:
                w = jnp.where(strict, w, 0.0)
            acc = acc + _dot(w.astype(BF16), vb)
            stay = stay + jnp.sum(ls, axis=-1, keepdims=True)
            new += [stay, acc]
        return tuple(new)

    init = (jnp.zeros((blk, 1), F32), jnp.zeros((blk, LANES), F32)) * 2
    carry = block(i, init, True)
    carry = lax.fori_loop(0, i, lambda t, c: block(i - 1 - t, c, False), carry)
    o = jnp.where(is_a, carry[1], carry[3])
    ga = ga_ref[...]
    o_ref[...] = o * (ga * _sigmoid(ga))


def _attention(proj3, gq, gk):
    bsz, seq, _ = proj3.shape
    blk = min(ATTN_BLOCK, seq)
    q0 = (2 * SSM_WIDTH) // LANES
    k0 = q0 + HEAD_PAIRS
    v0 = k0 + HEAD_PAIRS
    g0 = v0 + HEAD_PAIRS
    kern = functools.partial(_attn_kernel, blk=blk, seq=seq)
    gq2 = jnp.tile(gq.reshape(1, HEAD_DIM), (1, 2))
    gk2 = jnp.tile(gk.reshape(1, HEAD_DIM), (1, 2))
    return pl.pallas_call(
        kern,
        grid=(bsz, HEAD_PAIRS, seq // blk),
        in_specs=[
            pl.BlockSpec((None, blk, LANES), lambda b, p, i: (b, i, q0 + p)),
            pl.BlockSpec((None, seq, LANES), lambda b, p, i: (b, 0, k0 + p)),
            pl.BlockSpec((None, seq, LANES), lambda b, p, i: (b, 0, v0 + p)),
            pl.BlockSpec((None, blk, LANES), lambda b, p, i: (b, i, g0 + p)),
            pl.BlockSpec((1, LANES), lambda b, p, i: (0, 0)),
            pl.BlockSpec((1, LANES), lambda b, p, i: (0, 0)),
        ],
        out_specs=pl.BlockSpec((None, blk, LANES), lambda b, p, i: (b, i, p)),
        out_shape=jax.ShapeDtypeStruct((bsz, seq, ATTN_WIDTH), F32),
        scratch_shapes=[pltpu.VMEM((seq, LANES), BF16), pltpu.VMEM((seq, LANES), BF16)],
        compiler_params=pltpu.CompilerParams(
            dimension_semantics=("arbitrary", "arbitrary", "arbitrary"), vmem_limit_bytes=VMEM_LIMIT),
        name="attention",
    )(proj3, proj3, proj3, proj3, gq2, gk2)


def _out_kernel(h_ref, yt_ref, u_ref, gs_ref, ya_ref, p_ref, d_ref, wg_ref, bg_ref, wo1_ref, wo2_ref,
                pg_ref, wpg_ref, wpp_ref, o_ref):
    y = yt_ref[...] + d_ref[...] * u_ref[...]
    z = 0.5 * y * (1.0 + jnp.tanh(math.sqrt(2.0 / math.pi) * (y + 0.044715 * (y * y * y))))
    zz = _dot(z.astype(BF16), wg_ref[...]) + bg_ref[...]
    val = zz[:, 0:SSM_WIDTH]
    gate = zz[:, SSM_WIDTH:2 * SSM_WIDTH]
    gs = gs_ref[...]
    y_ssm = (val * _sigmoid(gate)) * (gs * _sigmoid(gs))
    h = h_ref[...] + _dot(y_ssm.astype(BF16), wo1_ref[...]) + _dot(ya_ref[...].astype(BF16), wo2_ref[...])
    ms = jnp.mean(h * h, axis=-1, keepdims=True)
    hn = (h * lax.rsqrt(ms + RMS_EPS)) * pg_ref[...]
    ple_gate = _sigmoid(_dot(hn.astype(BF16), wpg_ref[...]))
    o_ref[...] = h + ple_gate * _dot(p_ref[...].astype(BF16), wpp_ref[...])


def _out(h, y_t, proj, y_att, p, d_skip, w_glu_bf, b_glu, w_out_bf, ple_g, w_pg_bf, w_pp_bf):
    t = h.shape[0]
    tm = min(TOKEN_TILE, t)

    def const(shape):
        return pl.BlockSpec(shape, lambda i: (0, 0))

    return pl.pallas_call(
        _out_kernel,
        grid=(t // tm,),
        in_specs=[
            pl.BlockSpec((tm, D_MODEL), lambda i: (i, 0)),
            pl.BlockSpec((tm, SSM_WIDTH), lambda i: (i, 0)),
            pl.BlockSpec((tm, SSM_WIDTH), lambda i: (i, 0)),
            pl.BlockSpec((tm, SSM_WIDTH), lambda i: (i, 1)),
            pl.BlockSpec((tm, ATTN_WIDTH), lambda i: (i, 0)),
            pl.BlockSpec((tm, PLE_DIM), lambda i: (i, 0)),
            const((1, SSM_WIDTH)),
            const((SSM_WIDTH, 2 * SSM_WIDTH)),
            const((1, 2 * SSM_WIDTH)),
            const((SSM_WIDTH, D_MODEL)),
            const((ATTN_WIDTH, D_MODEL)),
            const((1, D_MODEL)),
            const((D_MODEL, D_MODEL)),
            const((PLE_DIM, D_MODEL)),
        ],
        out_specs=pl.BlockSpec((tm, D_MODEL), lambda i: (i, 0)),
        out_shape=jax.ShapeDtypeStruct((t, D_MODEL), F32),
        compiler_params=pltpu.CompilerParams(
            dimension_semantics=("arbitrary",), vmem_limit_bytes=VMEM_LIMIT),
        name="out",
    )(h, y_t, proj, proj, y_att, p, d_skip.reshape(1, SSM_WIDTH), w_glu_bf, b_glu.reshape(1, 2 * SSM_WIDTH),
      w_out_bf[:SSM_WIDTH], w_out_bf[SSM_WIDTH:], ple_g.reshape(1, D_MODEL), w_pg_bf, w_pp_bf)


def kernel(x, p, mix_norm_g, w_in, ssm_a_re, ssm_a_im, ssm_log_dt, ssm_b_re, ssm_b_im, ssm_c_re, ssm_c_im,
           ssm_d, ssm_w_glu, ssm_b_glu, q_norm_g, k_norm_g, w_out, ple_norm_g, w_ple_gate, w_ple_proj):
    bsz, seq, _ = x.shape
    depth = w_in.shape[0]
    t = bsz * seq
    n_chunks = seq // SSM_CHUNK
    assert seq % SSM_CHUNK == 0 and seq % min(ATTN_BLOCK, seq) == 0 and t % min(TOKEN_TILE, t) == 0

    m_in, m_out, toep, al = _ssm_setup(ssm_a_re, ssm_a_im, ssm_log_dt, ssm_b_re, ssm_b_im, ssm_c_re, ssm_c_im)
    w_in_bf = w_in.astype(BF16)
    w_glu_bf = ssm_w_glu.astype(BF16)
    w_out_bf = w_out.astype(BF16)
    w_pg_bf = w_ple_gate.astype(BF16)
    w_pp_bf = w_ple_proj.astype(BF16)

    h = x.reshape(t, D_MODEL)
    for i in range(depth):
        proj = _inproj(h, mix_norm_g[i], w_in_bf[i])
        u_t = (proj[:, :SSM_WIDTH].astype(BF16)
               .reshape(bsz, n_chunks, SSM_CHUNK, SSM_PAIRS, 2, SSM_GROUP)
               .transpose(3, 1, 0, 4, 2, 5)
               .reshape(SSM_PAIRS, n_chunks * bsz, 2 * CHUNK_COLS))
        y_p = _ssm_main(u_t, m_in[i], m_out[i], toep[i], al[i], n_chunks, bsz)
        y_t = (y_p.reshape(SSM_PAIRS, n_chunks, bsz, 2, SSM_CHUNK, SSM_GROUP)
               .transpose(2, 1, 4, 0, 3, 5)
               .reshape(t, SSM_WIDTH))
        y_att = _attention(proj.reshape(bsz, seq, IN_COLS), q_norm_g[i], k_norm_g[i]).reshape(t, ATTN_WIDTH)
        h = _out(h, y_t, proj, y_att, p[i].reshape(t, PLE_DIM), ssm_d[i].reshape(SSM_WIDTH), w_glu_bf[i],
                 ssm_b_glu[i], w_out_bf[i], ple_norm_g[i], w_pg_bf[i], w_pp_bf[i])
    return h.reshape(bsz, seq, D_MODEL)
```

```python
import functools
import math

import jax
import jax.numpy as jnp
from jax import lax
from jax.experimental import pallas as pl
from jax.experimental.pallas import tpu as pltpu

F32 = jnp.float32
BF16 = jnp.bfloat16

D_MODEL = 1024
PLE_DIM = 256
SSM_WIDTH = 512
SSM_GROUP = 16
SSM_STATE = 64
LANES = 128
SLAB_GROUPS = LANES // SSM_GROUP
SSM_SLABS = SSM_WIDTH // LANES
SLAB_STATES = SLAB_GROUPS * SSM_STATE
SSM_CHUNK = 8
CHUNK_COLS = SSM_CHUNK * LANES
ATTN_WIDTH = 512
HEAD_DIM = 64
HEAD_PAIRS = ATTN_WIDTH // LANES
PAIRS_PER_STEP = 2
IN_COLS = 3072
F32_COLS = 2 * SSM_WIDTH + ATTN_WIDTH
QKV_COLS = 3 * ATTN_WIDTH
RMS_EPS = 1e-6
LOG2E = 1.4426950408889634
ATTN_BLOCK = 256
STAY_BITS_DONE = 152.0
TOKEN_TILE = 512
VMEM_LIMIT = 56 * 1024 * 1024


def _sigmoid(x):
    return 1.0 / (1.0 + jnp.exp(-x))


def _dot(a, b):
    return jnp.dot(a, b, preferred_element_type=F32)


def _dot_exact(a, b):
    return jnp.dot(a, b, preferred_element_type=F32, precision=lax.Precision.HIGHEST)


def _head_mask():
    return lax.broadcasted_iota(jnp.int32, (1, LANES), 1) < HEAD_DIM


def _headnorm(x, g, is_a):
    x2 = x * x
    s_a = jnp.sum(jnp.where(is_a, x2, 0.0), axis=-1, keepdims=True)
    s_b = jnp.sum(jnp.where(is_a, 0.0, x2), axis=-1, keepdims=True)
    ms = jnp.where(is_a, s_a, s_b) * (1.0 / HEAD_DIM)
    return (x * lax.rsqrt(ms + RMS_EPS)) * g


def _inproj_kernel(x_ref, g_ref, w_ref, gq_ref, gk_ref, f_ref, qkv_ref):
    x = x_ref[...]
    ms = jnp.mean(x * x, axis=-1, keepdims=True)
    hn = (x * lax.rsqrt(ms + RMS_EPS)) * g_ref[...]
    res = _dot(hn.astype(BF16), w_ref[...])
    f_ref[...] = res[:, 0:F32_COLS]
    is_a = _head_mask()
    q_scale = (HEAD_DIM ** -0.5) * LOG2E
    for s in range(HEAD_PAIRS):
        c0 = F32_COLS + s * LANES
        q = _headnorm(res[:, c0:c0 + LANES], gq_ref[...], is_a) * q_scale
        qkv_ref[:, s * LANES:(s + 1) * LANES] = q.astype(BF16)
        c1 = c0 + ATTN_WIDTH
        k = _headnorm(res[:, c1:c1 + LANES], gk_ref[...], is_a)
        qkv_ref[:, ATTN_WIDTH + s * LANES:ATTN_WIDTH + (s + 1) * LANES] = k.astype(BF16)
    qkv_ref[:, 2 * ATTN_WIDTH:3 * ATTN_WIDTH] = res[:, F32_COLS + 2 * ATTN_WIDTH:IN_COLS].astype(BF16)


def _inproj(h, g, w_bf, gq2, gk2):
    t = h.shape[0]
    tm = min(TOKEN_TILE, t)
    return pl.pallas_call(
        _inproj_kernel,
        grid=(t // tm,),
        in_specs=[
            pl.BlockSpec((tm, D_MODEL), lambda i: (i, 0)),
            pl.BlockSpec((1, D_MODEL), lambda i: (0, 0)),
            pl.BlockSpec((D_MODEL, IN_COLS), lambda i: (0, 0)),
            pl.BlockSpec((1, LANES), lambda i: (0, 0)),
            pl.BlockSpec((1, LANES), lambda i: (0, 0)),
        ],
        out_specs=[pl.BlockSpec((tm, F32_COLS), lambda i: (i, 0)),
                   pl.BlockSpec((tm, QKV_COLS), lambda i: (i, 0))],
        out_shape=[jax.ShapeDtypeStruct((t, F32_COLS), F32), jax.ShapeDtypeStruct((t, QKV_COLS), BF16)],
        compiler_params=pltpu.CompilerParams(
            dimension_semantics=("arbitrary",), vmem_limit_bytes=VMEM_LIMIT),
        name="inproj",
    )(h, g.reshape(1, D_MODEL), w_bf, gq2, gk2)


def _ssm_setup_kernel(are_r, aim_r, ldt_r, are_c, aim_c, ldt_c, bre_ref, bim_ref, cre_ref, cim_ref,
                      w1_ref, w2_ref, pw_ref, *, n_pw, n_steps):
    ns = SLAB_STATES
    a_re = are_r[...]
    a_im = aim_r[...]
    dt = jnp.exp(ldt_r[...])
    lam_r = a_re * dt
    lam_i = a_im * dt
    mag = jnp.exp(lam_r)
    ab_re = mag * jnp.cos(lam_i)
    ab_im = mag * jnp.sin(lam_i)
    num_re = ab_re - 1.0
    num_im = ab_im
    den = a_re * a_re + a_im * a_im
    f_re = (num_re * a_re + num_im * a_im) / den
    f_im = (num_im * a_re - num_re * a_im) / den
    b_re = bre_ref[...]
    b_im = bim_ref[...]
    bb_re = jnp.tile(f_re * b_re - f_im * b_im, (SLAB_GROUPS, 1))
    bb_im = jnp.tile(f_re * b_im + f_im * b_re, (SLAB_GROUPS, 1))
    grp_row = lax.broadcasted_iota(jnp.int32, (LANES, ns), 0) >> 4
    grp_col = lax.broadcasted_iota(jnp.int32, (LANES, ns), 1) >> 6
    same_in = grp_row == grp_col
    zin = jnp.zeros((LANES, ns), F32)
    bb_re = jnp.where(same_in, bb_re, zin)
    bb_im = jnp.where(same_in, bb_im, zin)

    e_in = (SSM_CHUNK - 1 - lax.broadcasted_iota(jnp.int32, (SSM_CHUNK, 1), 0)).astype(F32)
    mag_in = jnp.exp(lam_r * e_in)
    pin_re = mag_in * jnp.cos(lam_i * e_in)
    pin_im = mag_in * jnp.sin(lam_i * e_in)
    for lp in range(SSM_CHUNK):
        p_re = pin_re[lp:lp + 1, :]
        p_im = pin_im[lp:lp + 1, :]
        w1_ref[lp * LANES:(lp + 1) * LANES, 0:ns] = (p_re * bb_re - p_im * bb_im).astype(BF16)
        w1_ref[lp * LANES:(lp + 1) * LANES, ns:2 * ns] = (p_re * bb_im + p_im * bb_re).astype(BF16)

    dt_c = jnp.exp(ldt_c[...])
    lam_rc = are_c[...] * dt_c
    lam_ic = aim_c[...] * dt_c
    lag = (lax.broadcasted_iota(jnp.int32, (1, LANES), 1) & (SSM_CHUNK - 1)).astype(F32)
    grp_row_o = lax.broadcasted_iota(jnp.int32, (ns, LANES), 0) >> 6
    grp_col_o = lax.broadcasted_iota(jnp.int32, (ns, LANES), 1) >> 4
    same_out = grp_row_o == grp_col_o
    zout = jnp.zeros((ns, LANES), F32)
    c_re = jnp.where(same_out, cre_ref[...], zout)
    c_im = jnp.where(same_out, cim_ref[...], zout)

    def powers(expo):
        mg = jnp.exp(lam_rc * (lag + expo))
        return mg * jnp.cos(lam_ic * (lag + expo)), mg * jnp.sin(lam_ic * (lag + expo))

    p1_re, p1_im = powers(1.0)
    p0_re, p0_im = powers(0.0)
    kf = []
    for l in range(SSM_CHUNK):
        q_re = jnp.broadcast_to(p1_re[:, l:l + 1], (ns, LANES))
        q_im = jnp.broadcast_to(p1_im[:, l:l + 1], (ns, LANES))
        w2_ref[0:ns, l * LANES:(l + 1) * LANES] = (c_re * q_re - c_im * q_im).astype(BF16)
        w2_ref[ns:2 * ns, l * LANES:(l + 1) * LANES] = (-c_re * q_im - c_im * q_re).astype(BF16)
        r_re = jnp.broadcast_to(p0_re[:, l:l + 1], (ns, LANES))
        r_im = jnp.broadcast_to(p0_im[:, l:l + 1], (ns, LANES))
        kf.append((_dot_exact(bb_re, c_re * r_re - c_im * r_im)
                   + _dot_exact(bb_im, -c_re * r_im - c_im * r_re)).astype(BF16))
    zblk = jnp.zeros((LANES, LANES), BF16)
    for lp in range(SSM_CHUNK):
        for l in range(SSM_CHUNK):
            w1_ref[lp * LANES:(lp + 1) * LANES, 2 * ns + l * LANES:2 * ns + (l + 1) * LANES] = (
                kf[l - lp] if l >= lp else zblk)

    step = jnp.minimum(lax.broadcasted_iota(jnp.int32, (n_pw, 1), 0), n_steps - 1)
    e_pw = jnp.left_shift(SSM_CHUNK, step).astype(F32)
    mag_pw = jnp.exp(lam_r * e_pw)
    pw_ref[:, 0:ns] = mag_pw * jnp.cos(lam_i * e_pw)
    pw_ref[:, ns:2 * ns] = mag_pw * jnp.sin(lam_i * e_pw)


def _ssm_setup(a_re, a_im, log_dt, b_re, b_im, c_re, c_im, n_pw, n_steps):
    depth = a_re.shape[0]
    g, p, h = a_re.shape[1], SSM_STATE, SSM_GROUP
    ns = SLAB_STATES
    ldt = jnp.broadcast_to(log_dt[:, :, None], (depth, g, p))

    def row(v):
        return v.reshape(depth, SSM_SLABS, 1, ns)

    def col(v):
        return v.reshape(depth, SSM_SLABS, ns, 1)

    def b_lay(v):
        return v.reshape(depth, SSM_SLABS, SLAB_GROUPS, p, h).transpose(0, 1, 4, 2, 3).reshape(
            depth, SSM_SLABS, h, ns)

    def c_lay(v):
        vt = v.reshape(depth, SSM_SLABS, SLAB_GROUPS, h, p).transpose(0, 1, 2, 4, 3).reshape(
            depth, SSM_SLABS, ns, h)
        return jnp.tile(vt, (1, 1, 1, SLAB_GROUPS))

    def spec(shape):
        return pl.BlockSpec((None, None) + shape, lambda d, q: (d, q) + (0,) * len(shape))

    return pl.pallas_call(
        functools.partial(_ssm_setup_kernel, n_pw=n_pw, n_steps=n_steps),
        grid=(depth, SSM_SLABS),
        in_specs=[spec((1, ns))] * 3 + [spec((ns, 1))] * 3 + [spec((h, ns))] * 2 + [spec((ns, LANES))] * 2,
        out_specs=[spec((CHUNK_COLS, 2 * ns + CHUNK_COLS)), spec((2 * ns, CHUNK_COLS)), spec((n_pw, 2 * ns))],
        out_shape=[
            jax.ShapeDtypeStruct((depth, SSM_SLABS, CHUNK_COLS, 2 * ns + CHUNK_COLS), BF16),
            jax.ShapeDtypeStruct((depth, SSM_SLABS, 2 * ns, CHUNK_COLS), BF16),
            jax.ShapeDtypeStruct((depth, SSM_SLABS, n_pw, 2 * ns), F32),
        ],
        compiler_params=pltpu.CompilerParams(
            dimension_semantics=("arbitrary", "arbitrary"), vmem_limit_bytes=VMEM_LIMIT),
        name="ssm_setup",
    )(row(a_re), row(a_im), row(ldt), col(a_re), col(a_im), col(ldt),
      b_lay(b_re), b_lay(b_im), c_lay(c_re), c_lay(c_im))


def _ssm_main_kernel(u_ref, w1_ref, w2_ref, pw_ref, y_ref, lhs_ref, buf_a, buf_b, *, n_chunks, n_steps, pad):
    ns = SLAB_STATES
    for l in range(SSM_CHUNK):
        lhs_ref[:, l * LANES:(l + 1) * LANES] = u_ref[pl.ds(l, n_chunks, stride=SSM_CHUNK), :].astype(BF16)
    zy = _dot(lhs_ref[...], w1_ref[...])

    zeros = jnp.zeros((pad, 2 * ns), F32)
    buf_a[0:pad, :] = zeros
    buf_b[0:pad, :] = zeros
    buf_a[pad:pad + n_chunks, :] = zy[:, 0:2 * ns]
    src, dst = buf_a, buf_b
    for k in range(n_steps):
        d = 1 << k
        x_r = src[pad:pad + n_chunks, 0:ns]
        x_i = src[pad:pad + n_chunks, ns:2 * ns]
        s_r = src[pad - d:pad - d + n_chunks, 0:ns]
        s_i = src[pad - d:pad - d + n_chunks, ns:2 * ns]
        p_r = pw_ref[k:k + 1, 0:ns]
        p_i = pw_ref[k:k + 1, ns:2 * ns]
        dst[pad:pad + n_chunks, 0:ns] = x_r + p_r * s_r - p_i * s_i
        dst[pad:pad + n_chunks, ns:2 * ns] = x_i + p_r * s_i + p_i * s_r
        src, dst = dst, src
    s_prev = src[pad - 1:pad - 1 + n_chunks, :].astype(BF16)
    y = zy[:, 2 * ns:2 * ns + CHUNK_COLS] + _dot(s_prev, w2_ref[...])
    for l in range(SSM_CHUNK):
        y_ref[pl.ds(l, n_chunks, stride=SSM_CHUNK), :] = y[:, l * LANES:(l + 1) * LANES]


def _ssm_main(pf3, w1, w2, pw, n_steps):
    bsz, seq, _ = pf3.shape
    n_chunks = seq // SSM_CHUNK
    ns = SLAB_STATES
    pad = max(8, n_chunks // 2)
    n_pw = pw.shape[1]
    kern = functools.partial(_ssm_main_kernel, n_chunks=n_chunks, n_steps=n_steps, pad=pad)
    return pl.pallas_call(
        kern,
        grid=(SSM_SLABS, bsz),
        in_specs=[
            pl.BlockSpec((None, seq, LANES), lambda s, b: (b, 0, s)),
            pl.BlockSpec((None, CHUNK_COLS, 2 * ns + CHUNK_COLS), lambda s, b: (s, 0, 0)),
            pl.BlockSpec((None, 2 * ns, CHUNK_COLS), lambda s, b: (s, 0, 0)),
            pl.BlockSpec((None, n_pw, 2 * ns), lambda s, b: (s, 0, 0)),
        ],
        out_specs=pl.BlockSpec((None, seq, LANES), lambda s, b: (b, 0, s)),
        out_shape=jax.ShapeDtypeStruct((bsz, seq, SSM_WIDTH), F32),
        scratch_shapes=[pltpu.VMEM((n_chunks, CHUNK_COLS), BF16),
                        pltpu.VMEM((pad + n_chunks, 2 * ns), F32),
                        pltpu.VMEM((pad + n_chunks, 2 * ns), F32)],
        compiler_params=pltpu.CompilerParams(
            dimension_semantics=("arbitrary", "arbitrary"), vmem_limit_bytes=VMEM_LIMIT),
        name="ssm_main",
    )(pf3, w1, w2, pw)


def _attn_kernel(q_ref, k_ref, v_ref, ga_ref, o_ref, acc_ref, stay_ref, *, blk):
    i = pl.program_id(2)
    is_a = _head_mask()
    row = lax.broadcasted_iota(jnp.int32, (blk, blk), 0)
    col = lax.broadcasted_iota(jnp.int32, (blk, blk), 1)
    strict = col < row
    tri = jnp.where(row > col, 1.0, 0.0).astype(BF16)
    zero_bf = jnp.zeros((blk, LANES), BF16)
    sign_bit = jnp.int32(-2 ** 31)

    q_heads = []
    for p in range(PAIRS_PER_STEP):
        qp = q_ref[:, p * LANES:(p + 1) * LANES]
        q_heads += [jnp.where(is_a, qp, zero_bf), jnp.where(is_a, zero_bf, qp)]

    def block(j, masked, first):
        r = pl.multiple_of(j * blk, blk)
        low = None
        for p in range(PAIRS_PER_STEP):
            kb = k_ref[pl.ds(r, blk), p * LANES:(p + 1) * LANES]
            vb = v_ref[pl.ds(r, blk), p * LANES:(p + 1) * LANES]
            v_heads = (jnp.where(is_a, vb, zero_bf), jnp.where(is_a, zero_bf, vb))
            pv = None
            for hd in range(2):
                n = 2 * p + hd
                z = lax.dot_general(q_heads[n], kb, (((1,), (1,)), ((), ())), preferred_element_type=F32)
                neg_abs = lax.bitcast_convert_type(lax.bitcast_convert_type(z, jnp.int32) | sign_bit, F32)
                sp = jnp.maximum(z, 0.0) + jnp.log(1.0 + jnp.exp2(neg_abs)) * LOG2E
                if masked:
                    sp = jnp.where(strict, sp, 0.0)
                later = _dot(sp.astype(BF16), tri)
                e = z - sp - later
                if not first:
                    stay_prev = stay_ref[n][:, 0:1]
                    e = e - stay_prev
                w = jnp.exp2(e)
                if masked:
                    w = jnp.where(strict, w, 0.0)
                d = _dot(w.astype(BF16), v_heads[hd])
                pv = d if pv is None else pv + d
                tot = jnp.sum(sp, axis=-1, keepdims=True)
                stay = tot if first else stay_prev + tot
                stay_ref[n] = jnp.broadcast_to(stay, (blk, LANES))
                low = stay if low is None else jnp.minimum(low, stay)
            acc_ref[p] = pv if first else acc_ref[p] + pv
        return jnp.min(low)

    low0 = block(i, True, True)

    def cond(c):
        j, low = c
        return jnp.logical_and(j >= 0, low < STAY_BITS_DONE)

    def body(c):
        j, _ = c
        return j - 1, block(j, False, False)

    lax.while_loop(cond, body, (i - 1, low0))
    for p in range(PAIRS_PER_STEP):
        ga = ga_ref[:, p * LANES:(p + 1) * LANES]
        o_ref[:, p * LANES:(p + 1) * LANES] = acc_ref[p] * (ga * _sigmoid(ga))


def _attention(qkv3, pf3):
    bsz, seq, _ = qkv3.shape
    blk = min(ATTN_BLOCK, seq)
    w = PAIRS_PER_STEP * LANES
    n_grp = HEAD_PAIRS // PAIRS_PER_STEP
    ga0 = (2 * SSM_WIDTH) // w
    kern = functools.partial(_attn_kernel, blk=blk)
    return pl.pallas_call(
        kern,
        grid=(bsz, n_grp, seq // blk),
        in_specs=[
            pl.BlockSpec((None, blk, w), lambda b, g, i: (b, i, g)),
            pl.BlockSpec((None, seq, w), lambda b, g, i: (b, 0, n_grp + g)),
            pl.BlockSpec((None, seq, w), lambda b, g, i: (b, 0, 2 * n_grp + g)),
            pl.BlockSpec((None, blk, w), lambda b, g, i: (b, i, ga0 + g)),
        ],
        out_specs=pl.BlockSpec((None, blk, w), lambda b, g, i: (b, i, g)),
        out_shape=jax.ShapeDtypeStruct((bsz, seq, ATTN_WIDTH), F32),
        scratch_shapes=[pltpu.VMEM((PAIRS_PER_STEP, blk, LANES), F32),
                        pltpu.VMEM((2 * PAIRS_PER_STEP, blk, LANES), F32)],
        compiler_params=pltpu.CompilerParams(
            dimension_semantics=("arbitrary", "arbitrary", "arbitrary"), vmem_limit_bytes=VMEM_LIMIT),
        name="attention",
    )(qkv3, qkv3, qkv3, pf3)


def _out_kernel(h_ref, ys_ref, u_ref, gs_ref, ya_ref, p_ref, d_ref, wg_ref, bg_ref, wo1_ref, wo2_ref,
                pg_ref, wpg_ref, wpp_ref, o_ref):
    y = ys_ref[...] + d_ref[...] * u_ref[...]
    z = 0.5 * y * (1.0 + jnp.tanh(math.sqrt(2.0 / math.pi) * (y + 0.044715 * (y * y * y))))
    zz = _dot(z.astype(BF16), wg_ref[...]) + bg_ref[...]
    val = zz[:, 0:SSM_WIDTH]
    gate = zz[:, SSM_WIDTH:2 * SSM_WIDTH]
    gs = gs_ref[...]
    y_ssm = (val * _sigmoid(gate)) * (gs * _sigmoid(gs))
    h = h_ref[...] + _dot(y_ssm.astype(BF16), wo1_ref[...]) + _dot(ya_ref[...].astype(BF16), wo2_ref[...])
    ms = jnp.mean(h * h, axis=-1, keepdims=True)
    hn = (h * lax.rsqrt(ms + RMS_EPS)) * pg_ref[...]
    ple_gate = _sigmoid(_dot(hn.astype(BF16), wpg_ref[...]))
    o_ref[...] = h + ple_gate * _dot(p_ref[...].astype(BF16), wpp_ref[...])


def _out(h, y_s, pf, y_att, p, d_skip, w_glu_bf, b_glu, w_out_bf, ple_g, w_pg_bf, w_pp_bf):
    t = h.shape[0]
    tm = min(TOKEN_TILE, t)

    def const(shape):
        return pl.BlockSpec(shape, lambda i: (0, 0))

    return pl.pallas_call(
        _out_kernel,
        grid=(t // tm,),
        in_specs=[
            pl.BlockSpec((tm, D_MODEL), lambda i: (i, 0)),
            pl.BlockSpec((tm, SSM_WIDTH), lambda i: (i, 0)),
            pl.BlockSpec((tm, SSM_WIDTH), lambda i: (i, 0)),
            pl.BlockSpec((tm, SSM_WIDTH), lambda i: (i, 1)),
            pl.BlockSpec((tm, ATTN_WIDTH), lambda i: (i, 0)),
            pl.BlockSpec((tm, PLE_DIM), lambda i: (i, 0)),
            const((1, SSM_WIDTH)),
            const((SSM_WIDTH, 2 * SSM_WIDTH)),
            const((1, 2 * SSM_WIDTH)),
            const((SSM_WIDTH, D_MODEL)),
            const((ATTN_WIDTH, D_MODEL)),
            const((1, D_MODEL)),
            const((D_MODEL, D_MODEL)),
            const((PLE_DIM, D_MODEL)),
        ],
        out_specs=pl.BlockSpec((tm, D_MODEL), lambda i: (i, 0)),
        out_shape=jax.ShapeDtypeStruct((t, D_MODEL), F32),
        compiler_params=pltpu.CompilerParams(
            dimension_semantics=("arbitrary",), vmem_limit_bytes=VMEM_LIMIT),
        name="out",
    )(h, y_s, pf, pf, y_att, p, d_skip.reshape(1, SSM_WIDTH), w_glu_bf, b_glu.reshape(1, 2 * SSM_WIDTH),
      w_out_bf[:SSM_WIDTH], w_out_bf[SSM_WIDTH:], ple_g.reshape(1, D_MODEL), w_pg_bf, w_pp_bf)


def kernel(x, p, mix_norm_g, w_in, ssm_a_re, ssm_a_im, ssm_log_dt, ssm_b_re, ssm_b_im, ssm_c_re, ssm_c_im,
           ssm_d, ssm_w_glu, ssm_b_glu, q_norm_g, k_norm_g, w_out, ple_norm_g, w_ple_gate, w_ple_proj):
    bsz, seq, _ = x.shape
    depth = w_in.shape[0]
    t = bsz * seq
    n_chunks = seq // SSM_CHUNK
    assert seq % SSM_CHUNK == 0 and seq % min(ATTN_BLOCK, seq) == 0 and t % min(TOKEN_TILE, t) == 0
    n_steps = max(1, (n_chunks - 1).bit_length())
    n_pw = -(-n_steps // 8) * 8

    w1, w2, pw = _ssm_setup(ssm_a_re, ssm_a_im, ssm_log_dt, ssm_b_re, ssm_b_im, ssm_c_re, ssm_c_im, n_pw, n_steps)
    a0 = 2 * SSM_WIDTH
    w_in_bf = jnp.concatenate(
        [w_in[:, :, :a0], w_in[:, :, a0 + 3 * ATTN_WIDTH:], w_in[:, :, a0:a0 + 3 * ATTN_WIDTH]], axis=-1
    ).astype(BF16)
    w_glu_bf = ssm_w_glu.astype(BF16)
    w_out_bf = w_out.astype(BF16)
    w_pg_bf = w_ple_gate.astype(BF16)
    w_pp_bf = w_ple_proj.astype(BF16)

    h = x.reshape(t, D_MODEL)
    for i in range(depth):
        gq2 = jnp.tile(q_norm_g[i].reshape(1, HEAD_DIM), (1, 2))
        gk2 = jnp.tile(k_norm_g[i].reshape(1, HEAD_DIM), (1, 2))
        pf, qkv = _inproj(h, mix_norm_g[i], w_in_bf[i], gq2, gk2)
        pf3 = pf.reshape(bsz, seq, F32_COLS)
        y_s = _ssm_main(pf3, w1[i], w2[i], pw[i], n_steps).reshape(t, SSM_WIDTH)
        y_att = _attention(qkv.reshape(bsz, seq, QKV_COLS), pf3).reshape(t, ATTN_WIDTH)
        h = _out(h, y_s, pf, y_att, p[i].reshape(t, PLE_DIM), ssm_d[i].reshape(SSM_WIDTH), w_glu_bf[i],
                 ssm_b_glu[i], w_out_bf[i], ple_norm_g[i], w_pg_bf[i], w_pp_bf[i])
    return h.reshape(bsz, seq, D_MODEL)
```

```python
import functools
import math

import jax
import jax.numpy as jnp
from jax import lax
from jax.experimental import pallas as pl
from jax.experimental.pallas import tpu as pltpu

F32 = jnp.float32
BF16 = jnp.bfloat16

D_MODEL = 1024
PLE_DIM = 256
SSM_WIDTH = 512
SSM_GROUP = 16
SSM_STATE = 64
LANES = 128
SLAB_GROUPS = LANES // SSM_GROUP
SSM_SLABS = SSM_WIDTH // LANES
SLAB_STATES = SLAB_GROUPS * SSM_STATE
SSM_CHUNK = 8
CHUNK_COLS = SSM_CHUNK * LANES
SCAN_RADIX = 8
ATTN_WIDTH = 512
HEAD_DIM = 64
HEAD_PAIRS = ATTN_WIDTH // LANES
PAIRS_PER_STEP = 4
IN_COLS = 3072
F32_COLS = 2 * SSM_WIDTH + ATTN_WIDTH
QKV_COLS = 3 * ATTN_WIDTH
RMS_EPS = 1e-6
LOG2E = 1.4426950408889634
ATTN_BLOCK = 256
STAY_BITS_DONE = 152.0
TOKEN_TILE = 512
OUT_SUBTILES = 2
VMEM_LIMIT = 56 * 1024 * 1024


def _sigmoid(x):
    return 1.0 / (1.0 + jnp.exp(-x))


def _dot(a, b):
    return jnp.dot(a, b, preferred_element_type=F32)


def _dot_exact(a, b):
    return jnp.dot(a, b, preferred_element_type=F32, precision=lax.Precision.HIGHEST)


def _head_mask():
    return lax.broadcasted_iota(jnp.int32, (1, LANES), 1) < HEAD_DIM


def _headnorm(x, g, is_a):
    x2 = x * x
    s_a = jnp.sum(jnp.where(is_a, x2, 0.0), axis=-1, keepdims=True)
    s_b = jnp.sum(jnp.where(is_a, 0.0, x2), axis=-1, keepdims=True)
    ms = jnp.where(is_a, s_a, s_b) * (1.0 / HEAD_DIM)
    return (x * lax.rsqrt(ms + RMS_EPS)) * g


def _inproj_kernel(x_ref, g_ref, w_ref, gq_ref, gk_ref, f_ref, qkv_ref):
    x = x_ref[...]
    ms = jnp.mean(x * x, axis=-1, keepdims=True)
    hn = (x * lax.rsqrt(ms + RMS_EPS)) * g_ref[...]
    res = _dot(hn.astype(BF16), w_ref[...])
    f_ref[...] = res[:, 0:F32_COLS]
    is_a = _head_mask()
    q_scale = (HEAD_DIM ** -0.5) * LOG2E
    for s in range(HEAD_PAIRS):
        c0 = F32_COLS + s * LANES
        q = _headnorm(res[:, c0:c0 + LANES], gq_ref[...], is_a) * q_scale
        qkv_ref[:, s * LANES:(s + 1) * LANES] = q.astype(BF16)
        c1 = c0 + ATTN_WIDTH
        k = _headnorm(res[:, c1:c1 + LANES], gk_ref[...], is_a)
        qkv_ref[:, ATTN_WIDTH + s * LANES:ATTN_WIDTH + (s + 1) * LANES] = k.astype(BF16)
    qkv_ref[:, 2 * ATTN_WIDTH:3 * ATTN_WIDTH] = res[:, F32_COLS + 2 * ATTN_WIDTH:IN_COLS].astype(BF16)


def _inproj(h, g, w_bf, gq2, gk2):
    t = h.shape[0]
    tm = min(TOKEN_TILE, t)
    return pl.pallas_call(
        _inproj_kernel,
        grid=(t // tm,),
        in_specs=[
            pl.BlockSpec((tm, D_MODEL), lambda i: (i, 0)),
            pl.BlockSpec((1, D_MODEL), lambda i: (0, 0)),
            pl.BlockSpec((D_MODEL, IN_COLS), lambda i: (0, 0)),
            pl.BlockSpec((1, LANES), lambda i: (0, 0)),
            pl.BlockSpec((1, LANES), lambda i: (0, 0)),
        ],
        out_specs=[pl.BlockSpec((tm, F32_COLS), lambda i: (i, 0)),
                   pl.BlockSpec((tm, QKV_COLS), lambda i: (i, 0))],
        out_shape=[jax.ShapeDtypeStruct((t, F32_COLS), F32), jax.ShapeDtypeStruct((t, QKV_COLS), BF16)],
        compiler_params=pltpu.CompilerParams(
            dimension_semantics=("arbitrary",), vmem_limit_bytes=VMEM_LIMIT),
        name="inproj",
    )(h, g.reshape(1, D_MODEL), w_bf, gq2, gk2)


def _ssm_setup_kernel(are_r, aim_r, ldt_r, are_c, aim_c, ldt_c, bre_ref, bim_ref, cre_ref, cim_ref,
                      w1_ref, w2_ref, pw_ref, *, n_pw, n_lvl2):
    ns = SLAB_STATES
    a_re = are_r[...]
    a_im = aim_r[...]
    dt = jnp.exp(ldt_r[...])
    lam_r = a_re * dt
    lam_i = a_im * dt
    mag = jnp.exp(lam_r)
    ab_re = mag * jnp.cos(lam_i)
    ab_im = mag * jnp.sin(lam_i)
    num_re = ab_re - 1.0
    num_im = ab_im
    den = a_re * a_re + a_im * a_im
    f_re = (num_re * a_re + num_im * a_im) / den
    f_im = (num_im * a_re - num_re * a_im) / den
    b_re = bre_ref[...]
    b_im = bim_ref[...]
    bb_re = jnp.tile(f_re * b_re - f_im * b_im, (SLAB_GROUPS, 1))
    bb_im = jnp.tile(f_re * b_im + f_im * b_re, (SLAB_GROUPS, 1))
    grp_row = lax.broadcasted_iota(jnp.int32, (LANES, ns), 0) >> 4
    grp_col = lax.broadcasted_iota(jnp.int32, (LANES, ns), 1) >> 6
    same_in = grp_row == grp_col
    zin = jnp.zeros((LANES, ns), F32)
    bb_re = jnp.where(same_in, bb_re, zin)
    bb_im = jnp.where(same_in, bb_im, zin)

    e_in = (SSM_CHUNK - 1 - lax.broadcasted_iota(jnp.int32, (SSM_CHUNK, 1), 0)).astype(F32)
    mag_in = jnp.exp(lam_r * e_in)
    pin_re = mag_in * jnp.cos(lam_i * e_in)
    pin_im = mag_in * jnp.sin(lam_i * e_in)
    for lp in range(SSM_CHUNK):
        p_re = pin_re[lp:lp + 1, :]
        p_im = pin_im[lp:lp + 1, :]
        w1_ref[lp * LANES:(lp + 1) * LANES, 0:ns] = (p_re * bb_re - p_im * bb_im).astype(BF16)
        w1_ref[lp * LANES:(lp + 1) * LANES, ns:2 * ns] = (p_re * bb_im + p_im * bb_re).astype(BF16)

    dt_c = jnp.exp(ldt_c[...])
    lam_rc = are_c[...] * dt_c
    lam_ic = aim_c[...] * dt_c
    lag = (lax.broadcasted_iota(jnp.int32, (1, LANES), 1) & (SSM_CHUNK - 1)).astype(F32)
    grp_row_o = lax.broadcasted_iota(jnp.int32, (ns, LANES), 0) >> 6
    grp_col_o = lax.broadcasted_iota(jnp.int32, (ns, LANES), 1) >> 4
    same_out = grp_row_o == grp_col_o
    zout = jnp.zeros((ns, LANES), F32)
    c_re = jnp.where(same_out, cre_ref[...], zout)
    c_im = jnp.where(same_out, cim_ref[...], zout)

    def powers(expo):
        mg = jnp.exp(lam_rc * (lag + expo))
        return mg * jnp.cos(lam_ic * (lag + expo)), mg * jnp.sin(lam_ic * (lag + expo))

    p1_re, p1_im = powers(1.0)
    p0_re, p0_im = powers(0.0)
    kf = []
    for l in range(SSM_CHUNK):
        q_re = jnp.broadcast_to(p1_re[:, l:l + 1], (ns, LANES))
        q_im = jnp.broadcast_to(p1_im[:, l:l + 1], (ns, LANES))
        w2_ref[0:ns, l * LANES:(l + 1) * LANES] = (c_re * q_re - c_im * q_im).astype(BF16)
        w2_ref[ns:2 * ns, l * LANES:(l + 1) * LANES] = (-c_re * q_im - c_im * q_re).astype(BF16)
        r_re = jnp.broadcast_to(p0_re[:, l:l + 1], (ns, LANES))
        r_im = jnp.broadcast_to(p0_im[:, l:l + 1], (ns, LANES))
        kf.append((_dot_exact(bb_re, c_re * r_re - c_im * r_im)
                   + _dot_exact(bb_im, -c_re * r_im - c_im * r_re)).astype(BF16))
    zblk = jnp.zeros((LANES, LANES), BF16)
    for lp in range(SSM_CHUNK):
        for l in range(SSM_CHUNK):
            w1_ref[lp * LANES:(lp + 1) * LANES, 2 * ns + l * LANES:2 * ns + (l + 1) * LANES] = (
                kf[l - lp] if l >= lp else zblk)

    prow = lax.broadcasted_iota(jnp.int32, (n_pw, 1), 0)
    dbl = jnp.clip(prow - SCAN_RADIX, 0, max(n_lvl2 - 1, 0))
    e_pw = jnp.where(prow < SCAN_RADIX, SSM_CHUNK * prow,
                     jnp.left_shift(SSM_CHUNK * SCAN_RADIX, dbl)).astype(F32)
    mag_pw = jnp.exp(lam_r * e_pw)
    pw_ref[:, 0:ns] = mag_pw * jnp.cos(lam_i * e_pw)
    pw_ref[:, ns:2 * ns] = mag_pw * jnp.sin(lam_i * e_pw)


def _ssm_setup(a_re, a_im, log_dt, b_re, b_im, c_re, c_im, n_pw, n_lvl2):
    depth = a_re.shape[0]
    g, p, h = a_re.shape[1], SSM_STATE, SSM_GROUP
    ns = SLAB_STATES
    ldt = jnp.broadcast_to(log_dt[:, :, None], (depth, g, p))

    def row(v):
        return v.reshape(depth, SSM_SLABS, 1, ns)

    def col(v):
        return v.reshape(depth, SSM_SLABS, ns, 1)

    def b_lay(v):
        return v.reshape(depth, SSM_SLABS, SLAB_GROUPS, p, h).transpose(0, 1, 4, 2, 3).reshape(
            depth, SSM_SLABS, h, ns)

    def c_lay(v):
        vt = v.reshape(depth, SSM_SLABS, SLAB_GROUPS, h, p).transpose(0, 1, 2, 4, 3).reshape(
            depth, SSM_SLABS, ns, h)
        return jnp.tile(vt, (1, 1, 1, SLAB_GROUPS))

    def spec(shape):
        return pl.BlockSpec((None, None) + shape, lambda d, q: (d, q) + (0,) * len(shape))

    return pl.pallas_call(
        functools.partial(_ssm_setup_kernel, n_pw=n_pw, n_lvl2=n_lvl2),
        grid=(depth, SSM_SLABS),
        in_specs=[spec((1, ns))] * 3 + [spec((ns, 1))] * 3 + [spec((h, ns))] * 2 + [spec((ns, LANES))] * 2,
        out_specs=[spec((CHUNK_COLS, 2 * ns + CHUNK_COLS)), spec((2 * ns, CHUNK_COLS)), spec((n_pw, 2 * ns))],
        out_shape=[
            jax.ShapeDtypeStruct((depth, SSM_SLABS, CHUNK_COLS, 2 * ns + CHUNK_COLS), BF16),
            jax.ShapeDtypeStruct((depth, SSM_SLABS, 2 * ns, CHUNK_COLS), BF16),
            jax.ShapeDtypeStruct((depth, SSM_SLABS, n_pw, 2 * ns), F32),
        ],
        compiler_params=pltpu.CompilerParams(
            dimension_semantics=("arbitrary", "arbitrary"), vmem_limit_bytes=VMEM_LIMIT),
        name="ssm_setup",
    )(row(a_re), row(a_im), row(ldt), col(a_re), col(a_im), col(ldt),
      b_lay(b_re), b_lay(b_im), c_lay(c_re), c_lay(c_im))


def _ssm_main_kernel(u_ref, w1_ref, w2_ref, pw_ref, y_ref, lhs_ref, z_scr, sp_scr, g_a, g_b,
                     *, n_chunks, n_lvl2, pad):
    ns = SLAB_STATES
    nsl = ns // LANES
    ng = n_chunks // SCAN_RADIX
    for l in range(SSM_CHUNK):
        lhs_ref[:, l * LANES:(l + 1) * LANES] = u_ref[pl.ds(l, n_chunks, stride=SSM_CHUNK), :].astype(BF16)
    zy = _dot(lhs_ref[...], w1_ref[...])
    for j in range(2 * nsl):
        z_scr[j] = zy[:, j * LANES:(j + 1) * LANES]

    for js in range(nsl):
        jr, ji = js, nsl + js
        lr = slice(js * LANES, (js + 1) * LANES)
        li = slice(ns + js * LANES, ns + (js + 1) * LANES)
        a_r = pw_ref[1:2, lr]
        a_i = pw_ref[1:2, li]
        xs = []
        for lo in range(SCAN_RADIX):
            z_r = z_scr[jr, pl.ds(lo, ng, stride=SCAN_RADIX), :]
            z_i = z_scr[ji, pl.ds(lo, ng, stride=SCAN_RADIX), :]
            if lo == 0:
                x_r, x_i = z_r, z_i
            else:
                x_r, x_i = a_r * x_r - a_i * x_i + z_r, a_r * x_i + a_i * x_r + z_i
            xs.append((x_r, x_i))
        ga, gb = g_a.at[js], g_b.at[js]
        zeros = jnp.zeros((pad, 2 * LANES), F32)
        ga[0:pad, :] = zeros
        gb[0:pad, :] = zeros
        ga[pad:pad + ng, 0:LANES] = x_r
        ga[pad:pad + ng, LANES:2 * LANES] = x_i
        src, dst = ga, gb
        for k in range(n_lvl2):
            d = 1 << k
            t_r = src[pad:pad + ng, 0:LANES]
            t_i = src[pad:pad + ng, LANES:2 * LANES]
            s_r = src[pad - d:pad - d + ng, 0:LANES]
            s_i = src[pad - d:pad - d + ng, LANES:2 * LANES]
            p_r = pw_ref[SCAN_RADIX + k:SCAN_RADIX + k + 1, lr]
            p_i = pw_ref[SCAN_RADIX + k:SCAN_RADIX + k + 1, li]
            dst[pad:pad + ng, 0:LANES] = t_r + p_r * s_r - p_i * s_i
            dst[pad:pad + ng, LANES:2 * LANES] = t_i + p_r * s_i + p_i * s_r
            src, dst = dst, src
        sg_r = src[pad - 1:pad - 1 + ng, 0:LANES]
        sg_i = src[pad - 1:pad - 1 + ng, LANES:2 * LANES]
        for lo in range(SCAN_RADIX):
            if lo == 0:
                s_r, s_i = sg_r, sg_i
            else:
                p_r = pw_ref[lo:lo + 1, lr]
                p_i = pw_ref[lo:lo + 1, li]
                s_r = p_r * sg_r - p_i * sg_i + xs[lo - 1][0]
                s_i = p_r * sg_i + p_i * sg_r + xs[lo - 1][1]
            sp_scr[jr, pl.ds(lo, ng, stride=SCAN_RADIX), :] = s_r
            sp_scr[ji, pl.ds(lo, ng, stride=SCAN_RADIX), :] = s_i

    s_prev = jnp.concatenate([sp_scr[j] for j in range(2 * nsl)], axis=1).astype(BF16)
    y = zy[:, 2 * ns:2 * ns + CHUNK_COLS] + _dot(s_prev, w2_ref[...])
    for l in range(SSM_CHUNK):
        y_ref[pl.ds(l, n_chunks, stride=SSM_CHUNK), :] = y[:, l * LANES:(l + 1) * LANES]


def _ssm_main(pf3, w1, w2, pw, n_lvl2):
    bsz, seq, _ = pf3.shape
    n_chunks = seq // SSM_CHUNK
    ns = SLAB_STATES
    nsl = ns // LANES
    ng = n_chunks // SCAN_RADIX
    pad = max(8, ng // 2)
    n_pw = pw.shape[1]
    kern = functools.partial(_ssm_main_kernel, n_chunks=n_chunks, n_lvl2=n_lvl2, pad=pad)
    return pl.pallas_call(
        kern,
        grid=(SSM_SLABS, bsz),
        in_specs=[
            pl.BlockSpec((None, seq, LANES), lambda s, b: (b, 0, s)),
            pl.BlockSpec((None, CHUNK_COLS, 2 * ns + CHUNK_COLS), lambda s, b: (s, 0, 0)),
            pl.BlockSpec((None, 2 * ns, CHUNK_COLS), lambda s, b: (s, 0, 0)),
            pl.BlockSpec((None, n_pw, 2 * ns), lambda s, b: (s, 0, 0)),
        ],
        out_specs=pl.BlockSpec((None, seq, LANES), lambda s, b: (b, 0, s)),
        out_shape=jax.ShapeDtypeStruct((bsz, seq, SSM_WIDTH), F32),
        scratch_shapes=[pltpu.VMEM((n_chunks, CHUNK_COLS), BF16),
                        pltpu.VMEM((2 * nsl, n_chunks, LANES), F32),
                        pltpu.VMEM((2 * nsl, n_chunks, LANES), F32),
                        pltpu.VMEM((nsl, pad + ng, 2 * LANES), F32),
                        pltpu.VMEM((nsl, pad + ng, 2 * LANES), F32)],
        compiler_params=pltpu.CompilerParams(
            dimension_semantics=("arbitrary", "arbitrary"), vmem_limit_bytes=VMEM_LIMIT),
        name="ssm_main",
    )(pf3, w1, w2, pw)


def _attn_kernel(q_ref, k_ref, v_ref, ga_ref, o_ref, acc_ref, stay_ref, *, blk):
    i = pl.program_id(2)
    is_a = _head_mask()
    row = lax.broadcasted_iota(jnp.int32, (blk, blk), 0)
    col = lax.broadcasted_iota(jnp.int32, (blk, blk), 1)
    strict = col < row
    tri = jnp.where(row > col, 1.0, 0.0).astype(BF16)
    zero_bf = jnp.zeros((blk, LANES), BF16)

    q_heads = []
    for p in range(PAIRS_PER_STEP):
        qp = q_ref[:, p * LANES:(p + 1) * LANES]
        q_heads += [jnp.where(is_a, qp, zero_bf), jnp.where(is_a, zero_bf, qp)]

    n_heads = 2 * PAIRS_PER_STEP

    def block(j, masked, first):
        r = pl.multiple_of(j * blk, blk)
        kbs = [k_ref[pl.ds(r, blk), p * LANES:(p + 1) * LANES] for p in range(PAIRS_PER_STEP)]
        zs = [lax.dot_general(q_heads[n], kbs[n // 2], (((1,), (1,)), ((), ())), preferred_element_type=F32)
              for n in range(n_heads)]
        sps, lefts, stays = [], [], []
        low = None
        for n in range(n_heads):
            z = zs[n]
            sp = jnp.maximum(z, 0.0) + jnp.log(1.0 + jnp.exp2(-jnp.abs(z))) * LOG2E
            if masked:
                sp = jnp.where(strict, sp, 0.0)
            tot = jnp.sum(sp, axis=-1, keepdims=True)
            left = z - sp
            if first:
                stay = tot
            else:
                stay_prev = stay_ref[n][:, 0:1]
                left = left - stay_prev
                stay = stay_prev + tot
            low = stay if low is None else jnp.minimum(low, stay)
            sps.append(sp.astype(BF16))
            lefts.append(left)
            stays.append(stay)
        low = jnp.min(low)
        laters = [_dot(sps[n], tri) for n in range(n_heads)]
        ws = []
        for n in range(n_heads):
            w = jnp.exp2(lefts[n] - laters[n])
            if masked:
                w = jnp.where(strict, w, 0.0)
            ws.append(w.astype(BF16))
        for p in range(PAIRS_PER_STEP):
            vb = v_ref[pl.ds(r, blk), p * LANES:(p + 1) * LANES]
            pv = _dot(ws[2 * p], jnp.where(is_a, vb, zero_bf)) + _dot(ws[2 * p + 1], jnp.where(is_a, zero_bf, vb))
            acc_ref[p] = pv if first else acc_ref[p] + pv
        for n in range(n_heads):
            stay_ref[n] = jnp.broadcast_to(stays[n], (blk, LANES))
        return low

    low0 = block(i, True, True)

    def cond(c):
        j, low = c
        return jnp.logical_and(j >= 0, low < STAY_BITS_DONE)

    def body(c):
        j, _ = c
        return j - 1, block(j, False, False)

    lax.while_loop(cond, body, (i - 1, low0))
    for p in range(PAIRS_PER_STEP):
        ga = ga_ref[:, p * LANES:(p + 1) * LANES]
        o_ref[:, p * LANES:(p + 1) * LANES] = acc_ref[p] * (ga * _sigmoid(ga))


def _attention(qkv3, pf3):
    bsz, seq, _ = qkv3.shape
    blk = min(ATTN_BLOCK, seq)
    w = PAIRS_PER_STEP * LANES
    n_grp = HEAD_PAIRS // PAIRS_PER_STEP
    ga0 = (2 * SSM_WIDTH) // w
    kern = functools.partial(_attn_kernel, blk=blk)
    return pl.pallas_call(
        kern,
        grid=(bsz, n_grp, seq // blk),
        in_specs=[
            pl.BlockSpec((None, blk, w), lambda b, g, i: (b, i, g)),
            pl.BlockSpec((None, seq, w), lambda b, g, i: (b, 0, n_grp + g)),
            pl.BlockSpec((None, seq, w), lambda b, g, i: (b, 0, 2 * n_grp + g)),
            pl.BlockSpec((None, blk, w), lambda b, g, i: (b, i, ga0 + g)),
        ],
        out_specs=pl.BlockSpec((None, blk, w), lambda b, g, i: (b, i, g)),
        out_shape=jax.ShapeDtypeStruct((bsz, seq, ATTN_WIDTH), F32),
        scratch_shapes=[pltpu.VMEM((PAIRS_PER_STEP, blk, LANES), F32),
                        pltpu.VMEM((2 * PAIRS_PER_STEP, blk, LANES), F32)],
        compiler_params=pltpu.CompilerParams(
            dimension_semantics=("arbitrary", "arbitrary", "arbitrary"), vmem_limit_bytes=VMEM_LIMIT),
        name="attention",
    )(qkv3, qkv3, qkv3, pf3)


def _out_kernel(h_ref, ys_ref, u_ref, gs_ref, ya_ref, p_ref, d_ref, wg_ref, bg_ref, wo1_ref, wo2_ref,
                pg_ref, wpg_ref, wpp_ref, o_ref):
    rows = h_ref.shape[0] // OUT_SUBTILES
    subs = [slice(i * rows, (i + 1) * rows) for i in range(OUT_SUBTILES)]
    zs = []
    for sl in subs:
        y = ys_ref[sl, :] + d_ref[...] * u_ref[sl, :]
        z = 0.5 * y * (1.0 + jnp.tanh(math.sqrt(2.0 / math.pi) * (y + 0.044715 * (y * y * y))))
        zs.append(z.astype(BF16))
    zzs = [_dot(z, wg_ref[...]) + bg_ref[...] for z in zs]
    y_ssms = []
    for sl, zz in zip(subs, zzs):
        gs = gs_ref[sl, :]
        y_ssm = (zz[:, 0:SSM_WIDTH] * _sigmoid(zz[:, SSM_WIDTH:2 * SSM_WIDTH])) * (gs * _sigmoid(gs))
        y_ssms.append(y_ssm.astype(BF16))
    hs = [h_ref[sl, :] + _dot(y_ssm, wo1_ref[...]) + _dot(ya_ref[sl, :].astype(BF16), wo2_ref[...])
          for sl, y_ssm in zip(subs, y_ssms)]
    hns = []
    for h in hs:
        ms = jnp.mean(h * h, axis=-1, keepdims=True)
        hns.append(((h * lax.rsqrt(ms + RMS_EPS)) * pg_ref[...]).astype(BF16))
    gates = [_sigmoid(_dot(hn, wpg_ref[...])) for hn in hns]
    for sl, h, gate in zip(subs, hs, gates):
        o_ref[sl, :] = h + gate * _dot(p_ref[sl, :].astype(BF16), wpp_ref[...])


def _out(h, y_s, pf, y_att, p, d_skip, w_glu_bf, b_glu, w_out_bf, ple_g, w_pg_bf, w_pp_bf):
    t = h.shape[0]
    tm = min(TOKEN_TILE, t)

    def const(shape):
        return pl.BlockSpec(shape, lambda i: (0, 0))

    return pl.pallas_call(
        _out_kernel,
        grid=(t // tm,),
        in_specs=[
            pl.BlockSpec((tm, D_MODEL), lambda i: (i, 0)),
            pl.BlockSpec((tm, SSM_WIDTH), lambda i: (i, 0)),
            pl.BlockSpec((tm, SSM_WIDTH), lambda i: (i, 0)),
            pl.BlockSpec((tm, SSM_WIDTH), lambda i: (i, 1)),
            pl.BlockSpec((tm, ATTN_WIDTH), lambda i: (i, 0)),
            pl.BlockSpec((tm, PLE_DIM), lambda i: (i, 0)),
            const((1, SSM_WIDTH)),
            const((SSM_WIDTH, 2 * SSM_WIDTH)),
            const((1, 2 * SSM_WIDTH)),
            const((SSM_WIDTH, D_MODEL)),
            const((ATTN_WIDTH, D_MODEL)),
            const((1, D_MODEL)),
            const((D_MODEL, D_MODEL)),
            const((PLE_DIM, D_MODEL)),
        ],
        out_specs=pl.BlockSpec((tm, D_MODEL), lambda i: (i, 0)),
        out_shape=jax.ShapeDtypeStruct((t, D_MODEL), F32),
        compiler_params=pltpu.CompilerParams(
            dimension_semantics=("arbitrary",), vmem_limit_bytes=VMEM_LIMIT),
        name="out",
    )(h, y_s, pf, pf, y_att, p, d_skip.reshape(1, SSM_WIDTH), w_glu_bf, b_glu.reshape(1, 2 * SSM_WIDTH),
      w_out_bf[:SSM_WIDTH], w_out_bf[SSM_WIDTH:], ple_g.reshape(1, D_MODEL), w_pg_bf, w_pp_bf)


def kernel(x, p, mix_norm_g, w_in, ssm_a_re, ssm_a_im, ssm_log_dt, ssm_b_re, ssm_b_im, ssm_c_re, ssm_c_im,
           ssm_d, ssm_w_glu, ssm_b_glu, q_norm_g, k_norm_g, w_out, ple_norm_g, w_ple_gate, w_ple_proj):
    bsz, seq, _ = x.shape
    depth = w_in.shape[0]
    t = bsz * seq
    n_chunks = seq // SSM_CHUNK
    assert seq % SSM_CHUNK == 0 and seq % min(ATTN_BLOCK, seq) == 0 and t % min(TOKEN_TILE, t) == 0
    n_groups = n_chunks // SCAN_RADIX
    assert n_chunks % SCAN_RADIX == 0 and n_groups & (n_groups - 1) == 0
    n_lvl2 = n_groups.bit_length() - 1
    n_pw = -(-(SCAN_RADIX + n_lvl2) // 8) * 8

    w1, w2, pw = _ssm_setup(ssm_a_re, ssm_a_im, ssm_log_dt, ssm_b_re, ssm_b_im, ssm_c_re, ssm_c_im, n_pw, n_lvl2)
    a0 = 2 * SSM_WIDTH
    w_in_bf = jnp.concatenate(
        [w_in[:, :, :a0], w_in[:, :, a0 + 3 * ATTN_WIDTH:], w_in[:, :, a0:a0 + 3 * ATTN_WIDTH]], axis=-1
    ).astype(BF16)
    w_glu_bf = ssm_w_glu.astype(BF16)
    w_out_bf = w_out.astype(BF16)
    w_pg_bf = w_ple_gate.astype(BF16)
    w_pp_bf = w_ple_proj.astype(BF16)

    h = x.reshape(t, D_MODEL)
    for i in range(depth):
        gq2 = jnp.tile(q_norm_g[i].reshape(1, HEAD_DIM), (1, 2))
        gk2 = jnp.tile(k_norm_g[i].reshape(1, HEAD_DIM), (1, 2))
        pf, qkv = _inproj(h, mix_norm_g[i], w_in_bf[i], gq2, gk2)
        pf3 = pf.reshape(bsz, seq, F32_COLS)
        y_s = _ssm_main(pf3, w1[i], w2[i], pw[i], n_lvl2).reshape(t, SSM_WIDTH)
        y_att = _attention(qkv.reshape(bsz, seq, QKV_COLS), pf3).reshape(t, ATTN_WIDTH)
        h = _out(h, y_s, pf, y_att, p[i].reshape(t, PLE_DIM), ssm_d[i].reshape(SSM_WIDTH), w_glu_bf[i],
                 ssm_b_glu[i], w_out_bf[i], ple_norm_g[i], w_pg_bf[i], w_pp_bf[i])
    return h.reshape(bsz, seq, D_MODEL)
```

```python
import functools
import math

import jax
import jax.numpy as jnp
from jax import lax
from jax.experimental import pallas as pl
from jax.experimental.pallas import tpu as pltpu

F32 = jnp.float32
BF16 = jnp.bfloat16

D_MODEL = 1024
PLE_DIM = 256
SSM_WIDTH = 512
SSM_GROUP = 16
SSM_STATE = 64
LANES = 128
MXU_TILE = 256
SLAB_GROUPS = LANES // SSM_GROUP
SSM_SLABS = SSM_WIDTH // LANES
SLAB_STATES = SLAB_GROUPS * SSM_STATE
SSM_CHUNK = 8
CHUNK_COLS = SSM_CHUNK * LANES
SCAN_RADIX = 8
ATTN_WIDTH = 512
HEAD_DIM = 64
HEAD_PAIRS = ATTN_WIDTH // LANES
PAIRS_PER_STEP = 4
IN_COLS = 3072
F32_COLS = 2 * SSM_WIDTH + ATTN_WIDTH
QKV_COLS = 3 * ATTN_WIDTH
RMS_EPS = 1e-6
LOG2E = 1.4426950408889634
ATTN_QUERY_BLOCK = 128
ATTN_KEY_WINDOW = 256
STAY_BITS_DONE = 152.0
TOKEN_TILE = 512
OUT_SUBTILES = 2
VMEM_LIMIT = 56 * 1024 * 1024


def _sigmoid(x):
    return 1.0 / (1.0 + jnp.exp(-x))


def _dot(a, b):
    return jnp.dot(a, b, preferred_element_type=F32)


def _dot_exact(a, b):
    return jnp.dot(a, b, preferred_element_type=F32, precision=lax.Precision.HIGHEST)


def _head_mask():
    return lax.broadcasted_iota(jnp.int32, (1, LANES), 1) < HEAD_DIM


def _headnorm(x, g, is_a):
    x2 = x * x
    s_a = jnp.sum(jnp.where(is_a, x2, 0.0), axis=-1, keepdims=True)
    s_b = jnp.sum(jnp.where(is_a, 0.0, x2), axis=-1, keepdims=True)
    ms = jnp.where(is_a, s_a, s_b) * (1.0 / HEAD_DIM)
    return (x * lax.rsqrt(ms + RMS_EPS)) * g


def _inproj_kernel(x_ref, g_ref, w_ref, gq_ref, gk_ref, f_ref, qkv_ref):
    x = x_ref[...]
    ms = jnp.mean(x * x, axis=-1, keepdims=True)
    hn = (x * lax.rsqrt(ms + RMS_EPS)) * g_ref[...]
    res = _dot(hn.astype(BF16), w_ref[...])
    f_ref[...] = res[:, 0:F32_COLS]
    is_a = _head_mask()
    q_scale = (HEAD_DIM ** -0.5) * LOG2E
    for s in range(HEAD_PAIRS):
        c0 = F32_COLS + s * LANES
        q = _headnorm(res[:, c0:c0 + LANES], gq_ref[...], is_a) * q_scale
        qkv_ref[:, s * LANES:(s + 1) * LANES] = q.astype(BF16)
        c1 = c0 + ATTN_WIDTH
        k = _headnorm(res[:, c1:c1 + LANES], gk_ref[...], is_a)
        qkv_ref[:, ATTN_WIDTH + s * LANES:ATTN_WIDTH + (s + 1) * LANES] = k.astype(BF16)
    qkv_ref[:, 2 * ATTN_WIDTH:3 * ATTN_WIDTH] = res[:, F32_COLS + 2 * ATTN_WIDTH:IN_COLS].astype(BF16)


def _inproj(h, g, w_bf, gq2, gk2):
    t = h.shape[0]
    tm = min(TOKEN_TILE, t)
    return pl.pallas_call(
        _inproj_kernel,
        grid=(t // tm,),
        in_specs=[
            pl.BlockSpec((tm, D_MODEL), lambda i: (i, 0)),
            pl.BlockSpec((1, D_MODEL), lambda i: (0, 0)),
            pl.BlockSpec((D_MODEL, IN_COLS), lambda i: (0, 0)),
            pl.BlockSpec((1, LANES), lambda i: (0, 0)),
            pl.BlockSpec((1, LANES), lambda i: (0, 0)),
        ],
        out_specs=[pl.BlockSpec((tm, F32_COLS), lambda i: (i, 0)),
                   pl.BlockSpec((tm, QKV_COLS), lambda i: (i, 0))],
        out_shape=[jax.ShapeDtypeStruct((t, F32_COLS), F32), jax.ShapeDtypeStruct((t, QKV_COLS), BF16)],
        compiler_params=pltpu.CompilerParams(
            dimension_semantics=("arbitrary",), vmem_limit_bytes=VMEM_LIMIT),
        name="inproj",
    )(h, g.reshape(1, D_MODEL), w_bf, gq2, gk2)


def _ssm_setup_kernel(are_r, aim_r, ldt_r, are_c, aim_c, ldt_c, bre_ref, bim_ref, cre_ref, cim_ref,
                      w1_ref, w2_ref, pw_ref, *, n_pw, n_lvl2):
    ns = SLAB_STATES
    a_re = are_r[...]
    a_im = aim_r[...]
    dt = jnp.exp(ldt_r[...])
    lam_r = a_re * dt
    lam_i = a_im * dt
    mag = jnp.exp(lam_r)
    ab_re = mag * jnp.cos(lam_i)
    ab_im = mag * jnp.sin(lam_i)
    num_re = ab_re - 1.0
    num_im = ab_im
    den = a_re * a_re + a_im * a_im
    f_re = (num_re * a_re + num_im * a_im) / den
    f_im = (num_im * a_re - num_re * a_im) / den
    b_re = bre_ref[...]
    b_im = bim_ref[...]
    bb_re = jnp.tile(f_re * b_re - f_im * b_im, (SLAB_GROUPS, 1))
    bb_im = jnp.tile(f_re * b_im + f_im * b_re, (SLAB_GROUPS, 1))
    grp_row = lax.broadcasted_iota(jnp.int32, (LANES, ns), 0) >> 4
    grp_col = lax.broadcasted_iota(jnp.int32, (LANES, ns), 1) >> 6
    same_in = grp_row == grp_col
    zin = jnp.zeros((LANES, ns), F32)
    bb_re = jnp.where(same_in, bb_re, zin)
    bb_im = jnp.where(same_in, bb_im, zin)

    e_in = (SSM_CHUNK - 1 - lax.broadcasted_iota(jnp.int32, (SSM_CHUNK, 1), 0)).astype(F32)
    mag_in = jnp.exp(lam_r * e_in)
    pin_re = mag_in * jnp.cos(lam_i * e_in)
    pin_im = mag_in * jnp.sin(lam_i * e_in)
    for lp in range(SSM_CHUNK):
        p_re = pin_re[lp:lp + 1, :]
        p_im = pin_im[lp:lp + 1, :]
        w1_ref[lp * LANES:(lp + 1) * LANES, 0:ns] = (p_re * bb_re - p_im * bb_im).astype(BF16)
        w1_ref[lp * LANES:(lp + 1) * LANES, ns:2 * ns] = (p_re * bb_im + p_im * bb_re).astype(BF16)

    dt_c = jnp.exp(ldt_c[...])
    lam_rc = are_c[...] * dt_c
    lam_ic = aim_c[...] * dt_c
    lag = (lax.broadcasted_iota(jnp.int32, (1, LANES), 1) & (2 * SSM_CHUNK - 1)).astype(F32)
    grp_row_o = lax.broadcasted_iota(jnp.int32, (ns, LANES), 0) >> 6
    grp_col_o = lax.broadcasted_iota(jnp.int32, (ns, LANES), 1) >> 4
    same_out = grp_row_o == grp_col_o
    zout = jnp.zeros((ns, LANES), F32)
    c_re = jnp.where(same_out, cre_ref[...], zout)
    c_im = jnp.where(same_out, cim_ref[...], zout)

    mag_c = jnp.exp(lam_rc * lag)
    pc_re = mag_c * jnp.cos(lam_ic * lag)
    pc_im = mag_c * jnp.sin(lam_ic * lag)
    c_pow = []
    for l in range(SSM_CHUNK + 1):
        q_re = jnp.broadcast_to(pc_re[:, l:l + 1], (ns, LANES))
        q_im = jnp.broadcast_to(pc_im[:, l:l + 1], (ns, LANES))
        c_pow.append((c_re * q_re - c_im * q_im, -c_re * q_im - c_im * q_re))
    kf = []
    for l in range(SSM_CHUNK):
        w2_ref[0:ns, l * LANES:(l + 1) * LANES] = c_pow[l + 1][0].astype(BF16)
        w2_ref[ns:2 * ns, l * LANES:(l + 1) * LANES] = c_pow[l + 1][1].astype(BF16)
        kf.append((_dot_exact(bb_re, c_pow[l][0]) + _dot_exact(bb_im, c_pow[l][1])).astype(BF16))
    zblk = jnp.zeros((LANES, LANES), BF16)
    for lp in range(SSM_CHUNK):
        for l in range(SSM_CHUNK):
            w1_ref[lp * LANES:(lp + 1) * LANES, 2 * ns + l * LANES:2 * ns + (l + 1) * LANES] = (
                kf[l - lp] if l >= lp else zblk)

    prow = lax.broadcasted_iota(jnp.int32, (n_pw, 1), 0)
    dbl = jnp.clip(prow - SCAN_RADIX, 0, max(n_lvl2 - 1, 0))
    e_pw = jnp.where(prow < SCAN_RADIX, SSM_CHUNK * prow,
                     jnp.left_shift(SSM_CHUNK * SCAN_RADIX, dbl)).astype(F32)
    mag_pw = jnp.exp(lam_r * e_pw)
    pw_ref[:, 0:ns] = mag_pw * jnp.cos(lam_i * e_pw)
    pw_ref[:, ns:2 * ns] = mag_pw * jnp.sin(lam_i * e_pw)


def _ssm_setup(a_re, a_im, log_dt, b_re, b_im, c_re, c_im, n_pw, n_lvl2):
    depth = a_re.shape[0]
    g, p, h = a_re.shape[1], SSM_STATE, SSM_GROUP
    ns = SLAB_STATES
    ldt = jnp.broadcast_to(log_dt[:, :, None], (depth, g, p))

    def row(v):
        return v.reshape(depth, SSM_SLABS, 1, ns)

    def col(v):
        return v.reshape(depth, SSM_SLABS, ns, 1)

    def b_lay(v):
        return v.reshape(depth, SSM_SLABS, SLAB_GROUPS, p, h).transpose(0, 1, 4, 2, 3).reshape(
            depth, SSM_SLABS, h, ns)

    def c_lay(v):
        vt = v.reshape(depth, SSM_SLABS, SLAB_GROUPS, h, p).transpose(0, 1, 2, 4, 3).reshape(
            depth, SSM_SLABS, ns, h)
        return jnp.tile(vt, (1, 1, 1, SLAB_GROUPS))

    def spec(shape):
        return pl.BlockSpec((None, None) + shape, lambda d, q: (d, q) + (0,) * len(shape))

    return pl.pallas_call(
        functools.partial(_ssm_setup_kernel, n_pw=n_pw, n_lvl2=n_lvl2),
        grid=(depth, SSM_SLABS),
        in_specs=[spec((1, ns))] * 3 + [spec((ns, 1))] * 3 + [spec((h, ns))] * 2 + [spec((ns, LANES))] * 2,
        out_specs=[spec((CHUNK_COLS, 2 * ns + CHUNK_COLS)), spec((2 * ns, CHUNK_COLS)), spec((n_pw, 2 * ns))],
        out_shape=[
            jax.ShapeDtypeStruct((depth, SSM_SLABS, CHUNK_COLS, 2 * ns + CHUNK_COLS), BF16),
            jax.ShapeDtypeStruct((depth, SSM_SLABS, 2 * ns, CHUNK_COLS), BF16),
            jax.ShapeDtypeStruct((depth, SSM_SLABS, n_pw, 2 * ns), F32),
        ],
        compiler_params=pltpu.CompilerParams(
            dimension_semantics=("arbitrary", "arbitrary"), vmem_limit_bytes=VMEM_LIMIT),
        name="ssm_setup",
    )(row(a_re), row(a_im), row(ldt), col(a_re), col(a_im), col(ldt),
      b_lay(b_re), b_lay(b_im), c_lay(c_re), c_lay(c_im))


def _ssm_main_kernel(u_ref, w1_ref, w2_ref, pw_ref, y_ref, lhs_ref, z_scr, sp_scr, g_a, g_b,
                     *, n_chunks, n_lvl2, pad):
    ns = SLAB_STATES
    nsl = ns // LANES
    ng = n_chunks // SCAN_RADIX
    for l in range(SSM_CHUNK):
        lhs_ref[:, l * LANES:(l + 1) * LANES] = u_ref[pl.ds(l, n_chunks, stride=SSM_CHUNK), :].astype(BF16)
    zin = _dot(lhs_ref[...], w1_ref[:, 0:2 * ns])
    for j in range(2 * nsl):
        z_scr[j] = zin[:, j * LANES:(j + 1) * LANES]
    y_intra = [_dot(lhs_ref[:, 0:(c + 1) * MXU_TILE],
                    w1_ref[0:(c + 1) * MXU_TILE, 2 * ns + c * MXU_TILE:2 * ns + (c + 1) * MXU_TILE])
               for c in range(CHUNK_COLS // MXU_TILE)]

    for js in range(nsl):
        jr, ji = js, nsl + js
        lr = slice(js * LANES, (js + 1) * LANES)
        li = slice(ns + js * LANES, ns + (js + 1) * LANES)
        a_r = pw_ref[1:2, lr]
        a_i = pw_ref[1:2, li]
        xs = []
        for lo in range(SCAN_RADIX):
            z_r = z_scr[jr, pl.ds(lo, ng, stride=SCAN_RADIX), :]
            z_i = z_scr[ji, pl.ds(lo, ng, stride=SCAN_RADIX), :]
            if lo == 0:
                x_r, x_i = z_r, z_i
            else:
                x_r, x_i = a_r * x_r - a_i * x_i + z_r, a_r * x_i + a_i * x_r + z_i
            xs.append((x_r, x_i))
        ga, gb = g_a.at[js], g_b.at[js]
        zeros = jnp.zeros((pad, 2 * LANES), F32)
        ga[0:pad, :] = zeros
        gb[0:pad, :] = zeros
        ga[pad:pad + ng, 0:LANES] = x_r
        ga[pad:pad + ng, LANES:2 * LANES] = x_i
        src, dst = ga, gb
        for k in range(n_lvl2):
            d = 1 << k
            t_r = src[pad:pad + ng, 0:LANES]
            t_i = src[pad:pad + ng, LANES:2 * LANES]
            s_r = src[pad - d:pad - d + ng, 0:LANES]
            s_i = src[pad - d:pad - d + ng, LANES:2 * LANES]
            p_r = pw_ref[SCAN_RADIX + k:SCAN_RADIX + k + 1, lr]
            p_i = pw_ref[SCAN_RADIX + k:SCAN_RADIX + k + 1, li]
            dst[pad:pad + ng, 0:LANES] = t_r + p_r * s_r - p_i * s_i
            dst[pad:pad + ng, LANES:2 * LANES] = t_i + p_r * s_i + p_i * s_r
            src, dst = dst, src
        sg_r = src[pad - 1:pad - 1 + ng, 0:LANES]
        sg_i = src[pad - 1:pad - 1 + ng, LANES:2 * LANES]
        for lo in range(SCAN_RADIX):
            if lo == 0:
                s_r, s_i = sg_r, sg_i
            else:
                p_r = pw_ref[lo:lo + 1, lr]
                p_i = pw_ref[lo:lo + 1, li]
                s_r = p_r * sg_r - p_i * sg_i + xs[lo - 1][0]
                s_i = p_r * sg_i + p_i * sg_r + xs[lo - 1][1]
            sp_scr[jr, pl.ds(lo, ng, stride=SCAN_RADIX), :] = s_r
            sp_scr[ji, pl.ds(lo, ng, stride=SCAN_RADIX), :] = s_i

    s_prev = jnp.concatenate([sp_scr[j] for j in range(2 * nsl)], axis=1).astype(BF16)
    y = jnp.concatenate(y_intra, axis=1) + _dot(s_prev, w2_ref[...])
    for l in range(SSM_CHUNK):
        y_ref[pl.ds(l, n_chunks, stride=SSM_CHUNK), :] = y[:, l * LANES:(l + 1) * LANES]


def _ssm_main(pf3, w1, w2, pw, n_lvl2):
    bsz, seq, _ = pf3.shape
    n_chunks = seq // SSM_CHUNK
    ns = SLAB_STATES
    nsl = ns // LANES
    ng = n_chunks // SCAN_RADIX
    pad = max(8, ng // 2)
    n_pw = pw.shape[1]
    kern = functools.partial(_ssm_main_kernel, n_chunks=n_chunks, n_lvl2=n_lvl2, pad=pad)
    return pl.pallas_call(
        kern,
        grid=(SSM_SLABS, bsz),
        in_specs=[
            pl.BlockSpec((None, seq, LANES), lambda s, b: (b, 0, s)),
            pl.BlockSpec((None, CHUNK_COLS, 2 * ns + CHUNK_COLS), lambda s, b: (s, 0, 0)),
            pl.BlockSpec((None, 2 * ns, CHUNK_COLS), lambda s, b: (s, 0, 0)),
            pl.BlockSpec((None, n_pw, 2 * ns), lambda s, b: (s, 0, 0)),
        ],
        out_specs=pl.BlockSpec((None, seq, LANES), lambda s, b: (b, 0, s)),
        out_shape=jax.ShapeDtypeStruct((bsz, seq, SSM_WIDTH), F32),
        scratch_shapes=[pltpu.VMEM((n_chunks, CHUNK_COLS), BF16),
                        pltpu.VMEM((2 * nsl, n_chunks, LANES), F32),
                        pltpu.VMEM((2 * nsl, n_chunks, LANES), F32),
                        pltpu.VMEM((nsl, pad + ng, 2 * LANES), F32),
                        pltpu.VMEM((nsl, pad + ng, 2 * LANES), F32)],
        compiler_params=pltpu.CompilerParams(
            dimension_semantics=("arbitrary", "arbitrary"), vmem_limit_bytes=VMEM_LIMIT),
        name="ssm_main",
    )(pf3, w1, w2, pw)


def _attn_kernel(q_ref, k_ref, v_ref, ga_ref, tri_ref, o_ref, acc_ref, stay_ref, *, qb, kw):
    i = pl.program_id(2)
    q0 = pl.multiple_of(i * qb, qb)
    is_a = _head_mask()
    n_heads = 2 * PAIRS_PER_STEP
    d_row = lax.broadcasted_iota(jnp.int32, (qb, qb), 0)
    d_col = lax.broadcasted_iota(jnp.int32, (qb, qb), 1)
    strict = d_col < d_row
    w_col = lax.broadcasted_iota(jnp.int32, (qb, kw), 1)
    tris = {kw: tri_ref[...], qb: tri_ref[0:qb, 0:qb]}
    zero_q = jnp.zeros((qb, LANES), BF16)

    q_heads = []
    for p in range(PAIRS_PER_STEP):
        qp = q_ref[:, p * LANES:(p + 1) * LANES]
        q_heads += [jnp.where(is_a, qp, zero_q), jnp.where(is_a, zero_q, qp)]

    def sweep(parts, stay):
        stay = list(stay)
        zs = []
        for k0, width, _ in parts:
            kbs = [k_ref[pl.ds(k0, width), p * LANES:(p + 1) * LANES] for p in range(PAIRS_PER_STEP)]
            zs.append([lax.dot_general(q_heads[n], kbs[n // 2], (((1,), (1,)), ((), ())),
                                       preferred_element_type=F32) for n in range(n_heads)])
        sps, lefts = [], []
        for (_, _, mask), zp in zip(parts, zs):
            sp_p, left_p = [], []
            for n in range(n_heads):
                z = zp[n]
                sp = jnp.maximum(z, 0.0) + jnp.log(1.0 + jnp.exp2(-jnp.abs(z))) * LOG2E
                if mask is not None:
                    sp = jnp.where(mask, sp, 0.0)
                tot = jnp.sum(sp, axis=-1, keepdims=True)
                left = z - sp
                if stay[n] is None:
                    stay[n] = tot
                else:
                    left = left - stay[n]
                    stay[n] = stay[n] + tot
                sp_p.append(sp.astype(BF16))
                left_p.append(left)
            sps.append(sp_p)
            lefts.append(left_p)
        laters = [[_dot(sp, tris[width]) for sp in sp_p] for (_, width, _), sp_p in zip(parts, sps)]
        ws = []
        for (_, _, mask), left_p, later_p in zip(parts, lefts, laters):
            w_p = []
            for n in range(n_heads):
                w = jnp.exp2(left_p[n] - later_p[n])
                if mask is not None:
                    w = jnp.where(mask, w, 0.0)
                w_p.append(w.astype(BF16))
            ws.append(w_p)
        pvs = [None] * PAIRS_PER_STEP
        for (k0, width, _), w_p in zip(parts, ws):
            zero_v = jnp.zeros((width, LANES), BF16)
            for p in range(PAIRS_PER_STEP):
                vb = v_ref[pl.ds(k0, width), p * LANES:(p + 1) * LANES]
                d = (_dot(w_p[2 * p], jnp.where(is_a, vb, zero_v))
                     + _dot(w_p[2 * p + 1], jnp.where(is_a, zero_v, vb)))
                pvs[p] = d if pvs[p] is None else pvs[p] + d
        return pvs, stay

    def commit(pvs, stay, first):
        low = None
        for p in range(PAIRS_PER_STEP):
            acc_ref[p] = pvs[p] if first else acc_ref[p] + pvs[p]
        for n in range(n_heads):
            stay_ref[n] = jnp.broadcast_to(stay[n], (qb, LANES))
            low = stay[n] if low is None else jnp.minimum(low, stay[n])
        return jnp.min(low)

    def diag_and_window():
        w0 = pl.multiple_of(q0 - kw, qb)
        return commit(*sweep([(q0, qb, strict), (w0, kw, None)], [None] * n_heads), True)

    def diag_only():
        return commit(*sweep([(q0, qb, strict)], [None] * n_heads), True)

    has_window = q0 >= kw
    low1 = lax.cond(has_window, diag_and_window, diag_only)
    start1 = jnp.where(has_window, q0 - kw, q0)

    def cond(c):
        start, low = c
        return jnp.logical_and(start > 0, low < STAY_BITS_DONE)

    def body(c):
        start, _ = c
        k0 = pl.multiple_of(jnp.maximum(start - kw, 0), qb)
        valid = (k0 + w_col) < start
        stay_in = [stay_ref[n][:, 0:1] for n in range(n_heads)]
        return k0, commit(*sweep([(k0, kw, valid)], stay_in), False)

    lax.while_loop(cond, body, (start1, low1))
    for p in range(PAIRS_PER_STEP):
        ga = ga_ref[:, p * LANES:(p + 1) * LANES]
        o_ref[:, p * LANES:(p + 1) * LANES] = acc_ref[p] * (ga * _sigmoid(ga))


def _attention(qkv3, pf3):
    bsz, seq, _ = qkv3.shape
    qb = min(ATTN_QUERY_BLOCK, seq)
    kw = min(ATTN_KEY_WINDOW, seq)
    assert seq % qb == 0 and kw % qb == 0
    blk = qb
    w = PAIRS_PER_STEP * LANES
    n_grp = HEAD_PAIRS // PAIRS_PER_STEP
    ga0 = (2 * SSM_WIDTH) // w
    kern = functools.partial(_attn_kernel, qb=qb, kw=kw)
    return pl.pallas_call(
        kern,
        grid=(bsz, n_grp, seq // blk),
        in_specs=[
            pl.BlockSpec((None, blk, w), lambda b, g, i: (b, i, g)),
            pl.BlockSpec((None, seq, w), lambda b, g, i: (b, 0, n_grp + g)),
            pl.BlockSpec((None, seq, w), lambda b, g, i: (b, 0, 2 * n_grp + g)),
            pl.BlockSpec((None, blk, w), lambda b, g, i: (b, i, ga0 + g)),
            pl.BlockSpec((kw, kw), lambda b, g, i: (0, 0)),
        ],
        out_specs=pl.BlockSpec((None, blk, w), lambda b, g, i: (b, i, g)),
        out_shape=jax.ShapeDtypeStruct((bsz, seq, ATTN_WIDTH), F32),
        scratch_shapes=[pltpu.VMEM((PAIRS_PER_STEP, blk, LANES), F32),
                        pltpu.VMEM((2 * PAIRS_PER_STEP, blk, LANES), F32)],
        compiler_params=pltpu.CompilerParams(
            dimension_semantics=("arbitrary", "arbitrary", "arbitrary"), vmem_limit_bytes=VMEM_LIMIT),
        name="attention",
    )(qkv3, qkv3, qkv3, pf3, jnp.tri(kw, kw, -1, dtype=BF16))


def _out_kernel(h_ref, ys_ref, u_ref, gs_ref, ya_ref, p_ref, d_ref, wg_ref, bg_ref, wo1_ref, wo2_ref,
                pg_ref, wpg_ref, wpp_ref, o_ref):
    rows = h_ref.shape[0] // OUT_SUBTILES
    subs = [slice(i * rows, (i + 1) * rows) for i in range(OUT_SUBTILES)]
    zs = []
    for sl in subs:
        y = ys_ref[sl, :] + d_ref[...] * u_ref[sl, :]
        z = 0.5 * y * (1.0 + jnp.tanh(math.sqrt(2.0 / math.pi) * (y + 0.044715 * (y * y * y))))
        zs.append(z.astype(BF16))
    zzs = [_dot(z, wg_ref[...]) + bg_ref[...] for z in zs]
    y_ssms = []
    for sl, zz in zip(subs, zzs):
        gs = gs_ref[sl, :]
        y_ssm = (zz[:, 0:SSM_WIDTH] * _sigmoid(zz[:, SSM_WIDTH:2 * SSM_WIDTH])) * (gs * _sigmoid(gs))
        y_ssms.append(y_ssm.astype(BF16))
    hs = [h_ref[sl, :] + _dot(y_ssm, wo1_ref[...]) + _dot(ya_ref[sl, :].astype(BF16), wo2_ref[...])
          for sl, y_ssm in zip(subs, y_ssms)]
    hns = []
    for h in hs:
        ms = jnp.mean(h * h, axis=-1, keepdims=True)
        hns.append(((h * lax.rsqrt(ms + RMS_EPS)) * pg_ref[...]).astype(BF16))
    gates = [_sigmoid(_dot(hn, wpg_ref[...])) for hn in hns]
    for sl, h, gate in zip(subs, hs, gates):
        o_ref[sl, :] = h + gate * _dot(p_ref[sl, :].astype(BF16), wpp_ref[...])


def _out(h, y_s, pf, y_att, p, d_skip, w_glu_bf, b_glu, w_out_bf, ple_g, w_pg_bf, w_pp_bf):
    t = h.shape[0]
    tm = min(TOKEN_TILE, t)

    def const(shape):
        return pl.BlockSpec(shape, lambda i: (0, 0))

    return pl.pallas_call(
        _out_kernel,
        grid=(t // tm,),
        in_specs=[
            pl.BlockSpec((tm, D_MODEL), lambda i: (i, 0)),
            pl.BlockSpec((tm, SSM_WIDTH), lambda i: (i, 0)),
            pl.BlockSpec((tm, SSM_WIDTH), lambda i: (i, 0)),
            pl.BlockSpec((tm, SSM_WIDTH), lambda i: (i, 1)),
            pl.BlockSpec((tm, ATTN_WIDTH), lambda i: (i, 0)),
            pl.BlockSpec((tm, PLE_DIM), lambda i: (i, 0)),
            const((1, SSM_WIDTH)),
            const((SSM_WIDTH, 2 * SSM_WIDTH)),
            const((1, 2 * SSM_WIDTH)),
            const((SSM_WIDTH, D_MODEL)),
            const((ATTN_WIDTH, D_MODEL)),
            const((1, D_MODEL)),
            const((D_MODEL, D_MODEL)),
            const((PLE_DIM, D_MODEL)),
        ],
        out_specs=pl.BlockSpec((tm, D_MODEL), lambda i: (i, 0)),
        out_shape=jax.ShapeDtypeStruct((t, D_MODEL), F32),
        compiler_params=pltpu.CompilerParams(
            dimension_semantics=("arbitrary",), vmem_limit_bytes=VMEM_LIMIT),
        name="out",
    )(h, y_s, pf, pf, y_att, p, d_skip.reshape(1, SSM_WIDTH), w_glu_bf, b_glu.reshape(1, 2 * SSM_WIDTH),
      w_out_bf[:SSM_WIDTH], w_out_bf[SSM_WIDTH:], ple_g.reshape(1, D_MODEL), w_pg_bf, w_pp_bf)


def kernel(x, p, mix_norm_g, w_in, ssm_a_re, ssm_a_im, ssm_log_dt, ssm_b_re, ssm_b_im, ssm_c_re, ssm_c_im,
           ssm_d, ssm_w_glu, ssm_b_glu, q_norm_g, k_norm_g, w_out, ple_norm_g, w_ple_gate, w_ple_proj):
    bsz, seq, _ = x.shape
    depth = w_in.shape[0]
    t = bsz * seq
    n_chunks = seq // SSM_CHUNK
    assert seq % SSM_CHUNK == 0 and t % min(TOKEN_TILE, t) == 0
    n_groups = n_chunks // SCAN_RADIX
    assert n_chunks % SCAN_RADIX == 0 and n_groups & (n_groups - 1) == 0
    n_lvl2 = n_groups.bit_length() - 1
    n_pw = -(-(SCAN_RADIX + n_lvl2) // 8) * 8

    w1, w2, pw = _ssm_setup(ssm_a_re, ssm_a_im, ssm_log_dt, ssm_b_re, ssm_b_im, ssm_c_re, ssm_c_im, n_pw, n_lvl2)
    a0 = 2 * SSM_WIDTH
    w_in_bf = jnp.concatenate(
        [w_in[:, :, :a0], w_in[:, :, a0 + 3 * ATTN_WIDTH:], w_in[:, :, a0:a0 + 3 * ATTN_WIDTH]], axis=-1
    ).astype(BF16)
    w_glu_bf = ssm_w_glu.astype(BF16)
    w_out_bf = w_out.astype(BF16)
    w_pg_bf = w_ple_gate.astype(BF16)
    w_pp_bf = w_ple_proj.astype(BF16)

    h = x.reshape(t, D_MODEL)
    for i in range(depth):
        gq2 = jnp.tile(q_norm_g[i].reshape(1, HEAD_DIM), (1, 2))
        gk2 = jnp.tile(k_norm_g[i].reshape(1, HEAD_DIM), (1, 2))
        pf, qkv = _inproj(h, mix_norm_g[i], w_in_bf[i], gq2, gk2)
        pf3 = pf.reshape(bsz, seq, F32_COLS)
        y_s = _ssm_main(pf3, w1[i], w2[i], pw[i], n_lvl2).reshape(t, SSM_WIDTH)
        y_att = _attention(qkv.reshape(bsz, seq, QKV_COLS), pf3).reshape(t, ATTN_WIDTH)
        h = _out(h, y_s, pf, y_att, p[i].reshape(t, PLE_DIM), ssm_d[i].reshape(SSM_WIDTH), w_glu_bf[i],
                 ssm_b_glu[i], w_out_bf[i], ple_norm_g[i], w_pg_bf[i], w_pp_bf[i])
    return h.reshape(bsz, seq, D_MODEL)
```

```python
import functools
import math

import jax
import jax.numpy as jnp
from jax import lax
from jax.experimental import pallas as pl
from jax.experimental.pallas import tpu as pltpu

F32 = jnp.float32
BF16 = jnp.bfloat16

D_MODEL = 1024
PLE_DIM = 256
SSM_WIDTH = 512
SSM_GROUP = 16
SSM_STATE = 64
LANES = 128
MXU_TILE = 256
SLAB_GROUPS = LANES // SSM_GROUP
SSM_SLABS = SSM_WIDTH // LANES
SLAB_STATES = SLAB_GROUPS * SSM_STATE
SSM_CHUNK = 8
CHUNK_COLS = SSM_CHUNK * LANES
SCAN_RADIX = 8
ATTN_WIDTH = 512
HEAD_DIM = 64
HEAD_PAIRS = ATTN_WIDTH // LANES
PAIRS_PER_STEP = 4
IN_COLS = 3072
F32_COLS = 2 * SSM_WIDTH + ATTN_WIDTH
QKV_COLS = 3 * ATTN_WIDTH
RMS_EPS = 1e-6
LOG2E = 1.4426950408889634
ATTN_QUERY_BLOCK = 128
ATTN_KEY_WINDOW = 256
STAY_BITS_DONE = 152.0
TOKEN_TILE = 512
OUT_SUBTILES = 2
VMEM_LIMIT = 56 * 1024 * 1024


def _sigmoid(x):
    return 1.0 / (1.0 + jnp.exp(-x))


def _dot(a, b):
    return jnp.dot(a, b, preferred_element_type=F32)


def _dot_exact(a, b):
    return jnp.dot(a, b, preferred_element_type=F32, precision=lax.Precision.HIGHEST)


def _head_mask():
    return lax.broadcasted_iota(jnp.int32, (1, LANES), 1) < HEAD_DIM


def _headnorm(x, g, is_a):
    x2 = x * x
    s_a = jnp.sum(jnp.where(is_a, x2, 0.0), axis=-1, keepdims=True)
    s_b = jnp.sum(jnp.where(is_a, 0.0, x2), axis=-1, keepdims=True)
    ms = jnp.where(is_a, s_a, s_b) * (1.0 / HEAD_DIM)
    return (x * lax.rsqrt(ms + RMS_EPS)) * g


def _inproj_kernel(x_ref, g_ref, w_ref, gq_ref, gk_ref, f_ref, qkv_ref, w_scr):
    @pl.when(pl.program_id(0) == 0)
    def _():
        a0 = 2 * SSM_WIDTH
        w_scr[:, 0:a0] = w_ref[:, 0:a0].astype(BF16)
        w_scr[:, a0:F32_COLS] = w_ref[:, a0 + QKV_COLS:IN_COLS].astype(BF16)
        w_scr[:, F32_COLS:IN_COLS] = w_ref[:, a0:a0 + QKV_COLS].astype(BF16)

    x = x_ref[...]
    ms = jnp.mean(x * x, axis=-1, keepdims=True)
    hn = (x * lax.rsqrt(ms + RMS_EPS)) * g_ref[...]
    res = _dot(hn.astype(BF16), w_scr[...])
    f_ref[...] = res[:, 0:F32_COLS]
    is_a = _head_mask()
    q_scale = (HEAD_DIM ** -0.5) * LOG2E
    for s in range(HEAD_PAIRS):
        c0 = F32_COLS + s * LANES
        q = _headnorm(res[:, c0:c0 + LANES], gq_ref[...], is_a) * q_scale
        qkv_ref[:, s * LANES:(s + 1) * LANES] = q.astype(BF16)
        c1 = c0 + ATTN_WIDTH
        k = _headnorm(res[:, c1:c1 + LANES], gk_ref[...], is_a)
        qkv_ref[:, ATTN_WIDTH + s * LANES:ATTN_WIDTH + (s + 1) * LANES] = k.astype(BF16)
    qkv_ref[:, 2 * ATTN_WIDTH:3 * ATTN_WIDTH] = res[:, F32_COLS + 2 * ATTN_WIDTH:IN_COLS].astype(BF16)


def _layer_spec(shape, layer):
    return pl.BlockSpec((None,) + shape, lambda i: (layer,) + (0,) * len(shape), pipeline_mode=pl.Buffered(1))


def _inproj(h, g_all, w_all, gq_all, gk_all, layer):
    t = h.shape[0]
    tm = min(TOKEN_TILE, t)
    return pl.pallas_call(
        _inproj_kernel,
        grid=(t // tm,),
        in_specs=[
            pl.BlockSpec((tm, D_MODEL), lambda i: (i, 0)),
            _layer_spec((1, D_MODEL), layer),
            _layer_spec((D_MODEL, IN_COLS), layer),
            _layer_spec((1, LANES), layer),
            _layer_spec((1, LANES), layer),
        ],
        out_specs=[pl.BlockSpec((tm, F32_COLS), lambda i: (i, 0)),
                   pl.BlockSpec((tm, QKV_COLS), lambda i: (i, 0))],
        out_shape=[jax.ShapeDtypeStruct((t, F32_COLS), F32), jax.ShapeDtypeStruct((t, QKV_COLS), BF16)],
        scratch_shapes=[pltpu.VMEM((D_MODEL, IN_COLS), BF16)],
        compiler_params=pltpu.CompilerParams(
            dimension_semantics=("arbitrary",), vmem_limit_bytes=VMEM_LIMIT),
        name="inproj",
    )(h, g_all, w_all, gq_all, gk_all)


def _ssm_setup_kernel(are_r, aim_r, ldt_r, are_c, aim_c, ldt_c, bre_ref, bim_ref, cre_ref, cim_ref,
                      w1_ref, w2_ref, pw_ref, *, n_pw, n_lvl2):
    ns = SLAB_STATES
    a_re = are_r[...]
    a_im = aim_r[...]
    dt = jnp.exp(ldt_r[...])
    lam_r = a_re * dt
    lam_i = a_im * dt
    mag = jnp.exp(lam_r)
    ab_re = mag * jnp.cos(lam_i)
    ab_im = mag * jnp.sin(lam_i)
    num_re = ab_re - 1.0
    num_im = ab_im
    den = a_re * a_re + a_im * a_im
    f_re = (num_re * a_re + num_im * a_im) / den
    f_im = (num_im * a_re - num_re * a_im) / den
    b_re = bre_ref[...]
    b_im = bim_ref[...]
    bb_re = jnp.tile(f_re * b_re - f_im * b_im, (SLAB_GROUPS, 1))
    bb_im = jnp.tile(f_re * b_im + f_im * b_re, (SLAB_GROUPS, 1))
    grp_row = lax.broadcasted_iota(jnp.int32, (LANES, ns), 0) >> 4
    grp_col = lax.broadcasted_iota(jnp.int32, (LANES, ns), 1) >> 6
    same_in = grp_row == grp_col
    zin = jnp.zeros((LANES, ns), F32)
    bb_re = jnp.where(same_in, bb_re, zin)
    bb_im = jnp.where(same_in, bb_im, zin)

    e_in = (SSM_CHUNK - 1 - lax.broadcasted_iota(jnp.int32, (SSM_CHUNK, 1), 0)).astype(F32)
    mag_in = jnp.exp(lam_r * e_in)
    pin_re = mag_in * jnp.cos(lam_i * e_in)
    pin_im = mag_in * jnp.sin(lam_i * e_in)
    for lp in range(SSM_CHUNK):
        p_re = pin_re[lp:lp + 1, :]
        p_im = pin_im[lp:lp + 1, :]
        w1_ref[lp * LANES:(lp + 1) * LANES, 0:ns] = (p_re * bb_re - p_im * bb_im).astype(BF16)
        w1_ref[lp * LANES:(lp + 1) * LANES, ns:2 * ns] = (p_re * bb_im + p_im * bb_re).astype(BF16)

    dt_c = jnp.exp(ldt_c[...])
    lam_rc = are_c[...] * dt_c
    lam_ic = aim_c[...] * dt_c
    lag = (lax.broadcasted_iota(jnp.int32, (1, LANES), 1) & (2 * SSM_CHUNK - 1)).astype(F32)
    grp_row_o = lax.broadcasted_iota(jnp.int32, (ns, LANES), 0) >> 6
    grp_col_o = lax.broadcasted_iota(jnp.int32, (ns, LANES), 1) >> 4
    same_out = grp_row_o == grp_col_o
    zout = jnp.zeros((ns, LANES), F32)
    c_re = jnp.where(same_out, cre_ref[...], zout)
    c_im = jnp.where(same_out, cim_ref[...], zout)

    mag_c = jnp.exp(lam_rc * lag)
    pc_re = mag_c * jnp.cos(lam_ic * lag)
    pc_im = mag_c * jnp.sin(lam_ic * lag)
    c_pow = []
    for l in range(SSM_CHUNK + 1):
        q_re = jnp.broadcast_to(pc_re[:, l:l + 1], (ns, LANES))
        q_im = jnp.broadcast_to(pc_im[:, l:l + 1], (ns, LANES))
        c_pow.append((c_re * q_re - c_im * q_im, -c_re * q_im - c_im * q_re))
    kf = []
    for l in range(SSM_CHUNK):
        w2_ref[0:ns, l * LANES:(l + 1) * LANES] = c_pow[l + 1][0].astype(BF16)
        w2_ref[ns:2 * ns, l * LANES:(l + 1) * LANES] = c_pow[l + 1][1].astype(BF16)
        kf.append((_dot_exact(bb_re, c_pow[l][0]) + _dot_exact(bb_im, c_pow[l][1])).astype(BF16))
    zblk = jnp.zeros((LANES, LANES), BF16)
    for lp in range(SSM_CHUNK):
        for l in range(SSM_CHUNK):
            w1_ref[lp * LANES:(lp + 1) * LANES, 2 * ns + l * LANES:2 * ns + (l + 1) * LANES] = (
                kf[l - lp] if l >= lp else zblk)

    prow = lax.broadcasted_iota(jnp.int32, (n_pw, 1), 0)
    dbl = jnp.clip(prow - SCAN_RADIX, 0, max(n_lvl2 - 1, 0))
    e_pw = jnp.where(prow < SCAN_RADIX, SSM_CHUNK * prow,
                     jnp.left_shift(SSM_CHUNK * SCAN_RADIX, dbl)).astype(F32)
    mag_pw = jnp.exp(lam_r * e_pw)
    pw_ref[:, 0:ns] = mag_pw * jnp.cos(lam_i * e_pw)
    pw_ref[:, ns:2 * ns] = mag_pw * jnp.sin(lam_i * e_pw)


def _ssm_setup(a_re, a_im, log_dt, b_re, b_im, c_re, c_im, n_pw, n_lvl2):
    depth = a_re.shape[0]
    g, p, h = a_re.shape[1], SSM_STATE, SSM_GROUP
    ns = SLAB_STATES
    ldt = jnp.broadcast_to(log_dt[:, :, None], (depth, g, p))

    def row(v):
        return v.reshape(depth, SSM_SLABS, 1, ns)

    def col(v):
        return v.reshape(depth, SSM_SLABS, ns, 1)

    def b_lay(v):
        return v.reshape(depth, SSM_SLABS, SLAB_GROUPS, p, h).transpose(0, 1, 4, 2, 3).reshape(
            depth, SSM_SLABS, h, ns)

    def c_lay(v):
        vt = v.reshape(depth, SSM_SLABS, SLAB_GROUPS, h, p).transpose(0, 1, 2, 4, 3).reshape(
            depth, SSM_SLABS, ns, h)
        return jnp.tile(vt, (1, 1, 1, SLAB_GROUPS))

    def spec(shape):
        return pl.BlockSpec((None, None) + shape, lambda d, q: (d, q) + (0,) * len(shape))

    return pl.pallas_call(
        functools.partial(_ssm_setup_kernel, n_pw=n_pw, n_lvl2=n_lvl2),
        grid=(depth, SSM_SLABS),
        in_specs=[spec((1, ns))] * 3 + [spec((ns, 1))] * 3 + [spec((h, ns))] * 2 + [spec((ns, LANES))] * 2,
        out_specs=[spec((CHUNK_COLS, 2 * ns + CHUNK_COLS)), spec((2 * ns, CHUNK_COLS)), spec((n_pw, 2 * ns))],
        out_shape=[
            jax.ShapeDtypeStruct((depth, SSM_SLABS, CHUNK_COLS, 2 * ns + CHUNK_COLS), BF16),
            jax.ShapeDtypeStruct((depth, SSM_SLABS, 2 * ns, CHUNK_COLS), BF16),
            jax.ShapeDtypeStruct((depth, SSM_SLABS, n_pw, 2 * ns), F32),
        ],
        compiler_params=pltpu.CompilerParams(
            dimension_semantics=("arbitrary", "arbitrary"), vmem_limit_bytes=VMEM_LIMIT),
        name="ssm_setup",
    )(row(a_re), row(a_im), row(ldt), col(a_re), col(a_im), col(ldt),
      b_lay(b_re), b_lay(b_im), c_lay(c_re), c_lay(c_im))


def _ssm_main_kernel(u_ref, w1_ref, w2_ref, pw_ref, y_ref, lhs_ref, z_scr, sp_scr, g_a, g_b,
                     *, n_chunks, n_lvl2, pad):
    ns = SLAB_STATES
    nsl = ns // LANES
    ng = n_chunks // SCAN_RADIX
    for l in range(SSM_CHUNK):
        lhs_ref[:, l * LANES:(l + 1) * LANES] = u_ref[pl.ds(l, n_chunks, stride=SSM_CHUNK), :].astype(BF16)
    zin = _dot(lhs_ref[...], w1_ref[:, 0:2 * ns])
    for j in range(2 * nsl):
        z_scr[j] = zin[:, j * LANES:(j + 1) * LANES]
    y_intra = [_dot(lhs_ref[:, 0:(c + 1) * MXU_TILE],
                    w1_ref[0:(c + 1) * MXU_TILE, 2 * ns + c * MXU_TILE:2 * ns + (c + 1) * MXU_TILE])
               for c in range(CHUNK_COLS // MXU_TILE)]

    for js in range(nsl):
        jr, ji = js, nsl + js
        lr = slice(js * LANES, (js + 1) * LANES)
        li = slice(ns + js * LANES, ns + (js + 1) * LANES)
        a_r = pw_ref[1:2, lr]
        a_i = pw_ref[1:2, li]
        xs = []
        for lo in range(SCAN_RADIX):
            z_r = z_scr[jr, pl.ds(lo, ng, stride=SCAN_RADIX), :]
            z_i = z_scr[ji, pl.ds(lo, ng, stride=SCAN_RADIX), :]
            if lo == 0:
                x_r, x_i = z_r, z_i
            else:
                x_r, x_i = a_r * x_r - a_i * x_i + z_r, a_r * x_i + a_i * x_r + z_i
            xs.append((x_r, x_i))
        ga, gb = g_a.at[js], g_b.at[js]
        zeros = jnp.zeros((pad, 2 * LANES), F32)
        ga[0:pad, :] = zeros
        gb[0:pad, :] = zeros
        ga[pad:pad + ng, 0:LANES] = x_r
        ga[pad:pad + ng, LANES:2 * LANES] = x_i
        src, dst = ga, gb
        for k in range(n_lvl2):
            d = 1 << k
            t_r = src[pad:pad + ng, 0:LANES]
            t_i = src[pad:pad + ng, LANES:2 * LANES]
            s_r = src[pad - d:pad - d + ng, 0:LANES]
            s_i = src[pad - d:pad - d + ng, LANES:2 * LANES]
            p_r = pw_ref[SCAN_RADIX + k:SCAN_RADIX + k + 1, lr]
            p_i = pw_ref[SCAN_RADIX + k:SCAN_RADIX + k + 1, li]
            dst[pad:pad + ng, 0:LANES] = t_r + p_r * s_r - p_i * s_i
            dst[pad:pad + ng, LANES:2 * LANES] = t_i + p_r * s_i + p_i * s_r
            src, dst = dst, src
        sg_r = src[pad - 1:pad - 1 + ng, 0:LANES]
        sg_i = src[pad - 1:pad - 1 + ng, LANES:2 * LANES]
        for lo in range(SCAN_RADIX):
            if lo == 0:
                s_r, s_i = sg_r, sg_i
            else:
                p_r = pw_ref[lo:lo + 1, lr]
                p_i = pw_ref[lo:lo + 1, li]
                s_r = p_r * sg_r - p_i * sg_i + xs[lo - 1][0]
                s_i = p_r * sg_i + p_i * sg_r + xs[lo - 1][1]
            sp_scr[jr, pl.ds(lo, ng, stride=SCAN_RADIX), :] = s_r
            sp_scr[ji, pl.ds(lo, ng, stride=SCAN_RADIX), :] = s_i

    s_prev = jnp.concatenate([sp_scr[j] for j in range(2 * nsl)], axis=1).astype(BF16)
    y = jnp.concatenate(y_intra, axis=1) + _dot(s_prev, w2_ref[...])
    for l in range(SSM_CHUNK):
        y_ref[pl.ds(l, n_chunks, stride=SSM_CHUNK), :] = y[:, l * LANES:(l + 1) * LANES]


def _ssm_main(pf3, w1, w2, pw, n_lvl2, layer):
    bsz, seq, _ = pf3.shape
    n_chunks = seq // SSM_CHUNK
    ns = SLAB_STATES
    nsl = ns // LANES
    ng = n_chunks // SCAN_RADIX
    pad = max(8, ng // 2)
    n_pw = pw.shape[2]
    kern = functools.partial(_ssm_main_kernel, n_chunks=n_chunks, n_lvl2=n_lvl2, pad=pad)
    return pl.pallas_call(
        kern,
        grid=(SSM_SLABS, bsz),
        in_specs=[
            pl.BlockSpec((None, seq, LANES), lambda s, b: (b, 0, s)),
            pl.BlockSpec((None, None, CHUNK_COLS, 2 * ns + CHUNK_COLS), lambda s, b: (layer, s, 0, 0)),
            pl.BlockSpec((None, None, 2 * ns, CHUNK_COLS), lambda s, b: (layer, s, 0, 0)),
            pl.BlockSpec((None, None, n_pw, 2 * ns), lambda s, b: (layer, s, 0, 0)),
        ],
        out_specs=pl.BlockSpec((None, seq, LANES), lambda s, b: (b, 0, s)),
        out_shape=jax.ShapeDtypeStruct((bsz, seq, SSM_WIDTH), F32),
        scratch_shapes=[pltpu.VMEM((n_chunks, CHUNK_COLS), BF16),
                        pltpu.VMEM((2 * nsl, n_chunks, LANES), F32),
                        pltpu.VMEM((2 * nsl, n_chunks, LANES), F32),
                        pltpu.VMEM((nsl, pad + ng, 2 * LANES), F32),
                        pltpu.VMEM((nsl, pad + ng, 2 * LANES), F32)],
        compiler_params=pltpu.CompilerParams(
            dimension_semantics=("arbitrary", "arbitrary"), vmem_limit_bytes=VMEM_LIMIT),
        name="ssm_main",
    )(pf3, w1, w2, pw)


def _attn_kernel(q_ref, k_ref, v_ref, ga_ref, tri_ref, o_ref, acc_ref, stay_ref, *, qb, kw, nq):
    i = pl.program_id(2)
    q0 = pl.multiple_of(i * (nq * qb), nq * qb)
    is_a = _head_mask()
    n_heads = 2 * PAIRS_PER_STEP
    d_row = lax.broadcasted_iota(jnp.int32, (qb, qb), 0)
    d_col = lax.broadcasted_iota(jnp.int32, (qb, qb), 1)
    strict = d_col < d_row
    w_col = lax.broadcasted_iota(jnp.int32, (qb, kw), 1)
    tris = {kw: tri_ref[...], qb: tri_ref[0:qb, 0:qb]}
    zero_q = jnp.zeros((qb, LANES), BF16)

    q_heads = []
    for s in range(nq):
        heads = []
        for p in range(PAIRS_PER_STEP):
            qp = q_ref[s * qb:(s + 1) * qb, p * LANES:(p + 1) * LANES]
            heads += [jnp.where(is_a, qp, zero_q), jnp.where(is_a, zero_q, qp)]
        q_heads.append(heads)

    def sweep(parts, stay):
        stay = [list(row) for row in stay]
        units = [(pi, n) for pi in range(len(parts)) for n in range(n_heads)]
        kbs = [[k_ref[pl.ds(k0, width), p * LANES:(p + 1) * LANES] for p in range(PAIRS_PER_STEP)]
               for _, k0, width, _ in parts]
        zs, sps, lefts, laters, ws = {}, {}, {}, {}, {}

        def scores(u):
            pi, n = u
            zs[u] = lax.dot_general(q_heads[parts[pi][0]][n], kbs[pi][n // 2], (((1,), (1,)), ((), ())),
                                    preferred_element_type=F32)

        def stick(u):
            pi, n = u
            s, _, _, mask = parts[pi]
            z = zs.pop(u)
            sp = jnp.maximum(z, 0.0) + jnp.log(1.0 + jnp.exp2(-jnp.abs(z))) * LOG2E
            if mask is not None:
                sp = jnp.where(mask, sp, 0.0)
            tot = jnp.sum(sp, axis=-1, keepdims=True)
            left = z - sp
            if stay[s][n] is None:
                stay[s][n] = tot
            else:
                left = left - stay[s][n]
                stay[s][n] = stay[s][n] + tot
            sps[u] = sp.astype(BF16)
            lefts[u] = left

        def later(u):
            laters[u] = _dot(sps.pop(u), tris[parts[u[0]][2]])

        def weight(u):
            mask = parts[u[0]][3]
            w = jnp.exp2(lefts.pop(u) - laters.pop(u))
            if mask is not None:
                w = jnp.where(mask, w, 0.0)
            ws[u] = w.astype(BF16)

        for stage in (scores, stick, later, weight):
            for u in units:
                stage(u)
        pvs = [[None] * PAIRS_PER_STEP for _ in range(nq)]
        for pi, (s, k0, width, _) in enumerate(parts):
            zero_v = jnp.zeros((width, LANES), BF16)
            for p in range(PAIRS_PER_STEP):
                vb = v_ref[pl.ds(k0, width), p * LANES:(p + 1) * LANES]
                d = (_dot(ws.pop((pi, 2 * p)), jnp.where(is_a, vb, zero_v))
                     + _dot(ws.pop((pi, 2 * p + 1)), jnp.where(is_a, zero_v, vb)))
                pvs[s][p] = d if pvs[s][p] is None else pvs[s][p] + d
        return pvs, stay

    def commit(pvs, stay, first):
        low = None
        for s in range(nq):
            for p in range(PAIRS_PER_STEP):
                acc_ref[s, p] = pvs[s][p] if first else acc_ref[s, p] + pvs[s][p]
            for n in range(n_heads):
                stay_ref[s, n] = jnp.broadcast_to(stay[s][n], (qb, LANES))
                low = stay[s][n] if low is None else jnp.minimum(low, stay[s][n])
        return jnp.min(low)

    fresh = [[None] * n_heads for _ in range(nq)]
    diags = [(s, pl.multiple_of(q0 + s * qb, qb), qb, strict) for s in range(nq)]

    def diag_and_window():
        wins = [(s, pl.multiple_of(q0 + s * qb - kw, qb), kw, None) for s in range(nq)]
        return commit(*sweep(diags + wins, fresh), True)

    def first_step():
        prevs = [(s, j * qb, qb, None) for s in range(nq) for j in reversed(range(s))]
        return commit(*sweep(diags + prevs, fresh), True)

    has_window = q0 >= kw
    low1 = lax.cond(has_window, diag_and_window, first_step)
    start1 = jnp.where(has_window, q0 + (nq - 1) * qb - kw, 0)

    def cond(c):
        start, low = c
        return jnp.logical_and(start > 0, low < STAY_BITS_DONE)

    def body(c):
        start, _ = c
        parts = []
        for s in range(nq):
            start_s = jnp.maximum(start - (nq - 1 - s) * qb, 0)
            k0 = pl.multiple_of(jnp.maximum(start_s - kw, 0), qb)
            parts.append((s, k0, kw, (k0 + w_col) < start_s))
        stay_in = [[stay_ref[s, n][:, 0:1] for n in range(n_heads)] for s in range(nq)]
        return jnp.maximum(start - kw, 0), commit(*sweep(parts, stay_in), False)

    lax.while_loop(cond, body, (start1, low1))
    for s in range(nq):
        for p in range(PAIRS_PER_STEP):
            ga = ga_ref[s * qb:(s + 1) * qb, p * LANES:(p + 1) * LANES]
            o_ref[s * qb:(s + 1) * qb, p * LANES:(p + 1) * LANES] = acc_ref[s, p] * (ga * _sigmoid(ga))


def _attention(qkv3, pf3):
    bsz, seq, _ = qkv3.shape
    qb = min(ATTN_QUERY_BLOCK, seq)
    kw = min(ATTN_KEY_WINDOW, seq)
    nq = kw // qb
    blk = nq * qb
    assert seq % blk == 0 and blk == kw
    w = PAIRS_PER_STEP * LANES
    n_grp = HEAD_PAIRS // PAIRS_PER_STEP
    ga0 = (2 * SSM_WIDTH) // w
    kern = functools.partial(_attn_kernel, qb=qb, kw=kw, nq=nq)
    return pl.pallas_call(
        kern,
        grid=(bsz, n_grp, seq // blk),
        in_specs=[
            pl.BlockSpec((None, blk, w), lambda b, g, i: (b, i, g)),
            pl.BlockSpec((None, seq, w), lambda b, g, i: (b, 0, n_grp + g)),
            pl.BlockSpec((None, seq, w), lambda b, g, i: (b, 0, 2 * n_grp + g)),
            pl.BlockSpec((None, blk, w), lambda b, g, i: (b, i, ga0 + g)),
            pl.BlockSpec((kw, kw), lambda b, g, i: (0, 0)),
        ],
        out_specs=pl.BlockSpec((None, blk, w), lambda b, g, i: (b, i, g)),
        out_shape=jax.ShapeDtypeStruct((bsz, seq, ATTN_WIDTH), F32),
        scratch_shapes=[pltpu.VMEM((nq, PAIRS_PER_STEP, qb, LANES), F32),
                        pltpu.VMEM((nq, 2 * PAIRS_PER_STEP, qb, LANES), F32)],
        compiler_params=pltpu.CompilerParams(
            dimension_semantics=("arbitrary", "arbitrary", "arbitrary"), vmem_limit_bytes=VMEM_LIMIT),
        name="attention",
    )(qkv3, qkv3, qkv3, pf3, jnp.tri(kw, kw, -1, dtype=BF16))


def _out_kernel(h_ref, ys_ref, u_ref, gs_ref, ya_ref, p_ref, d_ref, wgf_ref, bg_ref, wof_ref,
                pg_ref, wpgf_ref, wppf_ref, o_ref, wg_ref, wo1_ref, wo2_ref, wpg_ref, wpp_ref):
    @pl.when(pl.program_id(0) == 0)
    def _():
        wg_ref[...] = wgf_ref[...].astype(BF16)
        wo1_ref[...] = wof_ref[0:SSM_WIDTH, :].astype(BF16)
        wo2_ref[...] = wof_ref[SSM_WIDTH:SSM_WIDTH + ATTN_WIDTH, :].astype(BF16)
        wpg_ref[...] = wpgf_ref[...].astype(BF16)
        wpp_ref[...] = wppf_ref[...].astype(BF16)

    rows = h_ref.shape[0] // OUT_SUBTILES
    subs = [slice(i * rows, (i + 1) * rows) for i in range(OUT_SUBTILES)]
    zs = []
    for sl in subs:
        y = ys_ref[sl, :] + d_ref[...] * u_ref[sl, :]
        z = 0.5 * y * (1.0 + jnp.tanh(math.sqrt(2.0 / math.pi) * (y + 0.044715 * (y * y * y))))
        zs.append(z.astype(BF16))
    zzs = [_dot(z, wg_ref[...]) + bg_ref[...] for z in zs]
    y_ssms = []
    for sl, zz in zip(subs, zzs):
        gs = gs_ref[sl, :]
        y_ssm = (zz[:, 0:SSM_WIDTH] * _sigmoid(zz[:, SSM_WIDTH:2 * SSM_WIDTH])) * (gs * _sigmoid(gs))
        y_ssms.append(y_ssm.astype(BF16))
    hs = [h_ref[sl, :] + _dot(y_ssm, wo1_ref[...]) + _dot(ya_ref[sl, :].astype(BF16), wo2_ref[...])
          for sl, y_ssm in zip(subs, y_ssms)]
    hns = []
    for h in hs:
        ms = jnp.mean(h * h, axis=-1, keepdims=True)
        hns.append(((h * lax.rsqrt(ms + RMS_EPS)) * pg_ref[...]).astype(BF16))
    gates = [_sigmoid(_dot(hn, wpg_ref[...])) for hn in hns]
    for sl, h, gate in zip(subs, hs, gates):
        o_ref[sl, :] = h + gate * _dot(p_ref[sl, :].astype(BF16), wpp_ref[...])


def _out(h, y_s, pf, y_att, p_all, d_all, w_glu_all, b_glu_all, w_out_all, ple_g_all, w_pg_all, w_pp_all, layer):
    t = h.shape[0]
    tm = min(TOKEN_TILE, t)
    return pl.pallas_call(
        _out_kernel,
        grid=(t // tm,),
        in_specs=[
            pl.BlockSpec((tm, D_MODEL), lambda i: (i, 0)),
            pl.BlockSpec((tm, SSM_WIDTH), lambda i: (i, 0)),
            pl.BlockSpec((tm, SSM_WIDTH), lambda i: (i, 0)),
            pl.BlockSpec((tm, SSM_WIDTH), lambda i: (i, 1)),
            pl.BlockSpec((tm, ATTN_WIDTH), lambda i: (i, 0)),
            pl.BlockSpec((None, tm, PLE_DIM), lambda i: (layer, i, 0)),
            _layer_spec((1, SSM_WIDTH), layer),
            _layer_spec((SSM_WIDTH, 2 * SSM_WIDTH), layer),
            _layer_spec((1, 2 * SSM_WIDTH), layer),
            _layer_spec((SSM_WIDTH + ATTN_WIDTH, D_MODEL), layer),
            _layer_spec((1, D_MODEL), layer),
            _layer_spec((D_MODEL, D_MODEL), layer),
            _layer_spec((PLE_DIM, D_MODEL), layer),
        ],
        out_specs=pl.BlockSpec((tm, D_MODEL), lambda i: (i, 0)),
        out_shape=jax.ShapeDtypeStruct((t, D_MODEL), F32),
        scratch_shapes=[pltpu.VMEM((SSM_WIDTH, 2 * SSM_WIDTH), BF16), pltpu.VMEM((SSM_WIDTH, D_MODEL), BF16),
                        pltpu.VMEM((ATTN_WIDTH, D_MODEL), BF16), pltpu.VMEM((D_MODEL, D_MODEL), BF16),
                        pltpu.VMEM((PLE_DIM, D_MODEL), BF16)],
        compiler_params=pltpu.CompilerParams(
            dimension_semantics=("arbitrary",), vmem_limit_bytes=VMEM_LIMIT),
        name="out",
    )(h, y_s, pf, pf, y_att, p_all, d_all, w_glu_all, b_glu_all, w_out_all, ple_g_all, w_pg_all, w_pp_all)


def kernel(x, p, mix_norm_g, w_in, ssm_a_re, ssm_a_im, ssm_log_dt, ssm_b_re, ssm_b_im, ssm_c_re, ssm_c_im,
           ssm_d, ssm_w_glu, ssm_b_glu, q_norm_g, k_norm_g, w_out, ple_norm_g, w_ple_gate, w_ple_proj):
    bsz, seq, _ = x.shape
    depth = w_in.shape[0]
    t = bsz * seq
    n_chunks = seq // SSM_CHUNK
    assert seq % SSM_CHUNK == 0 and t % min(TOKEN_TILE, t) == 0
    n_groups = n_chunks // SCAN_RADIX
    assert n_chunks % SCAN_RADIX == 0 and n_groups & (n_groups - 1) == 0
    n_lvl2 = n_groups.bit_length() - 1
    n_pw = -(-(SCAN_RADIX + n_lvl2) // 8) * 8

    w1, w2, pw = _ssm_setup(ssm_a_re, ssm_a_im, ssm_log_dt, ssm_b_re, ssm_b_im, ssm_c_re, ssm_c_im, n_pw, n_lvl2)
    gq2 = jnp.tile(q_norm_g.reshape(depth, 1, HEAD_DIM), (1, 1, 2))
    gk2 = jnp.tile(k_norm_g.reshape(depth, 1, HEAD_DIM), (1, 1, 2))
    mix_g = mix_norm_g.reshape(depth, 1, D_MODEL)
    ple_g = ple_norm_g.reshape(depth, 1, D_MODEL)
    d_skip = ssm_d.reshape(depth, 1, SSM_WIDTH)
    b_glu = ssm_b_glu.reshape(depth, 1, 2 * SSM_WIDTH)
    p3 = p.reshape(depth, t, PLE_DIM)

    h = x.reshape(t, D_MODEL)
    for i in range(depth):
        pf, qkv = _inproj(h, mix_g, w_in, gq2, gk2, i)
        pf3 = pf.reshape(bsz, seq, F32_COLS)
        y_s = _ssm_main(pf3, w1, w2, pw, n_lvl2, i).reshape(t, SSM_WIDTH)
        y_att = _attention(qkv.reshape(bsz, seq, QKV_COLS), pf3).reshape(t, ATTN_WIDTH)
        h = _out(h, y_s, pf, y_att, p3, d_skip, ssm_w_glu, b_glu, w_out, ple_g, w_ple_gate, w_ple_proj, i)
    return h.reshape(bsz, seq, D_MODEL)
```

```python
import functools
import math

import jax
import jax.numpy as jnp
from jax import lax
from jax.experimental import pallas as pl
from jax.experimental.pallas import tpu as pltpu

F32 = jnp.float32
BF16 = jnp.bfloat16

D_MODEL = 1024
PLE_DIM = 256
SSM_WIDTH = 512
SSM_GROUP = 16
SSM_STATE = 64
LANES = 128
MXU_TILE = 256
SLAB_GROUPS = LANES // SSM_GROUP
SSM_SLABS = SSM_WIDTH // LANES
SLAB_STATES = SLAB_GROUPS * SSM_STATE
SSM_CHUNK = 8
CHUNK_COLS = SSM_CHUNK * LANES
SCAN_RADIX = 8
ATTN_WIDTH = 512
HEAD_DIM = 64
HEAD_PAIRS = ATTN_WIDTH // LANES
PAIRS_PER_STEP = 4
IN_COLS = 3072
F32_COLS = 2 * SSM_WIDTH + ATTN_WIDTH
QKV_COLS = 3 * ATTN_WIDTH
RMS_EPS = 1e-6
LOG2E = 1.4426950408889634
ATTN_QUERY_BLOCK = 128
ATTN_TOP_ROWS = 48
ATTN_KEY_WINDOW = 256
STAY_BITS_DONE = 152.0
TOKEN_TILE = 512
OUT_SUBTILES = 2
VMEM_LIMIT = 56 * 1024 * 1024


def _sigmoid(x):
    return 1.0 / (1.0 + jnp.exp(-x))


def _dot(a, b):
    return jnp.dot(a, b, preferred_element_type=F32)


def _dot_exact(a, b):
    return jnp.dot(a, b, preferred_element_type=F32, precision=lax.Precision.HIGHEST)


def _head_mask():
    return lax.broadcasted_iota(jnp.int32, (1, LANES), 1) < HEAD_DIM


def _headnorm(x, g, is_a):
    x2 = x * x
    s_a = jnp.sum(jnp.where(is_a, x2, 0.0), axis=-1, keepdims=True)
    s_b = jnp.sum(jnp.where(is_a, 0.0, x2), axis=-1, keepdims=True)
    ms = jnp.where(is_a, s_a, s_b) * (1.0 / HEAD_DIM)
    return (x * lax.rsqrt(ms + RMS_EPS)) * g


def _inproj_kernel(x_ref, g_ref, w_ref, gq_ref, gk_ref, f_ref, qkv_ref, w_scr):
    @pl.when(pl.program_id(0) == 0)
    def _():
        a0 = 2 * SSM_WIDTH
        w_scr[:, 0:a0] = w_ref[:, 0:a0].astype(BF16)
        w_scr[:, a0:F32_COLS] = w_ref[:, a0 + QKV_COLS:IN_COLS].astype(BF16)
        w_scr[:, F32_COLS:IN_COLS] = w_ref[:, a0:a0 + QKV_COLS].astype(BF16)

    x = x_ref[...]
    ms = jnp.mean(x * x, axis=-1, keepdims=True)
    hn = (x * lax.rsqrt(ms + RMS_EPS)) * g_ref[...]
    res = _dot(hn.astype(BF16), w_scr[...])
    f_ref[...] = res[:, 0:F32_COLS]
    is_a = _head_mask()
    q_scale = (HEAD_DIM ** -0.5) * LOG2E
    for s in range(HEAD_PAIRS):
        c0 = F32_COLS + s * LANES
        q = _headnorm(res[:, c0:c0 + LANES], gq_ref[...], is_a) * q_scale
        qkv_ref[:, s * LANES:(s + 1) * LANES] = q.astype(BF16)
        c1 = c0 + ATTN_WIDTH
        k = _headnorm(res[:, c1:c1 + LANES], gk_ref[...], is_a)
        qkv_ref[:, ATTN_WIDTH + s * LANES:ATTN_WIDTH + (s + 1) * LANES] = k.astype(BF16)
    qkv_ref[:, 2 * ATTN_WIDTH:3 * ATTN_WIDTH] = res[:, F32_COLS + 2 * ATTN_WIDTH:IN_COLS].astype(BF16)


def _layer_spec(shape, layer):
    return pl.BlockSpec((None,) + shape, lambda i: (layer,) + (0,) * len(shape), pipeline_mode=pl.Buffered(1))


def _inproj(h, g_all, w_all, gq_all, gk_all, layer):
    t = h.shape[0]
    tm = min(TOKEN_TILE, t)
    return pl.pallas_call(
        _inproj_kernel,
        grid=(t // tm,),
        in_specs=[
            pl.BlockSpec((tm, D_MODEL), lambda i: (i, 0)),
            _layer_spec((1, D_MODEL), layer),
            _layer_spec((D_MODEL, IN_COLS), layer),
            _layer_spec((1, LANES), layer),
            _layer_spec((1, LANES), layer),
        ],
        out_specs=[pl.BlockSpec((tm, F32_COLS), lambda i: (i, 0)),
                   pl.BlockSpec((tm, QKV_COLS), lambda i: (i, 0))],
        out_shape=[jax.ShapeDtypeStruct((t, F32_COLS), F32), jax.ShapeDtypeStruct((t, QKV_COLS), BF16)],
        scratch_shapes=[pltpu.VMEM((D_MODEL, IN_COLS), BF16)],
        compiler_params=pltpu.CompilerParams(
            dimension_semantics=("arbitrary",), vmem_limit_bytes=VMEM_LIMIT),
        name="inproj",
    )(h, g_all, w_all, gq_all, gk_all)


def _ssm_setup_kernel(are_r, aim_r, ldt_r, are_c, aim_c, ldt_c, bre_ref, bim_ref, cre_ref, cim_ref,
                      w1_ref, w2_ref, pw_ref, *, n_pw, n_lvl2):
    ns = SLAB_STATES
    a_re = are_r[...]
    a_im = aim_r[...]
    dt = jnp.exp(ldt_r[...])
    lam_r = a_re * dt
    lam_i = a_im * dt
    mag = jnp.exp(lam_r)
    ab_re = mag * jnp.cos(lam_i)
    ab_im = mag * jnp.sin(lam_i)
    num_re = ab_re - 1.0
    num_im = ab_im
    den = a_re * a_re + a_im * a_im
    f_re = (num_re * a_re + num_im * a_im) / den
    f_im = (num_im * a_re - num_re * a_im) / den
    b_re = bre_ref[...]
    b_im = bim_ref[...]
    bb_re = jnp.tile(f_re * b_re - f_im * b_im, (SLAB_GROUPS, 1))
    bb_im = jnp.tile(f_re * b_im + f_im * b_re, (SLAB_GROUPS, 1))
    grp_row = lax.broadcasted_iota(jnp.int32, (LANES, ns), 0) >> 4
    grp_col = lax.broadcasted_iota(jnp.int32, (LANES, ns), 1) >> 6
    same_in = grp_row == grp_col
    zin = jnp.zeros((LANES, ns), F32)
    bb_re = jnp.where(same_in, bb_re, zin)
    bb_im = jnp.where(same_in, bb_im, zin)

    e_in = (SSM_CHUNK - 1 - lax.broadcasted_iota(jnp.int32, (SSM_CHUNK, 1), 0)).astype(F32)
    mag_in = jnp.exp(lam_r * e_in)
    pin_re = mag_in * jnp.cos(lam_i * e_in)
    pin_im = mag_in * jnp.sin(lam_i * e_in)
    for lp in range(SSM_CHUNK):
        p_re = pin_re[lp:lp + 1, :]
        p_im = pin_im[lp:lp + 1, :]
        w1_ref[lp * LANES:(lp + 1) * LANES, 0:ns] = (p_re * bb_re - p_im * bb_im).astype(BF16)
        w1_ref[lp * LANES:(lp + 1) * LANES, ns:2 * ns] = (p_re * bb_im + p_im * bb_re).astype(BF16)

    dt_c = jnp.exp(ldt_c[...])
    lam_rc = are_c[...] * dt_c
    lam_ic = aim_c[...] * dt_c
    lag = (lax.broadcasted_iota(jnp.int32, (1, LANES), 1) & (2 * SSM_CHUNK - 1)).astype(F32)
    grp_row_o = lax.broadcasted_iota(jnp.int32, (ns, LANES), 0) >> 6
    grp_col_o = lax.broadcasted_iota(jnp.int32, (ns, LANES), 1) >> 4
    same_out = grp_row_o == grp_col_o
    zout = jnp.zeros((ns, LANES), F32)
    c_re = jnp.where(same_out, cre_ref[...], zout)
    c_im = jnp.where(same_out, cim_ref[...], zout)

    mag_c = jnp.exp(lam_rc * lag)
    pc_re = mag_c * jnp.cos(lam_ic * lag)
    pc_im = mag_c * jnp.sin(lam_ic * lag)
    c_pow = []
    for l in range(SSM_CHUNK + 1):
        q_re = jnp.broadcast_to(pc_re[:, l:l + 1], (ns, LANES))
        q_im = jnp.broadcast_to(pc_im[:, l:l + 1], (ns, LANES))
        c_pow.append((c_re * q_re - c_im * q_im, -c_re * q_im - c_im * q_re))
    kf = []
    for l in range(SSM_CHUNK):
        w2_ref[0:ns, l * LANES:(l + 1) * LANES] = c_pow[l + 1][0].astype(BF16)
        w2_ref[ns:2 * ns, l * LANES:(l + 1) * LANES] = c_pow[l + 1][1].astype(BF16)
        kf.append((_dot_exact(bb_re, c_pow[l][0]) + _dot_exact(bb_im, c_pow[l][1])).astype(BF16))
    zblk = jnp.zeros((LANES, LANES), BF16)
    for lp in range(SSM_CHUNK):
        for l in range(SSM_CHUNK):
            w1_ref[lp * LANES:(lp + 1) * LANES, 2 * ns + l * LANES:2 * ns + (l + 1) * LANES] = (
                kf[l - lp] if l >= lp else zblk)

    prow = lax.broadcasted_iota(jnp.int32, (n_pw, 1), 0)
    dbl = jnp.clip(prow - SCAN_RADIX, 0, max(n_lvl2 - 1, 0))
    e_pw = jnp.where(prow < SCAN_RADIX, SSM_CHUNK * prow,
                     jnp.left_shift(SSM_CHUNK * SCAN_RADIX, dbl)).astype(F32)
    mag_pw = jnp.exp(lam_r * e_pw)
    pw_ref[:, 0:ns] = mag_pw * jnp.cos(lam_i * e_pw)
    pw_ref[:, ns:2 * ns] = mag_pw * jnp.sin(lam_i * e_pw)


def _ssm_setup(a_re, a_im, log_dt, b_re, b_im, c_re, c_im, n_pw, n_lvl2):
    depth = a_re.shape[0]
    g, p, h = a_re.shape[1], SSM_STATE, SSM_GROUP
    ns = SLAB_STATES
    ldt = jnp.broadcast_to(log_dt[:, :, None], (depth, g, p))

    def row(v):
        return v.reshape(depth, SSM_SLABS, 1, ns)

    def col(v):
        return v.reshape(depth, SSM_SLABS, ns, 1)

    def b_lay(v):
        return v.reshape(depth, SSM_SLABS, SLAB_GROUPS, p, h).transpose(0, 1, 4, 2, 3).reshape(
            depth, SSM_SLABS, h, ns)

    def c_lay(v):
        vt = v.reshape(depth, SSM_SLABS, SLAB_GROUPS, h, p).transpose(0, 1, 2, 4, 3).reshape(
            depth, SSM_SLABS, ns, h)
        return jnp.tile(vt, (1, 1, 1, SLAB_GROUPS))

    def spec(shape):
        return pl.BlockSpec((None, None) + shape, lambda d, q: (d, q) + (0,) * len(shape))

    return pl.pallas_call(
        functools.partial(_ssm_setup_kernel, n_pw=n_pw, n_lvl2=n_lvl2),
        grid=(depth, SSM_SLABS),
        in_specs=[spec((1, ns))] * 3 + [spec((ns, 1))] * 3 + [spec((h, ns))] * 2 + [spec((ns, LANES))] * 2,
        out_specs=[spec((CHUNK_COLS, 2 * ns + CHUNK_COLS)), spec((2 * ns, CHUNK_COLS)), spec((n_pw, 2 * ns))],
        out_shape=[
            jax.ShapeDtypeStruct((depth, SSM_SLABS, CHUNK_COLS, 2 * ns + CHUNK_COLS), BF16),
            jax.ShapeDtypeStruct((depth, SSM_SLABS, 2 * ns, CHUNK_COLS), BF16),
            jax.ShapeDtypeStruct((depth, SSM_SLABS, n_pw, 2 * ns), F32),
        ],
        compiler_params=pltpu.CompilerParams(
            dimension_semantics=("arbitrary", "arbitrary"), vmem_limit_bytes=VMEM_LIMIT),
        name="ssm_setup",
    )(row(a_re), row(a_im), row(ldt), col(a_re), col(a_im), col(ldt),
      b_lay(b_re), b_lay(b_im), c_lay(c_re), c_lay(c_im))


def _ssm_main_kernel(u_ref, w1_ref, w2_ref, pw_ref, y_ref, lhs_ref, z_scr, sp_scr, g_a, g_b,
                     *, n_chunks, n_lvl2, pad):
    ns = SLAB_STATES
    nsl = ns // LANES
    ng = n_chunks // SCAN_RADIX
    for l in range(SSM_CHUNK):
        lhs_ref[:, l * LANES:(l + 1) * LANES] = u_ref[pl.ds(l, n_chunks, stride=SSM_CHUNK), :].astype(BF16)
    zin = _dot(lhs_ref[...], w1_ref[:, 0:2 * ns])
    for j in range(2 * nsl):
        z_scr[j] = zin[:, j * LANES:(j + 1) * LANES]
    y_intra = [_dot(lhs_ref[:, 0:(c + 1) * MXU_TILE],
                    w1_ref[0:(c + 1) * MXU_TILE, 2 * ns + c * MXU_TILE:2 * ns + (c + 1) * MXU_TILE])
               for c in range(CHUNK_COLS // MXU_TILE)]

    for js in range(nsl):
        jr, ji = js, nsl + js
        lr = slice(js * LANES, (js + 1) * LANES)
        li = slice(ns + js * LANES, ns + (js + 1) * LANES)
        a_r = pw_ref[1:2, lr]
        a_i = pw_ref[1:2, li]
        xs = []
        for lo in range(SCAN_RADIX):
            z_r = z_scr[jr, pl.ds(lo, ng, stride=SCAN_RADIX), :]
            z_i = z_scr[ji, pl.ds(lo, ng, stride=SCAN_RADIX), :]
            if lo == 0:
                x_r, x_i = z_r, z_i
            else:
                x_r, x_i = a_r * x_r - a_i * x_i + z_r, a_r * x_i + a_i * x_r + z_i
            xs.append((x_r, x_i))
        ga, gb = g_a.at[js], g_b.at[js]
        zeros = jnp.zeros((pad, 2 * LANES), F32)
        ga[0:pad, :] = zeros
        gb[0:pad, :] = zeros
        ga[pad:pad + ng, 0:LANES] = x_r
        ga[pad:pad + ng, LANES:2 * LANES] = x_i
        src, dst = ga, gb
        for k in range(n_lvl2):
            d = 1 << k
            t_r = src[pad:pad + ng, 0:LANES]
            t_i = src[pad:pad + ng, LANES:2 * LANES]
            s_r = src[pad - d:pad - d + ng, 0:LANES]
            s_i = src[pad - d:pad - d + ng, LANES:2 * LANES]
            p_r = pw_ref[SCAN_RADIX + k:SCAN_RADIX + k + 1, lr]
            p_i = pw_ref[SCAN_RADIX + k:SCAN_RADIX + k + 1, li]
            dst[pad:pad + ng, 0:LANES] = t_r + p_r * s_r - p_i * s_i
            dst[pad:pad + ng, LANES:2 * LANES] = t_i + p_r * s_i + p_i * s_r
            src, dst = dst, src
        sg_r = src[pad - 1:pad - 1 + ng, 0:LANES]
        sg_i = src[pad - 1:pad - 1 + ng, LANES:2 * LANES]
        for lo in range(SCAN_RADIX):
            if lo == 0:
                s_r, s_i = sg_r, sg_i
            else:
                p_r = pw_ref[lo:lo + 1, lr]
                p_i = pw_ref[lo:lo + 1, li]
                s_r = p_r * sg_r - p_i * sg_i + xs[lo - 1][0]
                s_i = p_r * sg_i + p_i * sg_r + xs[lo - 1][1]
            sp_scr[jr, pl.ds(lo, ng, stride=SCAN_RADIX), :] = s_r
            sp_scr[ji, pl.ds(lo, ng, stride=SCAN_RADIX), :] = s_i

    s_prev = jnp.concatenate([sp_scr[j] for j in range(2 * nsl)], axis=1).astype(BF16)
    y = jnp.concatenate(y_intra, axis=1) + _dot(s_prev, w2_ref[...])
    for l in range(SSM_CHUNK):
        y_ref[pl.ds(l, n_chunks, stride=SSM_CHUNK), :] = y[:, l * LANES:(l + 1) * LANES]


def _ssm_main(pf3, w1, w2, pw, n_lvl2, layer):
    bsz, seq, _ = pf3.shape
    n_chunks = seq // SSM_CHUNK
    ns = SLAB_STATES
    nsl = ns // LANES
    ng = n_chunks // SCAN_RADIX
    pad = max(8, ng // 2)
    n_pw = pw.shape[2]
    kern = functools.partial(_ssm_main_kernel, n_chunks=n_chunks, n_lvl2=n_lvl2, pad=pad)
    return pl.pallas_call(
        kern,
        grid=(SSM_SLABS, bsz),
        in_specs=[
            pl.BlockSpec((None, seq, LANES), lambda s, b: (b, 0, s)),
            pl.BlockSpec((None, None, CHUNK_COLS, 2 * ns + CHUNK_COLS), lambda s, b: (layer, s, 0, 0)),
            pl.BlockSpec((None, None, 2 * ns, CHUNK_COLS), lambda s, b: (layer, s, 0, 0)),
            pl.BlockSpec((None, None, n_pw, 2 * ns), lambda s, b: (layer, s, 0, 0)),
        ],
        out_specs=pl.BlockSpec((None, seq, LANES), lambda s, b: (b, 0, s)),
        out_shape=jax.ShapeDtypeStruct((bsz, seq, SSM_WIDTH), F32),
        scratch_shapes=[pltpu.VMEM((n_chunks, CHUNK_COLS), BF16),
                        pltpu.VMEM((2 * nsl, n_chunks, LANES), F32),
                        pltpu.VMEM((2 * nsl, n_chunks, LANES), F32),
                        pltpu.VMEM((nsl, pad + ng, 2 * LANES), F32),
                        pltpu.VMEM((nsl, pad + ng, 2 * LANES), F32)],
        compiler_params=pltpu.CompilerParams(
            dimension_semantics=("arbitrary", "arbitrary"), vmem_limit_bytes=VMEM_LIMIT),
        name="ssm_main",
    )(pf3, w1, w2, pw)


def _attn_kernel(q_ref, k_ref, v_ref, ga_ref, tri1_ref, tri2_ref, o_ref, acc_ref, stay_ref, kv_scr,
                 *, qb, kw, nq, top, seq):
    i = pl.program_id(2)
    q0 = pl.multiple_of(i * (nq * qb), nq * qb)
    n_heads = 2 * PAIRS_PER_STEP

    @pl.when(i == 0)
    def _():
        is_a = jnp.concatenate([_head_mask()] * PAIRS_PER_STEP, axis=1)
        chunk = min(seq, 512)

        def fill(c, carry):
            r = pl.multiple_of(c * chunk, chunk)
            for j, ref in enumerate((k_ref, v_ref)):
                blk = ref[pl.ds(r, chunk), :]
                zero = jnp.zeros_like(blk)
                kv_scr[2 * j, pl.ds(r, chunk), :] = jnp.where(is_a, blk, zero)
                kv_scr[2 * j + 1, pl.ds(r, chunk), :] = jnp.where(is_a, zero, blk)
            return carry
        lax.fori_loop(0, seq // chunk, fill, 0)

    d_row = lax.broadcasted_iota(jnp.int32, (qb, qb), 0)
    d_col = lax.broadcasted_iota(jnp.int32, (qb, qb), 1)
    strict = d_col < d_row
    strict2 = jnp.concatenate([strict, strict], axis=1)
    w_col = lax.broadcasted_iota(jnp.int32, (qb, kw), 1)
    tris = {qb: tri1_ref[...], kw: tri2_ref[...]}

    def rows_into(full, part, r0, nr):
        def rest(a, b):
            return jnp.zeros((b - a, part.shape[1]), part.dtype) if full is None else full[a:b]
        pieces = ([rest(0, r0)] if r0 > 0 else []) + [part] + ([rest(r0 + nr, qb)] if r0 + nr < qb else [])
        return pieces[0] if len(pieces) == 1 else jnp.concatenate(pieces, axis=0)

    def sweep(parts, stay):
        stay = [list(row) for row in stay]
        units = [(pi, p) for pi in range(len(parts)) for p in range(PAIRS_PER_STEP)]
        zs, sps, lefts, laters, ws = {}, {}, {}, {}, {}

        def stacked(j, k0, width, p):
            cols = slice(p * LANES, (p + 1) * LANES)
            return jnp.concatenate([kv_scr[2 * j, pl.ds(k0, width), cols],
                                    kv_scr[2 * j + 1, pl.ds(k0, width), cols]], axis=0)

        def scores(u):
            pi, p = u
            s, r0, nr, k0, width, _ = parts[pi]
            qp = q_ref[s * qb + r0:s * qb + r0 + nr, p * LANES:(p + 1) * LANES]
            zs[u] = lax.dot_general(qp, stacked(0, k0, width, p), (((1,), (1,)), ((), ())),
                                    preferred_element_type=F32)

        def stick(u):
            pi, p = u
            s, r0, nr, _, width, mask = parts[pi]
            z = zs.pop(u)
            sp = jnp.maximum(z, 0.0) + jnp.log(1.0 + jnp.exp2(-jnp.abs(z))) * LOG2E
            if mask is not None:
                sp = jnp.where(mask, sp, 0.0)
            base = z - sp
            halves = []
            for hd in range(2):
                n = 2 * p + hd
                cols = slice(hd * width, (hd + 1) * width)
                tot = jnp.sum(sp[:, cols], axis=-1, keepdims=True)
                old = stay[s][n]
                if old is None:
                    assert r0 == 0 and nr == qb
                    halves.append(base[:, cols])
                    stay[s][n] = tot
                else:
                    seen = old[r0:r0 + nr]
                    halves.append(base[:, cols] - seen)
                    stay[s][n] = rows_into(old, seen + tot, r0, nr)
            sps[u] = sp.astype(BF16)
            lefts[u] = jnp.concatenate(halves, axis=1)

        def later(u):
            laters[u] = _dot(sps.pop(u), tris[parts[u[0]][4]])

        def weight(u):
            mask = parts[u[0]][5]
            w = jnp.exp2(lefts.pop(u) - laters.pop(u))
            if mask is not None:
                w = jnp.where(mask, w, 0.0)
            ws[u] = w.astype(BF16)

        for stage in (scores, stick, later, weight):
            for u in units:
                stage(u)
        pvs = [[None] * PAIRS_PER_STEP for _ in range(nq)]
        for pi, (s, r0, nr, k0, width, _) in enumerate(parts):
            for p in range(PAIRS_PER_STEP):
                d = rows_into(None, _dot(ws.pop((pi, p)), stacked(1, k0, width, p)), r0, nr)
                pvs[s][p] = d if pvs[s][p] is None else pvs[s][p] + d
        return pvs, stay

    def commit(pvs, stay, first):
        low = None
        for s in range(nq):
            for p in range(PAIRS_PER_STEP):
                acc_ref[s, p] = pvs[s][p] if first else acc_ref[s, p] + pvs[s][p]
            for n in range(n_heads):
                stay_ref[s, n] = jnp.broadcast_to(stay[s][n], (qb, LANES))
                low = stay[s][n] if low is None else jnp.minimum(low, stay[s][n])
        return jnp.min(low)

    fresh = [[None] * n_heads for _ in range(nq)]
    diags = [(s, 0, qb, pl.multiple_of(q0 + s * qb, qb), qb, strict2) for s in range(nq)]

    def stay_now():
        return [[stay_ref[s, n][:, 0:1] for n in range(n_heads)] for s in range(nq)]

    def key_block(s, back):
        return pl.multiple_of(q0 + (s - back) * qb, qb)

    def main_step():
        near = [(s, 0, qb, key_block(s, 1), qb, None) for s in range(nq)]
        far_top = [(s, 0, top, key_block(s, 2), qb, None) for s in range(nq)]
        return commit(*sweep(diags + near + far_top, fresh), True)

    def far_rest():
        parts = [(s, top, qb - top, key_block(s, 2), qb, None) for s in range(nq)]
        return commit(*sweep(parts, stay_now()), False)

    def first_step():
        prevs = [(s, 0, qb, j * qb, qb, None) for s in range(nq) for j in reversed(range(s))]
        return commit(*sweep(diags + prevs, fresh), True)

    has_window = q0 >= kw
    low1 = lax.cond(has_window, main_step, first_step)
    low2 = lax.cond(jnp.logical_and(has_window, low1 < STAY_BITS_DONE), far_rest, lambda: low1)
    start1 = jnp.where(has_window, q0 + (nq - 1) * qb - kw, 0)

    def cond(c):
        start, low = c
        return jnp.logical_and(start > 0, low < STAY_BITS_DONE)

    def body(c):
        start, _ = c
        parts = []
        for s in range(nq):
            start_s = jnp.maximum(start - (nq - 1 - s) * qb, 0)
            k0 = pl.multiple_of(jnp.maximum(start_s - kw, 0), qb)
            valid = (k0 + w_col) < start_s
            parts.append((s, 0, qb, k0, kw, jnp.concatenate([valid, valid], axis=1)))
        return jnp.maximum(start - kw, 0), commit(*sweep(parts, stay_now()), False)

    lax.while_loop(cond, body, (start1, low2))
    for s in range(nq):
        for p in range(PAIRS_PER_STEP):
            ga = ga_ref[s * qb:(s + 1) * qb, p * LANES:(p + 1) * LANES]
            o_ref[s * qb:(s + 1) * qb, p * LANES:(p + 1) * LANES] = acc_ref[s, p] * (ga * _sigmoid(ga))


def _attention(qkv3, pf3):
    bsz, seq, _ = qkv3.shape
    qb = min(ATTN_QUERY_BLOCK, seq)
    kw = min(ATTN_KEY_WINDOW, seq)
    nq = kw // qb
    blk = nq * qb
    assert seq % blk == 0 and blk == kw and nq == 2
    w = PAIRS_PER_STEP * LANES
    n_grp = HEAD_PAIRS // PAIRS_PER_STEP
    ga0 = (2 * SSM_WIDTH) // w
    kern = functools.partial(_attn_kernel, qb=qb, kw=kw, nq=nq, top=ATTN_TOP_ROWS, seq=seq)

    def pair_tri(n):
        return jnp.kron(jnp.eye(2, dtype=F32), jnp.tri(n, n, -1, dtype=F32)).astype(BF16)
    return pl.pallas_call(
        kern,
        grid=(bsz, n_grp, seq // blk),
        in_specs=[
            pl.BlockSpec((None, blk, w), lambda b, g, i: (b, i, g)),
            pl.BlockSpec((None, seq, w), lambda b, g, i: (b, 0, n_grp + g)),
            pl.BlockSpec((None, seq, w), lambda b, g, i: (b, 0, 2 * n_grp + g)),
            pl.BlockSpec((None, blk, w), lambda b, g, i: (b, i, ga0 + g)),
            pl.BlockSpec((2 * qb, 2 * qb), lambda b, g, i: (0, 0)),
            pl.BlockSpec((2 * kw, 2 * kw), lambda b, g, i: (0, 0)),
        ],
        out_specs=pl.BlockSpec((None, blk, w), lambda b, g, i: (b, i, g)),
        out_shape=jax.ShapeDtypeStruct((bsz, seq, ATTN_WIDTH), F32),
        scratch_shapes=[pltpu.VMEM((nq, PAIRS_PER_STEP, qb, LANES), F32),
                        pltpu.VMEM((nq, 2 * PAIRS_PER_STEP, qb, LANES), F32),
                        pltpu.VMEM((4, seq, w), BF16)],
        compiler_params=pltpu.CompilerParams(
            dimension_semantics=("arbitrary", "arbitrary", "arbitrary"), vmem_limit_bytes=VMEM_LIMIT),
        name="attention",
    )(qkv3, qkv3, qkv3, pf3, pair_tri(qb), pair_tri(kw))


def _out_kernel(h_ref, ys_ref, u_ref, gs_ref, ya_ref, p_ref, d_ref, wgf_ref, bg_ref, wof_ref,
                pg_ref, wpgf_ref, wppf_ref, o_ref, wg_ref, wo1_ref, wo2_ref, wpg_ref, wpp_ref):
    @pl.when(pl.program_id(0) == 0)
    def _():
        wg_ref[...] = wgf_ref[...].astype(BF16)
        wo1_ref[...] = wof_ref[0:SSM_WIDTH, :].astype(BF16)
        wo2_ref[...] = wof_ref[SSM_WIDTH:SSM_WIDTH + ATTN_WIDTH, :].astype(BF16)
        wpg_ref[...] = wpgf_ref[...].astype(BF16)
        wpp_ref[...] = wppf_ref[...].astype(BF16)

    rows = h_ref.shape[0] // OUT_SUBTILES
    subs = [slice(i * rows, (i + 1) * rows) for i in range(OUT_SUBTILES)]
    zs = []
    for sl in subs:
        y = ys_ref[sl, :] + d_ref[...] * u_ref[sl, :]
        z = 0.5 * y * (1.0 + jnp.tanh(math.sqrt(2.0 / math.pi) * (y + 0.044715 * (y * y * y))))
        zs.append(z.astype(BF16))
    zzs = [_dot(z, wg_ref[...]) + bg_ref[...] for z in zs]
    y_ssms = []
    for sl, zz in zip(subs, zzs):
        gs = gs_ref[sl, :]
        y_ssm = (zz[:, 0:SSM_WIDTH] * _sigmoid(zz[:, SSM_WIDTH:2 * SSM_WIDTH])) * (gs * _sigmoid(gs))
        y_ssms.append(y_ssm.astype(BF16))
    hs = [h_ref[sl, :] + _dot(y_ssm, wo1_ref[...]) + _dot(ya_ref[sl, :].astype(BF16), wo2_ref[...])
          for sl, y_ssm in zip(subs, y_ssms)]
    hns = []
    for h in hs:
        ms = jnp.mean(h * h, axis=-1, keepdims=True)
        hns.append(((h * lax.rsqrt(ms + RMS_EPS)) * pg_ref[...]).astype(BF16))
    gates = [_sigmoid(_dot(hn, wpg_ref[...])) for hn in hns]
    for sl, h, gate in zip(subs, hs, gates):
        o_ref[sl, :] = h + gate * _dot(p_ref[sl, :].astype(BF16), wpp_ref[...])


def _out(h, y_s, pf, y_att, p_all, d_all, w_glu_all, b_glu_all, w_out_all, ple_g_all, w_pg_all, w_pp_all, layer):
    t = h.shape[0]
    tm = min(TOKEN_TILE, t)
    return pl.pallas_call(
        _out_kernel,
        grid=(t // tm,),
        in_specs=[
            pl.BlockSpec((tm, D_MODEL), lambda i: (i, 0)),
            pl.BlockSpec((tm, SSM_WIDTH), lambda i: (i, 0)),
            pl.BlockSpec((tm, SSM_WIDTH), lambda i: (i, 0)),
            pl.BlockSpec((tm, SSM_WIDTH), lambda i: (i, 1)),
            pl.BlockSpec((tm, ATTN_WIDTH), lambda i: (i, 0)),
            pl.BlockSpec((None, tm, PLE_DIM), lambda i: (layer, i, 0)),
            _layer_spec((1, SSM_WIDTH), layer),
            _layer_spec((SSM_WIDTH, 2 * SSM_WIDTH), layer),
            _layer_spec((1, 2 * SSM_WIDTH), layer),
            _layer_spec((SSM_WIDTH + ATTN_WIDTH, D_MODEL), layer),
            _layer_spec((1, D_MODEL), layer),
            _layer_spec((D_MODEL, D_MODEL), layer),
            _layer_spec((PLE_DIM, D_MODEL), layer),
        ],
        out_specs=pl.BlockSpec((tm, D_MODEL), lambda i: (i, 0)),
        out_shape=jax.ShapeDtypeStruct((t, D_MODEL), F32),
        scratch_shapes=[pltpu.VMEM((SSM_WIDTH, 2 * SSM_WIDTH), BF16), pltpu.VMEM((SSM_WIDTH, D_MODEL), BF16),
                        pltpu.VMEM((ATTN_WIDTH, D_MODEL), BF16), pltpu.VMEM((D_MODEL, D_MODEL), BF16),
                        pltpu.VMEM((PLE_DIM, D_MODEL), BF16)],
        compiler_params=pltpu.CompilerParams(
            dimension_semantics=("arbitrary",), vmem_limit_bytes=VMEM_LIMIT),
        name="out",
    )(h, y_s, pf, pf, y_att, p_all, d_all, w_glu_all, b_glu_all, w_out_all, ple_g_all, w_pg_all, w_pp_all)


def kernel(x, p, mix_norm_g, w_in, ssm_a_re, ssm_a_im, ssm_log_dt, ssm_b_re, ssm_b_im, ssm_c_re, ssm_c_im,
           ssm_d, ssm_w_glu, ssm_b_glu, q_norm_g, k_norm_g, w_out, ple_norm_g, w_ple_gate, w_ple_proj):
    bsz, seq, _ = x.shape
    depth = w_in.shape[0]
    t = bsz * seq
    n_chunks = seq // SSM_CHUNK
    assert seq % SSM_CHUNK == 0 and t % min(TOKEN_TILE, t) == 0
    n_groups = n_chunks // SCAN_RADIX
    assert n_chunks % SCAN_RADIX == 0 and n_groups & (n_groups - 1) == 0
    n_lvl2 = n_groups.bit_length() - 1
    n_pw = -(-(SCAN_RADIX + n_lvl2) // 8) * 8

    w1, w2, pw = _ssm_setup(ssm_a_re, ssm_a_im, ssm_log_dt, ssm_b_re, ssm_b_im, ssm_c_re, ssm_c_im, n_pw, n_lvl2)
    gq2 = jnp.tile(q_norm_g.reshape(depth, 1, HEAD_DIM), (1, 1, 2))
    gk2 = jnp.tile(k_norm_g.reshape(depth, 1, HEAD_DIM), (1, 1, 2))
    mix_g = mix_norm_g.reshape(depth, 1, D_MODEL)
    ple_g = ple_norm_g.reshape(depth, 1, D_MODEL)
    d_skip = ssm_d.reshape(depth, 1, SSM_WIDTH)
    b_glu = ssm_b_glu.reshape(depth, 1, 2 * SSM_WIDTH)
    p3 = p.reshape(depth, t, PLE_DIM)

    h = x.reshape(t, D_MODEL)
    for i in range(depth):
        pf, qkv = _inproj(h, mix_g, w_in, gq2, gk2, i)
        pf3 = pf.reshape(bsz, seq, F32_COLS)
        y_s = _ssm_main(pf3, w1, w2, pw, n_lvl2, i).reshape(t, SSM_WIDTH)
        y_att = _attention(qkv.reshape(bsz, seq, QKV_COLS), pf3).reshape(t, ATTN_WIDTH)
        h = _out(h, y_s, pf, y_att, p3, d_skip, ssm_w_glu, b_glu, w_out, ple_g, w_ple_gate, w_ple_proj, i)
    return h.reshape(bsz, seq, D_MODEL)
```

```python
import functools
import math

import jax
import jax.numpy as jnp
from jax import lax
from jax.experimental import pallas as pl
from jax.experimental.pallas import tpu as pltpu

F32 = jnp.float32
BF16 = jnp.bfloat16

D_MODEL = 1024
PLE_DIM = 256
SSM_WIDTH = 512
SSM_GROUP = 16
SSM_STATE = 64
LANES = 128
MXU_TILE = 256
SLAB_GROUPS = LANES // SSM_GROUP
SSM_SLABS = SSM_WIDTH // LANES
SLAB_STATES = SLAB_GROUPS * SSM_STATE
SSM_CHUNK = 8
CHUNK_COLS = SSM_CHUNK * LANES
SSM_SEQS_PER_STEP = 1
ROW_GRANULE = 16
SCAN_RADIX = 8
ATTN_WIDTH = 512
HEAD_DIM = 64
HEAD_PAIRS = ATTN_WIDTH // LANES
PAIRS_PER_STEP = 4
IN_COLS = 3072
F32_COLS = 2 * SSM_WIDTH + ATTN_WIDTH
QKV_COLS = 3 * ATTN_WIDTH
RMS_EPS = 1e-6
LOG2E = 1.4426950408889634
ATTN_QUERY_BLOCK = 128
ATTN_TOP_ROWS = 48
ATTN_KEY_WINDOW = 256
STAY_BITS_DONE = 152.0
TOKEN_TILE = 512
OUT_SUBTILES = 2
VMEM_LIMIT = 56 * 1024 * 1024


def _sigmoid(x):
    return 1.0 / (1.0 + jnp.exp(-x))


def _dot(a, b):
    return jnp.dot(a, b, preferred_element_type=F32)


def _dot_exact(a, b):
    return jnp.dot(a, b, preferred_element_type=F32, precision=lax.Precision.HIGHEST)


def _head_mask():
    return lax.broadcasted_iota(jnp.int32, (1, LANES), 1) < HEAD_DIM


def _headnorm(x, g, is_a):
    x2 = x * x
    s_a = jnp.sum(jnp.where(is_a, x2, 0.0), axis=-1, keepdims=True)
    s_b = jnp.sum(jnp.where(is_a, 0.0, x2), axis=-1, keepdims=True)
    ms = jnp.where(is_a, s_a, s_b) * (1.0 / HEAD_DIM)
    return (x * lax.rsqrt(ms + RMS_EPS)) * g


def _inproj_kernel(x_ref, g_ref, w_ref, gq_ref, gk_ref, f_ref, qkv_ref, w_scr):
    @pl.when(pl.program_id(0) == 0)
    def _():
        a0 = 2 * SSM_WIDTH
        w_scr[:, 0:a0] = w_ref[:, 0:a0].astype(BF16)
        w_scr[:, a0:F32_COLS] = w_ref[:, a0 + QKV_COLS:IN_COLS].astype(BF16)
        w_scr[:, F32_COLS:IN_COLS] = w_ref[:, a0:a0 + QKV_COLS].astype(BF16)

    x = x_ref[...]
    ms = jnp.mean(x * x, axis=-1, keepdims=True)
    hn = (x * lax.rsqrt(ms + RMS_EPS)) * g_ref[...]
    res = _dot(hn.astype(BF16), w_scr[...])
    f_ref[...] = res[:, 0:F32_COLS]
    is_a = _head_mask()
    q_scale = (HEAD_DIM ** -0.5) * LOG2E
    for s in range(HEAD_PAIRS):
        c0 = F32_COLS + s * LANES
        q = _headnorm(res[:, c0:c0 + LANES], gq_ref[...], is_a) * q_scale
        qkv_ref[:, s * LANES:(s + 1) * LANES] = q.astype(BF16)
        c1 = c0 + ATTN_WIDTH
        k = _headnorm(res[:, c1:c1 + LANES], gk_ref[...], is_a)
        qkv_ref[:, ATTN_WIDTH + s * LANES:ATTN_WIDTH + (s + 1) * LANES] = k.astype(BF16)
    qkv_ref[:, 2 * ATTN_WIDTH:3 * ATTN_WIDTH] = res[:, F32_COLS + 2 * ATTN_WIDTH:IN_COLS].astype(BF16)


def _layer_spec(shape, layer):
    return pl.BlockSpec((None,) + shape, lambda i: (layer,) + (0,) * len(shape), pipeline_mode=pl.Buffered(1))


def _inproj(h, g_all, w_all, gq_all, gk_all, layer):
    t = h.shape[0]
    tm = min(TOKEN_TILE, t)
    return pl.pallas_call(
        _inproj_kernel,
        grid=(t // tm,),
        in_specs=[
            pl.BlockSpec((tm, D_MODEL), lambda i: (i, 0)),
            _layer_spec((1, D_MODEL), layer),
            _layer_spec((D_MODEL, IN_COLS), layer),
            _layer_spec((1, LANES), layer),
            _layer_spec((1, LANES), layer),
        ],
        out_specs=[pl.BlockSpec((tm, F32_COLS), lambda i: (i, 0)),
                   pl.BlockSpec((tm, QKV_COLS), lambda i: (i, 0))],
        out_shape=[jax.ShapeDtypeStruct((t, F32_COLS), F32), jax.ShapeDtypeStruct((t, QKV_COLS), BF16)],
        scratch_shapes=[pltpu.VMEM((D_MODEL, IN_COLS), BF16)],
        compiler_params=pltpu.CompilerParams(
            dimension_semantics=("arbitrary",), vmem_limit_bytes=VMEM_LIMIT),
        name="inproj",
    )(h, g_all, w_all, gq_all, gk_all)


def _ssm_setup_kernel(are_r, aim_r, ldt_r, bre_ref, bim_ref, cre_ref, cim_ref,
                      w1_ref, w2_ref, pw_ref, *, n_pw, n_lvl2):
    ns = SLAB_STATES
    a_re = are_r[...]
    a_im = aim_r[...]
    dt = jnp.exp(ldt_r[...])
    lam_r = a_re * dt
    lam_i = a_im * dt
    mag = jnp.exp(lam_r)
    ab_re = mag * jnp.cos(lam_i)
    ab_im = mag * jnp.sin(lam_i)
    num_re = ab_re - 1.0
    num_im = ab_im
    den = a_re * a_re + a_im * a_im
    f_re = (num_re * a_re + num_im * a_im) / den
    f_im = (num_im * a_re - num_re * a_im) / den
    b_re = bre_ref[...]
    b_im = bim_ref[...]
    bb_re = jnp.tile(f_re * b_re - f_im * b_im, (SLAB_GROUPS, 1))
    bb_im = jnp.tile(f_re * b_im + f_im * b_re, (SLAB_GROUPS, 1))
    grp_row = lax.broadcasted_iota(jnp.int32, (LANES, ns), 0) >> 4
    grp_col = lax.broadcasted_iota(jnp.int32, (LANES, ns), 1) >> 6
    same_in = grp_row == grp_col
    zin = jnp.zeros((LANES, ns), F32)
    bb_re = jnp.where(same_in, bb_re, zin)
    bb_im = jnp.where(same_in, bb_im, zin)

    e_in = (SSM_CHUNK - 1 - lax.broadcasted_iota(jnp.int32, (SSM_CHUNK, 1), 0)).astype(F32)
    mag_in = jnp.exp(lam_r * e_in)
    pin_re = mag_in * jnp.cos(lam_i * e_in)
    pin_im = mag_in * jnp.sin(lam_i * e_in)
    for lp in range(SSM_CHUNK):
        p_re = pin_re[lp:lp + 1, :]
        p_im = pin_im[lp:lp + 1, :]
        w1_ref[lp * LANES:(lp + 1) * LANES, 0:ns] = (p_re * bb_re - p_im * bb_im).astype(BF16)
        w1_ref[lp * LANES:(lp + 1) * LANES, ns:2 * ns] = (p_re * bb_im + p_im * bb_re).astype(BF16)

    grp_row_o = lax.broadcasted_iota(jnp.int32, (ns, LANES), 0) >> 6
    grp_col_o = lax.broadcasted_iota(jnp.int32, (ns, LANES), 1) >> 4
    same_out = grp_row_o == grp_col_o
    zout = jnp.zeros((ns, LANES), F32)
    c_re = jnp.where(same_out, cre_ref[...], zout)
    c_im = jnp.where(same_out, cim_ref[...], zout)

    lag = lax.broadcasted_iota(jnp.int32, (2 * SSM_CHUNK, 1), 0).astype(F32)
    mag_c = jnp.exp(lam_r * lag)
    fill = jnp.zeros((LANES - 2 * SSM_CHUNK, ns), F32)
    pc_re = jnp.concatenate([mag_c * jnp.cos(lam_i * lag), fill], axis=0).T
    pc_im = jnp.concatenate([mag_c * jnp.sin(lam_i * lag), fill], axis=0).T
    c_pow = []
    for l in range(SSM_CHUNK + 1):
        q_re = jnp.broadcast_to(pc_re[:, l:l + 1], (ns, LANES))
        q_im = jnp.broadcast_to(pc_im[:, l:l + 1], (ns, LANES))
        c_pow.append((c_re * q_re - c_im * q_im, -c_re * q_im - c_im * q_re))
    kf = []
    for l in range(SSM_CHUNK):
        w2_ref[0:ns, l * LANES:(l + 1) * LANES] = c_pow[l + 1][0].astype(BF16)
        w2_ref[ns:2 * ns, l * LANES:(l + 1) * LANES] = c_pow[l + 1][1].astype(BF16)
        kf.append((_dot_exact(bb_re, c_pow[l][0]) + _dot_exact(bb_im, c_pow[l][1])).astype(BF16))
    zblk = jnp.zeros((LANES, LANES), BF16)
    for lp in range(SSM_CHUNK):
        for l in range(SSM_CHUNK):
            w1_ref[lp * LANES:(lp + 1) * LANES, 2 * ns + l * LANES:2 * ns + (l + 1) * LANES] = (
                kf[l - lp] if l >= lp else zblk)

    prow = lax.broadcasted_iota(jnp.int32, (n_pw, 1), 0)
    dbl = jnp.clip(prow - SCAN_RADIX, 0, max(n_lvl2 - 1, 0))
    e_pw = jnp.where(prow < SCAN_RADIX, SSM_CHUNK * prow,
                     jnp.left_shift(SSM_CHUNK * SCAN_RADIX, dbl)).astype(F32)
    mag_pw = jnp.exp(lam_r * e_pw)
    pw_ref[:, 0:ns] = mag_pw * jnp.cos(lam_i * e_pw)
    pw_ref[:, ns:2 * ns] = mag_pw * jnp.sin(lam_i * e_pw)


def _ssm_setup(a_re, a_im, log_dt, b_re, b_im, c_re, c_im, n_pw, n_lvl2):
    depth = a_re.shape[0]
    g, p, h = a_re.shape[1], SSM_STATE, SSM_GROUP
    ns = SLAB_STATES
    ldt = jnp.broadcast_to(log_dt[:, :, None], (depth, g, p))

    def row(v):
        return v.reshape(depth, SSM_SLABS, 1, ns)

    def b_lay(v):
        return v.reshape(depth, SSM_SLABS, SLAB_GROUPS, p, h).transpose(0, 1, 4, 2, 3).reshape(
            depth, SSM_SLABS, h, ns)

    def c_lay(v):
        vt = v.reshape(depth, SSM_SLABS, SLAB_GROUPS, h, p).transpose(0, 1, 2, 4, 3).reshape(
            depth, SSM_SLABS, ns, h)
        return jnp.tile(vt, (1, 1, 1, SLAB_GROUPS))

    def spec(shape):
        return pl.BlockSpec((None, None) + shape, lambda d, q: (d, q) + (0,) * len(shape))

    return pl.pallas_call(
        functools.partial(_ssm_setup_kernel, n_pw=n_pw, n_lvl2=n_lvl2),
        grid=(depth, SSM_SLABS),
        in_specs=[spec((1, ns))] * 3 + [spec((h, ns))] * 2 + [spec((ns, LANES))] * 2,
        out_specs=[spec((CHUNK_COLS, 2 * ns + CHUNK_COLS)), spec((2 * ns, CHUNK_COLS)), spec((n_pw, 2 * ns))],
        out_shape=[
            jax.ShapeDtypeStruct((depth, SSM_SLABS, CHUNK_COLS, 2 * ns + CHUNK_COLS), BF16),
            jax.ShapeDtypeStruct((depth, SSM_SLABS, 2 * ns, CHUNK_COLS), BF16),
            jax.ShapeDtypeStruct((depth, SSM_SLABS, n_pw, 2 * ns), F32),
        ],
        compiler_params=pltpu.CompilerParams(
            dimension_semantics=("arbitrary", "arbitrary"), vmem_limit_bytes=VMEM_LIMIT),
        name="ssm_setup",
    )(row(a_re), row(a_im), row(ldt), b_lay(b_re), b_lay(b_im), c_lay(c_re), c_lay(c_im))


def _ssm_main_kernel(u_all, w1_ref, w2_ref, pw_ref, y_all, *scratch, n_chunks, n_lvl2, pad, nb):
    per = len(scratch) // nb
    y_intras = [None] * nb
    for phase in (_ssm_project, _ssm_scan, _ssm_emit):
        for bb in range(nb):
            phase(u_all.at[bb], w1_ref, w2_ref, pw_ref, y_all.at[bb], y_intras, bb,
                  *scratch[bb * per:(bb + 1) * per], n_chunks=n_chunks, n_lvl2=n_lvl2, pad=pad)


def _ssm_project(u_ref, w1_ref, w2_ref, pw_ref, y_ref, y_intras, bb, lhs_ref, z_scr, sp_scr, g_a, g_b,
                 *, n_chunks, n_lvl2, pad):
    ns = SLAB_STATES
    nsl = ns // LANES
    for l in range(SSM_CHUNK):
        lhs_ref[:, l * LANES:(l + 1) * LANES] = u_ref[pl.ds(l, n_chunks, stride=SSM_CHUNK), :].astype(BF16)
    zin = _dot(lhs_ref[...], w1_ref[:, 0:2 * ns])
    for j in range(2 * nsl):
        z_scr[j] = zin[:, j * LANES:(j + 1) * LANES]
    y_intras[bb] = [_dot(lhs_ref[:, 0:(c + 1) * MXU_TILE],
                         w1_ref[0:(c + 1) * MXU_TILE, 2 * ns + c * MXU_TILE:2 * ns + (c + 1) * MXU_TILE])
                    for c in range(CHUNK_COLS // MXU_TILE)]


def _ssm_scan(u_ref, w1_ref, w2_ref, pw_ref, y_ref, y_intras, bb, lhs_ref, z_scr, sp_scr, g_a, g_b,
              *, n_chunks, n_lvl2, pad):
    ns = SLAB_STATES
    nsl = ns // LANES
    ng = n_chunks // SCAN_RADIX
    for js in range(nsl):
        jr, ji = js, nsl + js
        lr = slice(js * LANES, (js + 1) * LANES)
        li = slice(ns + js * LANES, ns + (js + 1) * LANES)
        a_r = pw_ref[1:2, lr]
        a_i = pw_ref[1:2, li]
        xs = []
        for lo in range(SCAN_RADIX):
            z_r = z_scr[jr, pl.ds(lo, ng, stride=SCAN_RADIX), :]
            z_i = z_scr[ji, pl.ds(lo, ng, stride=SCAN_RADIX), :]
            if lo == 0:
                x_r, x_i = z_r, z_i
            else:
                x_r, x_i = a_r * x_r - a_i * x_i + z_r, a_r * x_i + a_i * x_r + z_i
            xs.append((x_r, x_i))
        ga, gb = g_a.at[js], g_b.at[js]
        zeros = jnp.zeros((pad, 2 * LANES), F32)
        ga[0:pad, :] = zeros
        gb[0:pad, :] = zeros
        ga[pad:pad + ng, 0:LANES] = x_r
        ga[pad:pad + ng, LANES:2 * LANES] = x_i
        src, dst = ga, gb
        for k in range(n_lvl2):
            d = 1 << k
            t_r = src[pad:pad + ng, 0:LANES]
            t_i = src[pad:pad + ng, LANES:2 * LANES]
            s_r = src[pad - d:pad - d + ng, 0:LANES]
            s_i = src[pad - d:pad - d + ng, LANES:2 * LANES]
            p_r = pw_ref[SCAN_RADIX + k:SCAN_RADIX + k + 1, lr]
            p_i = pw_ref[SCAN_RADIX + k:SCAN_RADIX + k + 1, li]
            dst[pad:pad + ng, 0:LANES] = t_r + p_r * s_r - p_i * s_i
            dst[pad:pad + ng, LANES:2 * LANES] = t_i + p_r * s_i + p_i * s_r
            src, dst = dst, src
        sg_r = src[pad - 1:pad - 1 + ng, 0:LANES]
        sg_i = src[pad - 1:pad - 1 + ng, LANES:2 * LANES]
        for lo in range(SCAN_RADIX):
            if lo == 0:
                s_r, s_i = sg_r, sg_i
            else:
                p_r = pw_ref[lo:lo + 1, lr]
                p_i = pw_ref[lo:lo + 1, li]
                s_r = p_r * sg_r - p_i * sg_i + xs[lo - 1][0]
                s_i = p_r * sg_i + p_i * sg_r + xs[lo - 1][1]
            sp_scr[jr, pl.ds(lo, ng, stride=SCAN_RADIX), :] = s_r
            sp_scr[ji, pl.ds(lo, ng, stride=SCAN_RADIX), :] = s_i


def _ssm_emit(u_ref, w1_ref, w2_ref, pw_ref, y_ref, y_intras, bb, lhs_ref, z_scr, sp_scr, g_a, g_b,
              *, n_chunks, n_lvl2, pad):
    nsl = SLAB_STATES // LANES
    s_prev = jnp.concatenate([sp_scr[j] for j in range(2 * nsl)], axis=1).astype(BF16)
    y = jnp.concatenate(y_intras[bb], axis=1) + _dot(s_prev, w2_ref[...])
    for l in range(SSM_CHUNK):
        y_ref[pl.ds(l, n_chunks, stride=SSM_CHUNK), :] = y[:, l * LANES:(l + 1) * LANES]


def _ssm_main(pf3, w1, w2, pw, n_lvl2, layer):
    bsz, seq, _ = pf3.shape
    n_chunks = seq // SSM_CHUNK
    ns = SLAB_STATES
    nsl = ns // LANES
    ng = n_chunks // SCAN_RADIX
    pad = max(8, ng // 2)
    n_pw = pw.shape[2]
    nb = SSM_SEQS_PER_STEP if bsz % SSM_SEQS_PER_STEP == 0 else 1
    kern = functools.partial(_ssm_main_kernel, n_chunks=n_chunks, n_lvl2=n_lvl2, pad=pad, nb=nb)
    return pl.pallas_call(
        kern,
        grid=(SSM_SLABS, bsz // nb),
        in_specs=[
            pl.BlockSpec((nb, seq, LANES), lambda s, b: (b, 0, s)),
            pl.BlockSpec((None, None, CHUNK_COLS, 2 * ns + CHUNK_COLS), lambda s, b: (layer, s, 0, 0)),
            pl.BlockSpec((None, None, 2 * ns, CHUNK_COLS), lambda s, b: (layer, s, 0, 0)),
            pl.BlockSpec((None, None, n_pw, 2 * ns), lambda s, b: (layer, s, 0, 0)),
        ],
        out_specs=pl.BlockSpec((nb, seq, LANES), lambda s, b: (b, 0, s)),
        out_shape=jax.ShapeDtypeStruct((bsz, seq, SSM_WIDTH), F32),
        scratch_shapes=[pltpu.VMEM((n_chunks, CHUNK_COLS), BF16),
                        pltpu.VMEM((2 * nsl, n_chunks, LANES), F32),
                        pltpu.VMEM((2 * nsl, n_chunks, LANES), F32),
                        pltpu.VMEM((nsl, pad + ng, 2 * LANES), F32),
                        pltpu.VMEM((nsl, pad + ng, 2 * LANES), F32)] * nb,
        compiler_params=pltpu.CompilerParams(
            dimension_semantics=("arbitrary", "arbitrary"), vmem_limit_bytes=VMEM_LIMIT),
        name="ssm_main",
    )(pf3, w1, w2, pw)


def _attn_kernel(q_ref, k_ref, v_ref, tri0_ref, tri1_ref, tri2_ref, o_ref, acc_ref, stay_ref, kv_scr,
                 *, qb, kw, nq, top, seq):
    i = pl.program_id(2)
    q0 = pl.multiple_of(i * (nq * qb), nq * qb)
    n_heads = 2 * PAIRS_PER_STEP

    @pl.when(i == 0)
    def _():
        is_a = jnp.concatenate([_head_mask()] * PAIRS_PER_STEP, axis=1)
        chunk = min(seq, 512)

        def fill(c, carry):
            r = pl.multiple_of(c * chunk, chunk)
            for j, ref in enumerate((k_ref, v_ref)):
                blk = ref[pl.ds(r, chunk), :]
                zero = jnp.zeros_like(blk)
                kv_scr[2 * j, pl.ds(r, chunk), :] = jnp.where(is_a, blk, zero)
                kv_scr[2 * j + 1, pl.ds(r, chunk), :] = jnp.where(is_a, zero, blk)
            return carry
        lax.fori_loop(0, seq // chunk, fill, 0)

    hq = qb // 2
    t_row = lax.broadcasted_iota(jnp.int32, (hq, 2 * hq), 0)
    t_key = lax.broadcasted_iota(jnp.int32, (hq, 2 * hq), 1) & (hq - 1)
    b_row = lax.broadcasted_iota(jnp.int32, (hq, 2 * qb), 0) + hq
    b_key = lax.broadcasted_iota(jnp.int32, (hq, 2 * qb), 1) & (qb - 1)
    causal_top = t_key < t_row
    causal_bottom = b_key < b_row
    w_col = lax.broadcasted_iota(jnp.int32, (qb, kw), 1)
    tris = {hq: tri0_ref[...], qb: tri1_ref[...], kw: tri2_ref[...]}

    def rows_into(full, part, r0, nr):
        def rest(a, b):
            return jnp.zeros((b - a, part.shape[1]), part.dtype) if full is None else full[a:b]
        pieces = ([rest(0, r0)] if r0 > 0 else []) + [part] + ([rest(r0 + nr, qb)] if r0 + nr < qb else [])
        return pieces[0] if len(pieces) == 1 else jnp.concatenate(pieces, axis=0)

    def sweep(parts, stay):
        stay = [list(row) for row in stay]
        unswept = [[set(range(0, qb, ROW_GRANULE)) if st is None else set() for st in row] for row in stay]
        units = [(pi, p) for pi in range(len(parts)) for p in range(PAIRS_PER_STEP)]
        zs, sps, lefts, laters, ws = {}, {}, {}, {}, {}

        def stacked(j, k0, width, p):
            cols = slice(p * LANES, (p + 1) * LANES)
            return jnp.concatenate([kv_scr[2 * j, pl.ds(k0, width), cols],
                                    kv_scr[2 * j + 1, pl.ds(k0, width), cols]], axis=0)

        def scores(u):
            pi, p = u
            s, r0, nr, k0, width, _ = parts[pi]
            qp = q_ref[s * qb + r0:s * qb + r0 + nr, p * LANES:(p + 1) * LANES]
            zs[u] = lax.dot_general(qp, stacked(0, k0, width, p), (((1,), (1,)), ((), ())),
                                    preferred_element_type=F32)

        def stick(u):
            pi, p = u
            s, r0, nr, _, width, mask = parts[pi]
            z = zs.pop(u)
            sp = jnp.maximum(z, 0.0) + jnp.log(1.0 + jnp.exp2(-jnp.abs(z))) * LOG2E
            if mask is not None:
                sp = jnp.where(mask, sp, 0.0)
            base = z - sp
            touched = set(range(r0, r0 + nr, ROW_GRANULE))
            halves = []
            for hd in range(2):
                n = 2 * p + hd
                if width % LANES == 0:
                    cols = slice(hd * width, (hd + 1) * width)
                    tot = jnp.sum(sp[:, cols], axis=-1, keepdims=True)
                else:
                    in_a = lax.broadcasted_iota(jnp.int32, (1, 2 * width), 1) < width
                    tot = jnp.sum(jnp.where(in_a == (hd == 0), sp, 0.0), axis=-1, keepdims=True)
                if touched <= unswept[s][n]:
                    halves.append(None)
                    rows = tot
                else:
                    assert width % LANES == 0
                    seen = stay[s][n][r0:r0 + nr]
                    halves.append(base[:, cols] - seen)
                    rows = seen + tot
                unswept[s][n] = unswept[s][n] - touched
                stay[s][n] = rows if nr == qb else rows_into(stay[s][n], rows, r0, nr)
            sps[u] = sp.astype(BF16)
            lefts[u] = base if halves[0] is None else jnp.concatenate(halves, axis=1)

        def later(u):
            laters[u] = _dot(sps.pop(u), tris[parts[u[0]][4]])

        def weight(u):
            mask = parts[u[0]][5]
            w = jnp.exp2(lefts.pop(u) - laters.pop(u))
            if mask is not None:
                w = jnp.where(mask, w, 0.0)
            ws[u] = w.astype(BF16)

        for stage in (scores, stick, later, weight):
            for u in units:
                stage(u)
        pvs = [[None] * PAIRS_PER_STEP for _ in range(nq)]
        for pi, (s, r0, nr, k0, width, _) in enumerate(parts):
            for p in range(PAIRS_PER_STEP):
                d = rows_into(None, _dot(ws.pop((pi, p)), stacked(1, k0, width, p)), r0, nr)
                pvs[s][p] = d if pvs[s][p] is None else pvs[s][p] + d
        return pvs, stay

    def commit(pvs, stay, first):
        low = None
        for s in range(nq):
            for p in range(PAIRS_PER_STEP):
                acc_ref[s, p] = pvs[s][p] if first else acc_ref[s, p] + pvs[s][p]
            for n in range(n_heads):
                stay_ref[s, n] = jnp.broadcast_to(stay[s][n], (qb, LANES))
                low = stay[s][n] if low is None else jnp.minimum(low, stay[s][n])
        return jnp.min(low)

    fresh = [[None] * n_heads for _ in range(nq)]
    diags = []
    for s in range(nq):
        k_diag = pl.multiple_of(q0 + s * qb, qb)
        diags += [(s, 0, hq, k_diag, hq, causal_top), (s, hq, hq, k_diag, qb, causal_bottom)]

    def stay_now():
        return [[stay_ref[s, n][:, 0:1] for n in range(n_heads)] for s in range(nq)]

    def key_block(s, back):
        return pl.multiple_of(q0 + (s - back) * qb, qb)

    def main_step():
        near = [(s, 0, qb, key_block(s, 1), qb, None) for s in range(nq)]
        far_top = [(s, 0, top, key_block(s, 2), qb, None) for s in range(nq)]
        return commit(*sweep(diags + near + far_top, fresh), True)

    def far_rest():
        parts = [(s, top, qb - top, key_block(s, 2), qb, None) for s in range(nq)]
        return commit(*sweep(parts, stay_now()), False)

    def first_step():
        prevs = [(s, 0, qb, j * qb, qb, None) for s in range(nq) for j in reversed(range(s))]
        return commit(*sweep(diags + prevs, fresh), True)

    has_window = q0 >= kw
    low1 = lax.cond(has_window, main_step, first_step)
    low2 = lax.cond(jnp.logical_and(has_window, low1 < STAY_BITS_DONE), far_rest, lambda: low1)
    start1 = jnp.where(has_window, q0 + (nq - 1) * qb - kw, 0)

    def cond(c):
        start, low = c
        return jnp.logical_and(start > 0, low < STAY_BITS_DONE)

    def body(c):
        start, _ = c
        parts = []
        for s in range(nq):
            start_s = jnp.maximum(start - (nq - 1 - s) * qb, 0)
            k0 = pl.multiple_of(jnp.maximum(start_s - kw, 0), qb)
            valid = (k0 + w_col) < start_s
            parts.append((s, 0, qb, k0, kw, jnp.concatenate([valid, valid], axis=1)))
        return jnp.maximum(start - kw, 0), commit(*sweep(parts, stay_now()), False)

    lax.while_loop(cond, body, (start1, low2))
    for s in range(nq):
        for p in range(PAIRS_PER_STEP):
            o_ref[s * qb:(s + 1) * qb, p * LANES:(p + 1) * LANES] = acc_ref[s, p]


def _attention(qkv3):
    bsz, seq, _ = qkv3.shape
    qb = min(ATTN_QUERY_BLOCK, seq)
    kw = min(ATTN_KEY_WINDOW, seq)
    nq = kw // qb
    blk = nq * qb
    assert seq % blk == 0 and blk == kw and nq == 2
    w = PAIRS_PER_STEP * LANES
    n_grp = HEAD_PAIRS // PAIRS_PER_STEP
    kern = functools.partial(_attn_kernel, qb=qb, kw=kw, nq=nq, top=ATTN_TOP_ROWS, seq=seq)

    def pair_tri(n):
        return jnp.kron(jnp.eye(2, dtype=F32), jnp.tri(n, n, -1, dtype=F32)).astype(BF16)
    return pl.pallas_call(
        kern,
        grid=(bsz, n_grp, seq // blk),
        in_specs=[
            pl.BlockSpec((None, blk, w), lambda b, g, i: (b, i, g)),
            pl.BlockSpec((None, seq, w), lambda b, g, i: (b, 0, n_grp + g)),
            pl.BlockSpec((None, seq, w), lambda b, g, i: (b, 0, 2 * n_grp + g)),
            pl.BlockSpec((qb, qb), lambda b, g, i: (0, 0)),
            pl.BlockSpec((2 * qb, 2 * qb), lambda b, g, i: (0, 0)),
            pl.BlockSpec((2 * kw, 2 * kw), lambda b, g, i: (0, 0)),
        ],
        out_specs=pl.BlockSpec((None, blk, w), lambda b, g, i: (b, i, g)),
        out_shape=jax.ShapeDtypeStruct((bsz, seq, ATTN_WIDTH), F32),
        scratch_shapes=[pltpu.VMEM((nq, PAIRS_PER_STEP, qb, LANES), F32),
                        pltpu.VMEM((nq, 2 * PAIRS_PER_STEP, qb, LANES), F32),
                        pltpu.VMEM((4, seq, w), BF16)],
        compiler_params=pltpu.CompilerParams(
            dimension_semantics=("arbitrary", "arbitrary", "arbitrary"), vmem_limit_bytes=VMEM_LIMIT),
        name="attention",
    )(qkv3, qkv3, qkv3, pair_tri(qb // 2), pair_tri(qb), pair_tri(kw))


def _out_kernel(h_ref, ys_ref, u_ref, gs_ref, ga_ref, ya_ref, p_ref, d_ref, wgf_ref, bg_ref, wof_ref,
                pg_ref, wpgf_ref, wppf_ref, o_ref, wg_ref, wo1_ref, wo2_ref, wpg_ref, wpp_ref):
    @pl.when(pl.program_id(0) == 0)
    def _():
        wg_ref[...] = wgf_ref[...].astype(BF16)
        wo1_ref[...] = wof_ref[0:SSM_WIDTH, :].astype(BF16)
        wo2_ref[...] = wof_ref[SSM_WIDTH:SSM_WIDTH + ATTN_WIDTH, :].astype(BF16)
        wpg_ref[...] = wpgf_ref[...].astype(BF16)
        wpp_ref[...] = wppf_ref[...].astype(BF16)

    rows = h_ref.shape[0] // OUT_SUBTILES
    subs = [slice(i * rows, (i + 1) * rows) for i in range(OUT_SUBTILES)]
    zs = []
    for sl in subs:
        y = ys_ref[sl, :] + d_ref[...] * u_ref[sl, :]
        z = 0.5 * y * (1.0 + jnp.tanh(math.sqrt(2.0 / math.pi) * (y + 0.044715 * (y * y * y))))
        zs.append(z.astype(BF16))
    zzs = [_dot(z, wg_ref[...]) + bg_ref[...] for z in zs]
    y_ssms = []
    for sl, zz in zip(subs, zzs):
        gs = gs_ref[sl, :]
        y_ssm = (zz[:, 0:SSM_WIDTH] * _sigmoid(zz[:, SSM_WIDTH:2 * SSM_WIDTH])) * (gs * _sigmoid(gs))
        y_ssms.append(y_ssm.astype(BF16))
    y_atts = []
    for sl in subs:
        ga = ga_ref[sl, :]
        y_atts.append((ya_ref[sl, :] * (ga * _sigmoid(ga))).astype(BF16))
    hs = [h_ref[sl, :] + _dot(y_ssm, wo1_ref[...]) + _dot(y_att, wo2_ref[...])
          for sl, y_ssm, y_att in zip(subs, y_ssms, y_atts)]
    hns = []
    for h in hs:
        ms = jnp.mean(h * h, axis=-1, keepdims=True)
        hns.append(((h * lax.rsqrt(ms + RMS_EPS)) * pg_ref[...]).astype(BF16))
    gates = [_sigmoid(_dot(hn, wpg_ref[...])) for hn in hns]
    for sl, h, gate in zip(subs, hs, gates):
        o_ref[sl, :] = h + gate * _dot(p_ref[sl, :].astype(BF16), wpp_ref[...])


def _out(h, y_s, pf, y_att, p_all, d_all, w_glu_all, b_glu_all, w_out_all, ple_g_all, w_pg_all, w_pp_all, layer):
    t = h.shape[0]
    tm = min(TOKEN_TILE, t)
    return pl.pallas_call(
        _out_kernel,
        grid=(t // tm,),
        in_specs=[
            pl.BlockSpec((tm, D_MODEL), lambda i: (i, 0)),
            pl.BlockSpec((tm, SSM_WIDTH), lambda i: (i, 0)),
            pl.BlockSpec((tm, SSM_WIDTH), lambda i: (i, 0)),
            pl.BlockSpec((tm, SSM_WIDTH), lambda i: (i, 1)),
            pl.BlockSpec((tm, ATTN_WIDTH), lambda i: (i, 2)),
            pl.BlockSpec((tm, ATTN_WIDTH), lambda i: (i, 0)),
            pl.BlockSpec((None, tm, PLE_DIM), lambda i: (layer, i, 0)),
            _layer_spec((1, SSM_WIDTH), layer),
            _layer_spec((SSM_WIDTH, 2 * SSM_WIDTH), layer),
            _layer_spec((1, 2 * SSM_WIDTH), layer),
            _layer_spec((SSM_WIDTH + ATTN_WIDTH, D_MODEL), layer),
            _layer_spec((1, D_MODEL), layer),
            _layer_spec((D_MODEL, D_MODEL), layer),
            _layer_spec((PLE_DIM, D_MODEL), layer),
        ],
        out_specs=pl.BlockSpec((tm, D_MODEL), lambda i: (i, 0)),
        out_shape=jax.ShapeDtypeStruct((t, D_MODEL), F32),
        scratch_shapes=[pltpu.VMEM((SSM_WIDTH, 2 * SSM_WIDTH), BF16), pltpu.VMEM((SSM_WIDTH, D_MODEL), BF16),
                        pltpu.VMEM((ATTN_WIDTH, D_MODEL), BF16), pltpu.VMEM((D_MODEL, D_MODEL), BF16),
                        pltpu.VMEM((PLE_DIM, D_MODEL), BF16)],
        compiler_params=pltpu.CompilerParams(
            dimension_semantics=("arbitrary",), vmem_limit_bytes=VMEM_LIMIT),
        name="out",
    )(h, y_s, pf, pf, pf, y_att, p_all, d_all, w_glu_all, b_glu_all, w_out_all, ple_g_all, w_pg_all, w_pp_all)


def kernel(x, p, mix_norm_g, w_in, ssm_a_re, ssm_a_im, ssm_log_dt, ssm_b_re, ssm_b_im, ssm_c_re, ssm_c_im,
           ssm_d, ssm_w_glu, ssm_b_glu, q_norm_g, k_norm_g, w_out, ple_norm_g, w_ple_gate, w_ple_proj):
    bsz, seq, _ = x.shape
    depth = w_in.shape[0]
    t = bsz * seq
    n_chunks = seq // SSM_CHUNK
    assert seq % SSM_CHUNK == 0 and t % min(TOKEN_TILE, t) == 0
    n_groups = n_chunks // SCAN_RADIX
    assert n_chunks % SCAN_RADIX == 0 and n_groups & (n_groups - 1) == 0
    n_lvl2 = n_groups.bit_length() - 1
    n_pw = -(-(SCAN_RADIX + n_lvl2) // 8) * 8

    w1, w2, pw = _ssm_setup(ssm_a_re, ssm_a_im, ssm_log_dt, ssm_b_re, ssm_b_im, ssm_c_re, ssm_c_im, n_pw, n_lvl2)
    gq2 = jnp.tile(q_norm_g.reshape(depth, 1, HEAD_DIM), (1, 1, 2))
    gk2 = jnp.tile(k_norm_g.reshape(depth, 1, HEAD_DIM), (1, 1, 2))
    mix_g = mix_norm_g.reshape(depth, 1, D_MODEL)
    ple_g = ple_norm_g.reshape(depth, 1, D_MODEL)
    d_skip = ssm_d.reshape(depth, 1, SSM_WIDTH)
    b_glu = ssm_b_glu.reshape(depth, 1, 2 * SSM_WIDTH)
    p3 = p.reshape(depth, t, PLE_DIM)

    h = x.reshape(t, D_MODEL)
    for i in range(depth):
        pf, qkv = _inproj(h, mix_g, w_in, gq2, gk2, i)
        pf3 = pf.reshape(bsz, seq, F32_COLS)
        y_s = _ssm_main(pf3, w1, w2, pw, n_lvl2, i).reshape(t, SSM_WIDTH)
        y_att = _attention(qkv.reshape(bsz, seq, QKV_COLS)).reshape(t, ATTN_WIDTH)
        h = _out(h, y_s, pf, y_att, p3, d_skip, ssm_w_glu, b_glu, w_out, ple_g, w_ple_gate, w_ple_proj, i)
    return h.reshape(bsz, seq, D_MODEL)
```

```python
import functools
import math

import jax
import jax.numpy as jnp
from jax import lax
from jax.experimental import pallas as pl
from jax.experimental.pallas import tpu as pltpu

F32 = jnp.float32
BF16 = jnp.bfloat16

D_MODEL = 1024
PLE_DIM = 256
SSM_WIDTH = 512
SSM_GROUP = 16
SSM_STATE = 64
LANES = 128
MXU_TILE = 256
SLAB_GROUPS = LANES // SSM_GROUP
SSM_SLABS = SSM_WIDTH // LANES
SLAB_STATES = SLAB_GROUPS * SSM_STATE
SSM_CHUNK = 8
CHUNK_COLS = SSM_CHUNK * LANES
SSM_SEQS_PER_STEP = 1
ROW_GRANULE = 16
SCAN_RADIX = 8
ATTN_WIDTH = 512
HEAD_DIM = 64
HEAD_PAIRS = ATTN_WIDTH // LANES
PAIRS_PER_STEP = 4
IN_COLS = 3072
F32_COLS = 2 * SSM_WIDTH + ATTN_WIDTH
QKV_COLS = 3 * ATTN_WIDTH
RMS_EPS = 1e-6
LOG2E = 1.4426950408889634
ATTN_QUERY_BLOCK = 128
ATTN_TOP_ROWS = 32
ATTN_KEY_WINDOW = 256
STAY_BITS_DONE = 152.0
TOKEN_TILE = 512
OUT_SUBTILES = 2
VMEM_LIMIT = 56 * 1024 * 1024


def _sigmoid(x):
    return 1.0 / (1.0 + jnp.exp(-x))


def _dot(a, b):
    return jnp.dot(a, b, preferred_element_type=F32)


def _dot_exact(a, b):
    return jnp.dot(a, b, preferred_element_type=F32, precision=lax.Precision.HIGHEST)


def _head_mask():
    return lax.broadcasted_iota(jnp.int32, (1, LANES), 1) < HEAD_DIM


def _headnorm(x, g, is_a):
    x2 = x * x
    s_a = jnp.sum(jnp.where(is_a, x2, 0.0), axis=-1, keepdims=True)
    s_b = jnp.sum(jnp.where(is_a, 0.0, x2), axis=-1, keepdims=True)
    ms = jnp.where(is_a, s_a, s_b) * (1.0 / HEAD_DIM)
    return (x * lax.rsqrt(ms + RMS_EPS)) * g


def _inproj_kernel(x_ref, g_ref, w_ref, gq_ref, gk_ref, f_ref, qkv_ref, w_scr):
    @pl.when(pl.program_id(0) == 0)
    def _():
        a0 = 2 * SSM_WIDTH
        w_scr[:, 0:a0] = w_ref[:, 0:a0].astype(BF16)
        w_scr[:, a0:F32_COLS] = w_ref[:, a0 + QKV_COLS:IN_COLS].astype(BF16)
        w_scr[:, F32_COLS:IN_COLS] = w_ref[:, a0:a0 + QKV_COLS].astype(BF16)

    x = x_ref[...]
    ms = jnp.mean(x * x, axis=-1, keepdims=True)
    hn = (x * lax.rsqrt(ms + RMS_EPS)) * g_ref[...]
    res = _dot(hn.astype(BF16), w_scr[...])
    f_ref[...] = res[:, 0:F32_COLS]
    is_a = _head_mask()
    q_scale = (HEAD_DIM ** -0.5) * LOG2E
    for s in range(HEAD_PAIRS):
        c0 = F32_COLS + s * LANES
        q = _headnorm(res[:, c0:c0 + LANES], gq_ref[...], is_a) * q_scale
        qkv_ref[:, s * LANES:(s + 1) * LANES] = q.astype(BF16)
        c1 = c0 + ATTN_WIDTH
        k = _headnorm(res[:, c1:c1 + LANES], gk_ref[...], is_a)
        qkv_ref[:, ATTN_WIDTH + s * LANES:ATTN_WIDTH + (s + 1) * LANES] = k.astype(BF16)
    qkv_ref[:, 2 * ATTN_WIDTH:3 * ATTN_WIDTH] = res[:, F32_COLS + 2 * ATTN_WIDTH:IN_COLS].astype(BF16)


def _layer_spec(shape, layer):
    return pl.BlockSpec((None,) + shape, lambda i: (layer,) + (0,) * len(shape), pipeline_mode=pl.Buffered(1))


def _inproj(h, g_all, w_all, gq_all, gk_all, layer):
    t = h.shape[0]
    tm = min(TOKEN_TILE, t)
    return pl.pallas_call(
        _inproj_kernel,
        grid=(t // tm,),
        in_specs=[
            pl.BlockSpec((tm, D_MODEL), lambda i: (i, 0)),
            _layer_spec((1, D_MODEL), layer),
            _layer_spec((D_MODEL, IN_COLS), layer),
            _layer_spec((1, LANES), layer),
            _layer_spec((1, LANES), layer),
        ],
        out_specs=[pl.BlockSpec((tm, F32_COLS), lambda i: (i, 0)),
                   pl.BlockSpec((tm, QKV_COLS), lambda i: (i, 0))],
        out_shape=[jax.ShapeDtypeStruct((t, F32_COLS), F32), jax.ShapeDtypeStruct((t, QKV_COLS), BF16)],
        scratch_shapes=[pltpu.VMEM((D_MODEL, IN_COLS), BF16)],
        compiler_params=pltpu.CompilerParams(
            dimension_semantics=("arbitrary",), vmem_limit_bytes=VMEM_LIMIT),
        name="inproj",
    )(h, g_all, w_all, gq_all, gk_all)


def _ssm_setup_kernel(are_r, aim_r, ldt_r, bre_ref, bim_ref, cre_ref, cim_ref,
                      w1_ref, w2_ref, pw_ref, *, n_pw, n_lvl2):
    ns = SLAB_STATES
    a_re = are_r[...]
    a_im = aim_r[...]
    dt = jnp.exp(ldt_r[...])
    lam_r = a_re * dt
    lam_i = a_im * dt
    mag = jnp.exp(lam_r)
    ab_re = mag * jnp.cos(lam_i)
    ab_im = mag * jnp.sin(lam_i)
    num_re = ab_re - 1.0
    num_im = ab_im
    den = a_re * a_re + a_im * a_im
    f_re = (num_re * a_re + num_im * a_im) / den
    f_im = (num_im * a_re - num_re * a_im) / den
    b_re = bre_ref[...]
    b_im = bim_ref[...]
    bb_re = jnp.tile(f_re * b_re - f_im * b_im, (SLAB_GROUPS, 1))
    bb_im = jnp.tile(f_re * b_im + f_im * b_re, (SLAB_GROUPS, 1))
    grp_row = lax.broadcasted_iota(jnp.int32, (LANES, ns), 0) >> 4
    grp_col = lax.broadcasted_iota(jnp.int32, (LANES, ns), 1) >> 6
    same_in = grp_row == grp_col
    zin = jnp.zeros((LANES, ns), F32)
    bb_re = jnp.where(same_in, bb_re, zin)
    bb_im = jnp.where(same_in, bb_im, zin)

    e_in = (SSM_CHUNK - 1 - lax.broadcasted_iota(jnp.int32, (SSM_CHUNK, 1), 0)).astype(F32)
    mag_in = jnp.exp(lam_r * e_in)
    pin_re = mag_in * jnp.cos(lam_i * e_in)
    pin_im = mag_in * jnp.sin(lam_i * e_in)
    for lp in range(SSM_CHUNK):
        p_re = pin_re[lp:lp + 1, :]
        p_im = pin_im[lp:lp + 1, :]
        w1_ref[lp * LANES:(lp + 1) * LANES, 0:ns] = (p_re * bb_re - p_im * bb_im).astype(BF16)
        w1_ref[lp * LANES:(lp + 1) * LANES, ns:2 * ns] = (p_re * bb_im + p_im * bb_re).astype(BF16)

    grp_row_o = lax.broadcasted_iota(jnp.int32, (ns, LANES), 0) >> 6
    grp_col_o = lax.broadcasted_iota(jnp.int32, (ns, LANES), 1) >> 4
    same_out = grp_row_o == grp_col_o
    zout = jnp.zeros((ns, LANES), F32)
    c_re = jnp.where(same_out, cre_ref[...], zout)
    c_im = jnp.where(same_out, cim_ref[...], zout)

    lag = lax.broadcasted_iota(jnp.int32, (2 * SSM_CHUNK, 1), 0).astype(F32)
    mag_c = jnp.exp(lam_r * lag)
    fill = jnp.zeros((LANES - 2 * SSM_CHUNK, ns), F32)
    pc_re = jnp.concatenate([mag_c * jnp.cos(lam_i * lag), fill], axis=0).T
    pc_im = jnp.concatenate([mag_c * jnp.sin(lam_i * lag), fill], axis=0).T
    c_pow = []
    for l in range(SSM_CHUNK + 1):
        q_re = jnp.broadcast_to(pc_re[:, l:l + 1], (ns, LANES))
        q_im = jnp.broadcast_to(pc_im[:, l:l + 1], (ns, LANES))
        c_pow.append((c_re * q_re - c_im * q_im, -c_re * q_im - c_im * q_re))
    kf = []
    for l in range(SSM_CHUNK):
        w2_ref[0:ns, l * LANES:(l + 1) * LANES] = c_pow[l + 1][0].astype(BF16)
        w2_ref[ns:2 * ns, l * LANES:(l + 1) * LANES] = c_pow[l + 1][1].astype(BF16)
        kf.append((_dot_exact(bb_re, c_pow[l][0]) + _dot_exact(bb_im, c_pow[l][1])).astype(BF16))
    zblk = jnp.zeros((LANES, LANES), BF16)
    for lp in range(SSM_CHUNK):
        for l in range(SSM_CHUNK):
            w1_ref[lp * LANES:(lp + 1) * LANES, 2 * ns + l * LANES:2 * ns + (l + 1) * LANES] = (
                kf[l - lp] if l >= lp else zblk)

    prow = lax.broadcasted_iota(jnp.int32, (n_pw, 1), 0)
    dbl = jnp.clip(prow - SCAN_RADIX, 0, max(n_lvl2 - 1, 0))
    e_pw = jnp.where(prow < SCAN_RADIX, SSM_CHUNK * prow,
                     jnp.left_shift(SSM_CHUNK * SCAN_RADIX, dbl)).astype(F32)
    mag_pw = jnp.exp(lam_r * e_pw)
    pw_ref[:, 0:ns] = mag_pw * jnp.cos(lam_i * e_pw)
    pw_ref[:, ns:2 * ns] = mag_pw * jnp.sin(lam_i * e_pw)


def _ssm_setup(a_re, a_im, log_dt, b_re, b_im, c_re, c_im, n_pw, n_lvl2):
    depth = a_re.shape[0]
    g, p, h = a_re.shape[1], SSM_STATE, SSM_GROUP
    ns = SLAB_STATES
    ldt = jnp.broadcast_to(log_dt[:, :, None], (depth, g, p))

    def row(v):
        return v.reshape(depth, SSM_SLABS, 1, ns)

    def b_lay(v):
        return v.reshape(depth, SSM_SLABS, SLAB_GROUPS, p, h).transpose(0, 1, 4, 2, 3).reshape(
            depth, SSM_SLABS, h, ns)

    def c_lay(v):
        vt = v.reshape(depth, SSM_SLABS, SLAB_GROUPS, h, p).transpose(0, 1, 2, 4, 3).reshape(
            depth, SSM_SLABS, ns, h)
        return jnp.tile(vt, (1, 1, 1, SLAB_GROUPS))

    def spec(shape):
        return pl.BlockSpec((None, None) + shape, lambda d, q: (d, q) + (0,) * len(shape))

    return pl.pallas_call(
        functools.partial(_ssm_setup_kernel, n_pw=n_pw, n_lvl2=n_lvl2),
        grid=(depth, SSM_SLABS),
        in_specs=[spec((1, ns))] * 3 + [spec((h, ns))] * 2 + [spec((ns, LANES))] * 2,
        out_specs=[spec((CHUNK_COLS, 2 * ns + CHUNK_COLS)), spec((2 * ns, CHUNK_COLS)), spec((n_pw, 2 * ns))],
        out_shape=[
            jax.ShapeDtypeStruct((depth, SSM_SLABS, CHUNK_COLS, 2 * ns + CHUNK_COLS), BF16),
            jax.ShapeDtypeStruct((depth, SSM_SLABS, 2 * ns, CHUNK_COLS), BF16),
            jax.ShapeDtypeStruct((depth, SSM_SLABS, n_pw, 2 * ns), F32),
        ],
        compiler_params=pltpu.CompilerParams(
            dimension_semantics=("arbitrary", "arbitrary"), vmem_limit_bytes=VMEM_LIMIT),
        name="ssm_setup",
    )(row(a_re), row(a_im), row(ldt), b_lay(b_re), b_lay(b_im), c_lay(c_re), c_lay(c_im))


def _ssm_main_kernel(u_all, w1_ref, w2_ref, pw_ref, y_all, *scratch, n_chunks, n_lvl2, pad, nb):
    per = len(scratch) // nb
    y_intras = [None] * nb
    for phase in (_ssm_project, _ssm_scan, _ssm_emit):
        for bb in range(nb):
            phase(u_all.at[bb], w1_ref, w2_ref, pw_ref, y_all.at[bb], y_intras, bb,
                  *scratch[bb * per:(bb + 1) * per], n_chunks=n_chunks, n_lvl2=n_lvl2, pad=pad)


def _ssm_project(u_ref, w1_ref, w2_ref, pw_ref, y_ref, y_intras, bb, lhs_ref, z_scr, sp_scr, g_a, g_b,
                 *, n_chunks, n_lvl2, pad):
    ns = SLAB_STATES
    nsl = ns // LANES
    for l in range(SSM_CHUNK):
        lhs_ref[:, l * LANES:(l + 1) * LANES] = u_ref[pl.ds(l, n_chunks, stride=SSM_CHUNK), :].astype(BF16)
    zin = _dot(lhs_ref[...], w1_ref[:, 0:2 * ns])
    for j in range(2 * nsl):
        z_scr[j] = zin[:, j * LANES:(j + 1) * LANES]
    y_intras[bb] = [_dot(lhs_ref[:, 0:(c + 1) * MXU_TILE],
                         w1_ref[0:(c + 1) * MXU_TILE, 2 * ns + c * MXU_TILE:2 * ns + (c + 1) * MXU_TILE])
                    for c in range(CHUNK_COLS // MXU_TILE)]


def _ssm_scan(u_ref, w1_ref, w2_ref, pw_ref, y_ref, y_intras, bb, lhs_ref, z_scr, sp_scr, g_a, g_b,
              *, n_chunks, n_lvl2, pad):
    ns = SLAB_STATES
    nsl = ns // LANES
    ng = n_chunks // SCAN_RADIX
    for js in range(nsl):
        jr, ji = js, nsl + js
        lr = slice(js * LANES, (js + 1) * LANES)
        li = slice(ns + js * LANES, ns + (js + 1) * LANES)
        a_r = pw_ref[1:2, lr]
        a_i = pw_ref[1:2, li]
        xs = []
        for lo in range(SCAN_RADIX):
            z_r = z_scr[jr, pl.ds(lo, ng, stride=SCAN_RADIX), :]
            z_i = z_scr[ji, pl.ds(lo, ng, stride=SCAN_RADIX), :]
            if lo == 0:
                x_r, x_i = z_r, z_i
            else:
                x_r, x_i = a_r * x_r - a_i * x_i + z_r, a_r * x_i + a_i * x_r + z_i
            xs.append((x_r, x_i))
        ga, gb = g_a.at[js], g_b.at[js]
        zeros = jnp.zeros((pad, 2 * LANES), F32)
        ga[0:pad, :] = zeros
        gb[0:pad, :] = zeros
        ga[pad:pad + ng, 0:LANES] = x_r
        ga[pad:pad + ng, LANES:2 * LANES] = x_i
        src, dst = ga, gb
        for k in range(n_lvl2):
            d = 1 << k
            t_r = src[pad:pad + ng, 0:LANES]
            t_i = src[pad:pad + ng, LANES:2 * LANES]
            s_r = src[pad - d:pad - d + ng, 0:LANES]
            s_i = src[pad - d:pad - d + ng, LANES:2 * LANES]
            p_r = pw_ref[SCAN_RADIX + k:SCAN_RADIX + k + 1, lr]
            p_i = pw_ref[SCAN_RADIX + k:SCAN_RADIX + k + 1, li]
            dst[pad:pad + ng, 0:LANES] = t_r + p_r * s_r - p_i * s_i
            dst[pad:pad + ng, LANES:2 * LANES] = t_i + p_r * s_i + p_i * s_r
            src, dst = dst, src
        sg_r = src[pad - 1:pad - 1 + ng, 0:LANES]
        sg_i = src[pad - 1:pad - 1 + ng, LANES:2 * LANES]
        for lo in range(SCAN_RADIX):
            if lo == 0:
                s_r, s_i = sg_r, sg_i
            else:
                p_r = pw_ref[lo:lo + 1, lr]
                p_i = pw_ref[lo:lo + 1, li]
                s_r = p_r * sg_r - p_i * sg_i + xs[lo - 1][0]
                s_i = p_r * sg_i + p_i * sg_r + xs[lo - 1][1]
            sp_scr[jr, pl.ds(lo, ng, stride=SCAN_RADIX), :] = s_r
            sp_scr[ji, pl.ds(lo, ng, stride=SCAN_RADIX), :] = s_i


def _ssm_emit(u_ref, w1_ref, w2_ref, pw_ref, y_ref, y_intras, bb, lhs_ref, z_scr, sp_scr, g_a, g_b,
              *, n_chunks, n_lvl2, pad):
    nsl = SLAB_STATES // LANES
    s_prev = jnp.concatenate([sp_scr[j] for j in range(2 * nsl)], axis=1).astype(BF16)
    y = jnp.concatenate(y_intras[bb], axis=1) + _dot(s_prev, w2_ref[...])
    for l in range(SSM_CHUNK):
        y_ref[pl.ds(l, n_chunks, stride=SSM_CHUNK), :] = y[:, l * LANES:(l + 1) * LANES]


def _ssm_main(pf3, w1, w2, pw, n_lvl2, layer):
    bsz, seq, _ = pf3.shape
    n_chunks = seq // SSM_CHUNK
    ns = SLAB_STATES
    nsl = ns // LANES
    ng = n_chunks // SCAN_RADIX
    pad = max(8, ng // 2)
    n_pw = pw.shape[2]
    nb = SSM_SEQS_PER_STEP if bsz % SSM_SEQS_PER_STEP == 0 else 1
    kern = functools.partial(_ssm_main_kernel, n_chunks=n_chunks, n_lvl2=n_lvl2, pad=pad, nb=nb)
    return pl.pallas_call(
        kern,
        grid=(SSM_SLABS, bsz // nb),
        in_specs=[
            pl.BlockSpec((nb, seq, LANES), lambda s, b: (b, 0, s)),
            pl.BlockSpec((None, None, CHUNK_COLS, 2 * ns + CHUNK_COLS), lambda s, b: (layer, s, 0, 0)),
            pl.BlockSpec((None, None, 2 * ns, CHUNK_COLS), lambda s, b: (layer, s, 0, 0)),
            pl.BlockSpec((None, None, n_pw, 2 * ns), lambda s, b: (layer, s, 0, 0)),
        ],
        out_specs=pl.BlockSpec((nb, seq, LANES), lambda s, b: (b, 0, s)),
        out_shape=jax.ShapeDtypeStruct((bsz, seq, SSM_WIDTH), F32),
        scratch_shapes=[pltpu.VMEM((n_chunks, CHUNK_COLS), BF16),
                        pltpu.VMEM((2 * nsl, n_chunks, LANES), F32),
                        pltpu.VMEM((2 * nsl, n_chunks, LANES), F32),
                        pltpu.VMEM((nsl, pad + ng, 2 * LANES), F32),
                        pltpu.VMEM((nsl, pad + ng, 2 * LANES), F32)] * nb,
        compiler_params=pltpu.CompilerParams(
            dimension_semantics=("arbitrary", "arbitrary"), vmem_limit_bytes=VMEM_LIMIT),
        name="ssm_main",
    )(pf3, w1, w2, pw)


def _attn_kernel(q_ref, k_ref, v_ref, tri1_ref, tri2_ref, o_ref, acc_ref, stay_ref, kv_scr,
                 *, qb, kw, nq, top, seq):
    i = pl.program_id(2)
    q0 = pl.multiple_of(i * (nq * qb), nq * qb)
    n_heads = 2 * PAIRS_PER_STEP

    @pl.when(i == 0)
    def _():
        is_a = jnp.concatenate([_head_mask()] * PAIRS_PER_STEP, axis=1)
        chunk = min(seq, 512)

        def fill(c, carry):
            r = pl.multiple_of(c * chunk, chunk)
            for j, ref in enumerate((k_ref, v_ref)):
                blk = ref[pl.ds(r, chunk), :]
                zero = jnp.zeros_like(blk)
                kv_scr[2 * j, pl.ds(r, chunk), :] = jnp.where(is_a, blk, zero)
                kv_scr[2 * j + 1, pl.ds(r, chunk), :] = jnp.where(is_a, zero, blk)
            return carry
        lax.fori_loop(0, seq // chunk, fill, 0)

    d_row = lax.broadcasted_iota(jnp.int32, (qb, 2 * qb), 0)
    d_key = lax.broadcasted_iota(jnp.int32, (qb, 2 * qb), 1) & (qb - 1)
    causal = d_key < d_row
    w_col = lax.broadcasted_iota(jnp.int32, (qb, kw), 1)
    tris = {qb: tri1_ref[...], kw: tri2_ref[...]}

    def rows_into(full, part, r0, nr):
        def rest(a, b):
            return jnp.zeros((b - a, part.shape[1]), part.dtype) if full is None else full[a:b]
        pieces = ([rest(0, r0)] if r0 > 0 else []) + [part] + ([rest(r0 + nr, qb)] if r0 + nr < qb else [])
        return pieces[0] if len(pieces) == 1 else jnp.concatenate(pieces, axis=0)

    def sweep(parts, stay):
        stay = [list(row) for row in stay]
        unswept = [[set(range(0, qb, ROW_GRANULE)) if st is None else set() for st in row] for row in stay]
        units = [(pi, p) for pi in range(len(parts)) for p in range(PAIRS_PER_STEP)]
        zs, sps, lefts, laters, ws = {}, {}, {}, {}, {}

        def stacked(j, k0, width, p):
            cols = slice(p * LANES, (p + 1) * LANES)
            return jnp.concatenate([kv_scr[2 * j, pl.ds(k0, width), cols],
                                    kv_scr[2 * j + 1, pl.ds(k0, width), cols]], axis=0)

        def scores(u):
            pi, p = u
            s, r0, nr, k0, width, _ = parts[pi]
            qp = q_ref[s * qb + r0:s * qb + r0 + nr, p * LANES:(p + 1) * LANES]
            zs[u] = lax.dot_general(qp, stacked(0, k0, width, p), (((1,), (1,)), ((), ())),
                                    preferred_element_type=F32)

        def stick(u):
            pi, p = u
            s, r0, nr, _, width, mask = parts[pi]
            z = zs.pop(u)
            sp = jnp.maximum(z, 0.0) + jnp.log(1.0 + jnp.exp2(-jnp.abs(z))) * LOG2E
            if mask is not None:
                sp = jnp.where(mask, sp, 0.0)
            base = z - sp
            touched = set(range(r0, r0 + nr, ROW_GRANULE))
            halves = []
            for hd in range(2):
                n = 2 * p + hd
                cols = slice(hd * width, (hd + 1) * width)
                tot = jnp.sum(sp[:, cols], axis=-1, keepdims=True)
                if touched <= unswept[s][n]:
                    halves.append(None)
                    rows = tot
                else:
                    seen = stay[s][n][r0:r0 + nr]
                    halves.append(base[:, cols] - seen)
                    rows = seen + tot
                unswept[s][n] = unswept[s][n] - touched
                stay[s][n] = rows if nr == qb else rows_into(stay[s][n], rows, r0, nr)
            sps[u] = sp.astype(BF16)
            lefts[u] = base if halves[0] is None else jnp.concatenate(halves, axis=1)

        def later(u):
            laters[u] = _dot(sps.pop(u), tris[parts[u[0]][4]])

        def weight(u):
            mask = parts[u[0]][5]
            w = jnp.exp2(lefts.pop(u) - laters.pop(u))
            if mask is not None:
                w = jnp.where(mask, w, 0.0)
            ws[u] = w.astype(BF16)

        for stage in (scores, stick, later, weight):
            for u in units:
                stage(u)
        pvs = [[None] * PAIRS_PER_STEP for _ in range(nq)]
        for pi, (s, r0, nr, k0, width, _) in enumerate(parts):
            for p in range(PAIRS_PER_STEP):
                d = rows_into(None, _dot(ws.pop((pi, p)), stacked(1, k0, width, p)), r0, nr)
                pvs[s][p] = d if pvs[s][p] is None else pvs[s][p] + d
        return pvs, stay

    def commit(pvs, stay, first):
        low = None
        for s in range(nq):
            for p in range(PAIRS_PER_STEP):
                acc_ref[s, p] = pvs[s][p] if first else acc_ref[s, p] + pvs[s][p]
            for n in range(n_heads):
                stay_ref[s, n] = jnp.broadcast_to(stay[s][n], (qb, LANES))
                low = stay[s][n] if low is None else jnp.minimum(low, stay[s][n])
        return jnp.min(low)

    fresh = [[None] * n_heads for _ in range(nq)]
    diags = [(s, 0, qb, pl.multiple_of(q0 + s * qb, qb), qb, causal) for s in range(nq)]

    def stay_now():
        return [[stay_ref[s, n][:, 0:1] for n in range(n_heads)] for s in range(nq)]

    def key_block(s, back):
        return pl.multiple_of(q0 + (s - back) * qb, qb)

    def main_step():
        near = [(s, 0, qb, key_block(s, 1), qb, None) for s in range(nq)]
        far_top = [(s, 0, top, key_block(s, 2), qb, None) for s in range(nq)]
        return commit(*sweep(diags + near + far_top, fresh), True)

    def far_rest():
        parts = [(s, top, qb - top, key_block(s, 2), qb, None) for s in range(nq)]
        return commit(*sweep(parts, stay_now()), False)

    def first_step():
        prevs = [(s, 0, qb, j * qb, qb, None) for s in range(nq) for j in reversed(range(s))]
        return commit(*sweep(diags + prevs, fresh), True)

    has_window = q0 >= kw
    low1 = lax.cond(has_window, main_step, first_step)
    low2 = lax.cond(jnp.logical_and(has_window, low1 < STAY_BITS_DONE), far_rest, lambda: low1)
    start1 = jnp.where(has_window, q0 + (nq - 1) * qb - kw, 0)

    def cond(c):
        start, low = c
        return jnp.logical_and(start > 0, low < STAY_BITS_DONE)

    def body(c):
        start, _ = c
        parts = []
        for s in range(nq):
            start_s = jnp.maximum(start - (nq - 1 - s) * qb, 0)
            k0 = pl.multiple_of(jnp.maximum(start_s - kw, 0), qb)
            valid = (k0 + w_col) < start_s
            parts.append((s, 0, qb, k0, kw, jnp.concatenate([valid, valid], axis=1)))
        return jnp.maximum(start - kw, 0), commit(*sweep(parts, stay_now()), False)

    lax.while_loop(cond, body, (start1, low2))
    for s in range(nq):
        for p in range(PAIRS_PER_STEP):
            o_ref[s * qb:(s + 1) * qb, p * LANES:(p + 1) * LANES] = acc_ref[s, p]


def _attention(qkv3):
    bsz, seq, _ = qkv3.shape
    qb = min(ATTN_QUERY_BLOCK, seq)
    kw = min(ATTN_KEY_WINDOW, seq)
    nq = kw // qb
    blk = nq * qb
    assert seq % blk == 0 and blk == kw and nq == 2
    w = PAIRS_PER_STEP * LANES
    n_grp = HEAD_PAIRS // PAIRS_PER_STEP
    kern = functools.partial(_attn_kernel, qb=qb, kw=kw, nq=nq, top=ATTN_TOP_ROWS, seq=seq)

    def pair_tri(n):
        return jnp.kron(jnp.eye(2, dtype=F32), jnp.tri(n, n, -1, dtype=F32)).astype(BF16)
    return pl.pallas_call(
        kern,
        grid=(bsz, n_grp, seq // blk),
        in_specs=[
            pl.BlockSpec((None, blk, w), lambda b, g, i: (b, i, g)),
            pl.BlockSpec((None, seq, w), lambda b, g, i: (b, 0, n_grp + g)),
            pl.BlockSpec((None, seq, w), lambda b, g, i: (b, 0, 2 * n_grp + g)),
            pl.BlockSpec((2 * qb, 2 * qb), lambda b, g, i: (0, 0)),
            pl.BlockSpec((2 * kw, 2 * kw), lambda b, g, i: (0, 0)),
        ],
        out_specs=pl.BlockSpec((None, blk, w), lambda b, g, i: (b, i, g)),
        out_shape=jax.ShapeDtypeStruct((bsz, seq, ATTN_WIDTH), F32),
        scratch_shapes=[pltpu.VMEM((nq, PAIRS_PER_STEP, qb, LANES), F32),
                        pltpu.VMEM((nq, 2 * PAIRS_PER_STEP, qb, LANES), F32),
                        pltpu.VMEM((4, seq, w), BF16)],
        compiler_params=pltpu.CompilerParams(
            dimension_semantics=("arbitrary", "arbitrary", "arbitrary"), vmem_limit_bytes=VMEM_LIMIT),
        name="attention",
    )(qkv3, qkv3, qkv3, pair_tri(qb), pair_tri(kw))


def _out_kernel(h_ref, ys_ref, u_ref, gs_ref, ga_ref, ya_ref, p_ref, d_ref, wgf_ref, bg_ref, wof_ref,
                pg_ref, wpgf_ref, wppf_ref, o_ref, wg_ref, wo1_ref, wo2_ref, wpg_ref, wpp_ref):
    @pl.when(pl.program_id(0) == 0)
    def _():
        wg_ref[...] = wgf_ref[...].astype(BF16)
        wo1_ref[...] = wof_ref[0:SSM_WIDTH, :].astype(BF16)
        wo2_ref[...] = wof_ref[SSM_WIDTH:SSM_WIDTH + ATTN_WIDTH, :].astype(BF16)
        wpg_ref[...] = wpgf_ref[...].astype(BF16)
        wpp_ref[...] = wppf_ref[...].astype(BF16)

    rows = h_ref.shape[0] // OUT_SUBTILES
    subs = [slice(i * rows, (i + 1) * rows) for i in range(OUT_SUBTILES)]
    zs = []
    for sl in subs:
        y = ys_ref[sl, :] + d_ref[...] * u_ref[sl, :]
        z = 0.5 * y * (1.0 + jnp.tanh(math.sqrt(2.0 / math.pi) * (y + 0.044715 * (y * y * y))))
        zs.append(z.astype(BF16))
    zzs = [_dot(z, wg_ref[...]) + bg_ref[...] for z in zs]
    y_ssms = []
    for sl, zz in zip(subs, zzs):
        gs = gs_ref[sl, :]
        y_ssm = (zz[:, 0:SSM_WIDTH] * _sigmoid(zz[:, SSM_WIDTH:2 * SSM_WIDTH])) * (gs * _sigmoid(gs))
        y_ssms.append(y_ssm.astype(BF16))
    y_atts = []
    for sl in subs:
        ga = ga_ref[sl, :]
        y_atts.append((ya_ref[sl, :] * (ga * _sigmoid(ga))).astype(BF16))
    hs = [h_ref[sl, :] + _dot(y_ssm, wo1_ref[...]) + _dot(y_att, wo2_ref[...])
          for sl, y_ssm, y_att in zip(subs, y_ssms, y_atts)]
    hns = []
    for h in hs:
        ms = jnp.mean(h * h, axis=-1, keepdims=True)
        hns.append(((h * lax.rsqrt(ms + RMS_EPS)) * pg_ref[...]).astype(BF16))
    gates = [_sigmoid(_dot(hn, wpg_ref[...])) for hn in hns]
    for sl, h, gate in zip(subs, hs, gates):
        o_ref[sl, :] = h + gate * _dot(p_ref[sl, :].astype(BF16), wpp_ref[...])


def _out(h, y_s, pf, y_att, p_all, d_all, w_glu_all, b_glu_all, w_out_all, ple_g_all, w_pg_all, w_pp_all, layer):
    t = h.shape[0]
    tm = min(TOKEN_TILE, t)
    return pl.pallas_call(
        _out_kernel,
        grid=(t // tm,),
        in_specs=[
            pl.BlockSpec((tm, D_MODEL), lambda i: (i, 0)),
            pl.BlockSpec((tm, SSM_WIDTH), lambda i: (i, 0)),
            pl.BlockSpec((tm, SSM_WIDTH), lambda i: (i, 0)),
            pl.BlockSpec((tm, SSM_WIDTH), lambda i: (i, 1)),
            pl.BlockSpec((tm, ATTN_WIDTH), lambda i: (i, 2)),
            pl.BlockSpec((tm, ATTN_WIDTH), lambda i: (i, 0)),
            pl.BlockSpec((None, tm, PLE_DIM), lambda i: (layer, i, 0)),
            _layer_spec((1, SSM_WIDTH), layer),
            _layer_spec((SSM_WIDTH, 2 * SSM_WIDTH), layer),
            _layer_spec((1, 2 * SSM_WIDTH), layer),
            _layer_spec((SSM_WIDTH + ATTN_WIDTH, D_MODEL), layer),
            _layer_spec((1, D_MODEL), layer),
            _layer_spec((D_MODEL, D_MODEL), layer),
            _layer_spec((PLE_DIM, D_MODEL), layer),
        ],
        out_specs=pl.BlockSpec((tm, D_MODEL), lambda i: (i, 0)),
        out_shape=jax.ShapeDtypeStruct((t, D_MODEL), F32),
        scratch_shapes=[pltpu.VMEM((SSM_WIDTH, 2 * SSM_WIDTH), BF16), pltpu.VMEM((SSM_WIDTH, D_MODEL), BF16),
                        pltpu.VMEM((ATTN_WIDTH, D_MODEL), BF16), pltpu.VMEM((D_MODEL, D_MODEL), BF16),
                        pltpu.VMEM((PLE_DIM, D_MODEL), BF16)],
        compiler_params=pltpu.CompilerParams(
            dimension_semantics=("arbitrary",), vmem_limit_bytes=VMEM_LIMIT),
        name="out",
    )(h, y_s, pf, pf, pf, y_att, p_all, d_all, w_glu_all, b_glu_all, w_out_all, ple_g_all, w_pg_all, w_pp_all)


def kernel(x, p, mix_norm_g, w_in, ssm_a_re, ssm_a_im, ssm_log_dt, ssm_b_re, ssm_b_im, ssm_c_re, ssm_c_im,
           ssm_d, ssm_w_glu, ssm_b_glu, q_norm_g, k_norm_g, w_out, ple_norm_g, w_ple_gate, w_ple_proj):
    bsz, seq, _ = x.shape
    depth = w_in.shape[0]
    t = bsz * seq
    n_chunks = seq // SSM_CHUNK
    assert seq % SSM_CHUNK == 0 and t % min(TOKEN_TILE, t) == 0
    n_groups = n_chunks // SCAN_RADIX
    assert n_chunks % SCAN_RADIX == 0 and n_groups & (n_groups - 1) == 0
    n_lvl2 = n_groups.bit_length() - 1
    n_pw = -(-(SCAN_RADIX + n_lvl2) // 8) * 8

    w1, w2, pw = _ssm_setup(ssm_a_re, ssm_a_im, ssm_log_dt, ssm_b_re, ssm_b_im, ssm_c_re, ssm_c_im, n_pw, n_lvl2)
    gq2 = jnp.tile(q_norm_g.reshape(depth, 1, HEAD_DIM), (1, 1, 2))
    gk2 = jnp.tile(k_norm_g.reshape(depth, 1, HEAD_DIM), (1, 1, 2))
    mix_g = mix_norm_g.reshape(depth, 1, D_MODEL)
    ple_g = ple_norm_g.reshape(depth, 1, D_MODEL)
    d_skip = ssm_d.reshape(depth, 1, SSM_WIDTH)
    b_glu = ssm_b_glu.reshape(depth, 1, 2 * SSM_WIDTH)
    p3 = p.reshape(depth, t, PLE_DIM)

    h = x.reshape(t, D_MODEL)
    for i in range(depth):
        pf, qkv = _inproj(h, mix_g, w_in, gq2, gk2, i)
        pf3 = pf.reshape(bsz, seq, F32_COLS)
        y_s = _ssm_main(pf3, w1, w2, pw, n_lvl2, i).reshape(t, SSM_WIDTH)
        y_att = _attention(qkv.reshape(bsz, seq, QKV_COLS)).reshape(t, ATTN_WIDTH)
        h = _out(h, y_s, pf, y_att, p3, d_skip, ssm_w_glu, b_glu, w_out, ple_g, w_ple_gate, w_ple_proj, i)
    return h.reshape(bsz, seq, D_MODEL)
```

```python
import functools
import math

import jax
import jax.numpy as jnp
from jax import lax
from jax.experimental import pallas as pl
from jax.experimental.pallas import tpu as pltpu

F32 = jnp.float32
BF16 = jnp.bfloat16

D_MODEL = 1024
PLE_DIM = 256
SSM_WIDTH = 512
SSM_GROUP = 16
SSM_STATE = 64
LANES = 128
MXU_TILE = 256
SLAB_GROUPS = LANES // SSM_GROUP
SSM_SLABS = SSM_WIDTH // LANES
SLAB_STATES = SLAB_GROUPS * SSM_STATE
SSM_CHUNK = 8
CHUNK_COLS = SSM_CHUNK * LANES
SSM_SEQS_PER_STEP = 1
ROW_GRANULE = 16
SCAN_RADIX = 8
ATTN_WIDTH = 512
HEAD_DIM = 64
HEAD_PAIRS = ATTN_WIDTH // LANES
PAIRS_PER_STEP = 4
IN_COLS = 3072
F32_COLS = 2 * SSM_WIDTH + ATTN_WIDTH
QKV_COLS = 3 * ATTN_WIDTH
RMS_EPS = 1e-6
LOG2E = 1.4426950408889634
ATTN_QUERY_BLOCK = 128
ATTN_SUBBLOCKS = 4
ATTN_TOP_ROWS = 32
ATTN_KEY_WINDOW = 256
STAY_BITS_DONE = 152.0
TOKEN_TILE = 512
OUT_SUBTILES = 2
VMEM_LIMIT = 56 * 1024 * 1024


def _sigmoid(x):
    return 1.0 / (1.0 + jnp.exp(-x))


def _dot(a, b):
    return jnp.dot(a, b, preferred_element_type=F32)


def _dot_exact(a, b):
    return jnp.dot(a, b, preferred_element_type=F32, precision=lax.Precision.HIGHEST)


def _head_mask():
    return lax.broadcasted_iota(jnp.int32, (1, LANES), 1) < HEAD_DIM


def _headnorm(x, g, is_a):
    x2 = x * x
    s_a = jnp.sum(jnp.where(is_a, x2, 0.0), axis=-1, keepdims=True)
    s_b = jnp.sum(jnp.where(is_a, 0.0, x2), axis=-1, keepdims=True)
    ms = jnp.where(is_a, s_a, s_b) * (1.0 / HEAD_DIM)
    return (x * lax.rsqrt(ms + RMS_EPS)) * g


def _inproj_kernel(x_ref, g_ref, w_ref, gq_ref, gk_ref, f_ref, qkv_ref, w_scr):
    @pl.when(pl.program_id(0) == 0)
    def _():
        a0 = 2 * SSM_WIDTH
        w_scr[:, 0:a0] = w_ref[:, 0:a0].astype(BF16)
        w_scr[:, a0:F32_COLS] = w_ref[:, a0 + QKV_COLS:IN_COLS].astype(BF16)
        w_scr[:, F32_COLS:IN_COLS] = w_ref[:, a0:a0 + QKV_COLS].astype(BF16)

    x = x_ref[...]
    ms = jnp.mean(x * x, axis=-1, keepdims=True)
    hn = (x * lax.rsqrt(ms + RMS_EPS)) * g_ref[...]
    res = _dot(hn.astype(BF16), w_scr[...])
    f_ref[...] = res[:, 0:F32_COLS]
    is_a = _head_mask()
    q_scale = (HEAD_DIM ** -0.5) * LOG2E
    for s in range(HEAD_PAIRS):
        c0 = F32_COLS + s * LANES
        q = _headnorm(res[:, c0:c0 + LANES], gq_ref[...], is_a) * q_scale
        qkv_ref[:, s * LANES:(s + 1) * LANES] = q.astype(BF16)
        c1 = c0 + ATTN_WIDTH
        k = _headnorm(res[:, c1:c1 + LANES], gk_ref[...], is_a)
        qkv_ref[:, ATTN_WIDTH + s * LANES:ATTN_WIDTH + (s + 1) * LANES] = k.astype(BF16)
    qkv_ref[:, 2 * ATTN_WIDTH:3 * ATTN_WIDTH] = res[:, F32_COLS + 2 * ATTN_WIDTH:IN_COLS].astype(BF16)


def _layer_spec(shape, layer):
    return pl.BlockSpec((None,) + shape, lambda i: (layer,) + (0,) * len(shape), pipeline_mode=pl.Buffered(1))


def _inproj(h, g_all, w_all, gq_all, gk_all, layer):
    t = h.shape[0]
    tm = min(TOKEN_TILE, t)
    return pl.pallas_call(
        _inproj_kernel,
        grid=(t // tm,),
        in_specs=[
            pl.BlockSpec((tm, D_MODEL), lambda i: (i, 0)),
            _layer_spec((1, D_MODEL), layer),
            _layer_spec((D_MODEL, IN_COLS), layer),
            _layer_spec((1, LANES), layer),
            _layer_spec((1, LANES), layer),
        ],
        out_specs=[pl.BlockSpec((tm, F32_COLS), lambda i: (i, 0)),
                   pl.BlockSpec((tm, QKV_COLS), lambda i: (i, 0))],
        out_shape=[jax.ShapeDtypeStruct((t, F32_COLS), F32), jax.ShapeDtypeStruct((t, QKV_COLS), BF16)],
        scratch_shapes=[pltpu.VMEM((D_MODEL, IN_COLS), BF16)],
        compiler_params=pltpu.CompilerParams(
            dimension_semantics=("arbitrary",), vmem_limit_bytes=VMEM_LIMIT),
        name="inproj",
    )(h, g_all, w_all, gq_all, gk_all)


def _ssm_setup_kernel(are_r, aim_r, ldt_r, bre_ref, bim_ref, cre_ref, cim_ref,
                      w1_ref, w2_ref, pw_ref, *, n_pw, n_lvl2):
    ns = SLAB_STATES
    a_re = are_r[...]
    a_im = aim_r[...]
    dt = jnp.exp(ldt_r[...])
    lam_r = a_re * dt
    lam_i = a_im * dt
    mag = jnp.exp(lam_r)
    ab_re = mag * jnp.cos(lam_i)
    ab_im = mag * jnp.sin(lam_i)
    num_re = ab_re - 1.0
    num_im = ab_im
    den = a_re * a_re + a_im * a_im
    f_re = (num_re * a_re + num_im * a_im) / den
    f_im = (num_im * a_re - num_re * a_im) / den
    b_re = bre_ref[...]
    b_im = bim_ref[...]
    bb_re = jnp.tile(f_re * b_re - f_im * b_im, (SLAB_GROUPS, 1))
    bb_im = jnp.tile(f_re * b_im + f_im * b_re, (SLAB_GROUPS, 1))
    grp_row = lax.broadcasted_iota(jnp.int32, (LANES, ns), 0) >> 4
    grp_col = lax.broadcasted_iota(jnp.int32, (LANES, ns), 1) >> 6
    same_in = grp_row == grp_col
    zin = jnp.zeros((LANES, ns), F32)
    bb_re = jnp.where(same_in, bb_re, zin)
    bb_im = jnp.where(same_in, bb_im, zin)

    e_in = (SSM_CHUNK - 1 - lax.broadcasted_iota(jnp.int32, (SSM_CHUNK, 1), 0)).astype(F32)
    mag_in = jnp.exp(lam_r * e_in)
    pin_re = mag_in * jnp.cos(lam_i * e_in)
    pin_im = mag_in * jnp.sin(lam_i * e_in)
    for lp in range(SSM_CHUNK):
        p_re = pin_re[lp:lp + 1, :]
        p_im = pin_im[lp:lp + 1, :]
        w1_ref[lp * LANES:(lp + 1) * LANES, 0:ns] = (p_re * bb_re - p_im * bb_im).astype(BF16)
        w1_ref[lp * LANES:(lp + 1) * LANES, ns:2 * ns] = (p_re * bb_im + p_im * bb_re).astype(BF16)

    grp_row_o = lax.broadcasted_iota(jnp.int32, (ns, LANES), 0) >> 6
    grp_col_o = lax.broadcasted_iota(jnp.int32, (ns, LANES), 1) >> 4
    same_out = grp_row_o == grp_col_o
    zout = jnp.zeros((ns, LANES), F32)
    c_re = jnp.where(same_out, cre_ref[...], zout)
    c_im = jnp.where(same_out, cim_ref[...], zout)

    lag = lax.broadcasted_iota(jnp.int32, (2 * SSM_CHUNK, 1), 0).astype(F32)
    mag_c = jnp.exp(lam_r * lag)
    fill = jnp.zeros((LANES - 2 * SSM_CHUNK, ns), F32)
    pc_re = jnp.concatenate([mag_c * jnp.cos(lam_i * lag), fill], axis=0).T
    pc_im = jnp.concatenate([mag_c * jnp.sin(lam_i * lag), fill], axis=0).T
    c_pow = []
    for l in range(SSM_CHUNK + 1):
        q_re = jnp.broadcast_to(pc_re[:, l:l + 1], (ns, LANES))
        q_im = jnp.broadcast_to(pc_im[:, l:l + 1], (ns, LANES))
        c_pow.append((c_re * q_re - c_im * q_im, -c_re * q_im - c_im * q_re))
    kf = []
    for l in range(SSM_CHUNK):
        w2_ref[0:ns, l * LANES:(l + 1) * LANES] = c_pow[l + 1][0].astype(BF16)
        w2_ref[ns:2 * ns, l * LANES:(l + 1) * LANES] = c_pow[l + 1][1].astype(BF16)
        kf.append((_dot_exact(bb_re, c_pow[l][0]) + _dot_exact(bb_im, c_pow[l][1])).astype(BF16))
    zblk = jnp.zeros((LANES, LANES), BF16)
    for lp in range(SSM_CHUNK):
        for l in range(SSM_CHUNK):
            w1_ref[lp * LANES:(lp + 1) * LANES, 2 * ns + l * LANES:2 * ns + (l + 1) * LANES] = (
                kf[l - lp] if l >= lp else zblk)

    prow = lax.broadcasted_iota(jnp.int32, (n_pw, 1), 0)
    dbl = jnp.clip(prow - SCAN_RADIX, 0, max(n_lvl2 - 1, 0))
    e_pw = jnp.where(prow < SCAN_RADIX, SSM_CHUNK * prow,
                     jnp.left_shift(SSM_CHUNK * SCAN_RADIX, dbl)).astype(F32)
    mag_pw = jnp.exp(lam_r * e_pw)
    pw_ref[:, 0:ns] = mag_pw * jnp.cos(lam_i * e_pw)
    pw_ref[:, ns:2 * ns] = mag_pw * jnp.sin(lam_i * e_pw)


def _ssm_setup(a_re, a_im, log_dt, b_re, b_im, c_re, c_im, n_pw, n_lvl2):
    depth = a_re.shape[0]
    g, p, h = a_re.shape[1], SSM_STATE, SSM_GROUP
    ns = SLAB_STATES
    ldt = jnp.broadcast_to(log_dt[:, :, None], (depth, g, p))

    def row(v):
        return v.reshape(depth, SSM_SLABS, 1, ns)

    def b_lay(v):
        return v.reshape(depth, SSM_SLABS, SLAB_GROUPS, p, h).transpose(0, 1, 4, 2, 3).reshape(
            depth, SSM_SLABS, h, ns)

    def c_lay(v):
        vt = v.reshape(depth, SSM_SLABS, SLAB_GROUPS, h, p).transpose(0, 1, 2, 4, 3).reshape(
            depth, SSM_SLABS, ns, h)
        return jnp.tile(vt, (1, 1, 1, SLAB_GROUPS))

    def spec(shape):
        return pl.BlockSpec((None, None) + shape, lambda d, q: (d, q) + (0,) * len(shape))

    return pl.pallas_call(
        functools.partial(_ssm_setup_kernel, n_pw=n_pw, n_lvl2=n_lvl2),
        grid=(depth, SSM_SLABS),
        in_specs=[spec((1, ns))] * 3 + [spec((h, ns))] * 2 + [spec((ns, LANES))] * 2,
        out_specs=[spec((CHUNK_COLS, 2 * ns + CHUNK_COLS)), spec((2 * ns, CHUNK_COLS)), spec((n_pw, 2 * ns))],
        out_shape=[
            jax.ShapeDtypeStruct((depth, SSM_SLABS, CHUNK_COLS, 2 * ns + CHUNK_COLS), BF16),
            jax.ShapeDtypeStruct((depth, SSM_SLABS, 2 * ns, CHUNK_COLS), BF16),
            jax.ShapeDtypeStruct((depth, SSM_SLABS, n_pw, 2 * ns), F32),
        ],
        compiler_params=pltpu.CompilerParams(
            dimension_semantics=("arbitrary", "arbitrary"), vmem_limit_bytes=VMEM_LIMIT),
        name="ssm_setup",
    )(row(a_re), row(a_im), row(ldt), b_lay(b_re), b_lay(b_im), c_lay(c_re), c_lay(c_im))


def _ssm_main_kernel(u_all, w1_ref, w2_ref, pw_ref, y_all, *scratch, n_chunks, n_lvl2, pad, nb):
    per = len(scratch) // nb
    y_intras = [None] * nb
    for phase in (_ssm_project, _ssm_scan, _ssm_emit):
        for bb in range(nb):
            phase(u_all.at[bb], w1_ref, w2_ref, pw_ref, y_all.at[bb], y_intras, bb,
                  *scratch[bb * per:(bb + 1) * per], n_chunks=n_chunks, n_lvl2=n_lvl2, pad=pad)


def _ssm_project(u_ref, w1_ref, w2_ref, pw_ref, y_ref, y_intras, bb, lhs_ref, z_scr, sp_scr, g_a, g_b,
                 *, n_chunks, n_lvl2, pad):
    ns = SLAB_STATES
    nsl = ns // LANES
    for l in range(SSM_CHUNK):
        lhs_ref[:, l * LANES:(l + 1) * LANES] = u_ref[pl.ds(l, n_chunks, stride=SSM_CHUNK), :].astype(BF16)
    zin = _dot(lhs_ref[...], w1_ref[:, 0:2 * ns])
    for j in range(2 * nsl):
        z_scr[j] = zin[:, j * LANES:(j + 1) * LANES]
    y_intras[bb] = [_dot(lhs_ref[:, 0:(c + 1) * MXU_TILE],
                         w1_ref[0:(c + 1) * MXU_TILE, 2 * ns + c * MXU_TILE:2 * ns + (c + 1) * MXU_TILE])
                    for c in range(CHUNK_COLS // MXU_TILE)]


def _ssm_scan(u_ref, w1_ref, w2_ref, pw_ref, y_ref, y_intras, bb, lhs_ref, z_scr, sp_scr, g_a, g_b,
              *, n_chunks, n_lvl2, pad):
    ns = SLAB_STATES
    nsl = ns // LANES
    ng = n_chunks // SCAN_RADIX
    for js in range(nsl):
        jr, ji = js, nsl + js
        lr = slice(js * LANES, (js + 1) * LANES)
        li = slice(ns + js * LANES, ns + (js + 1) * LANES)
        a_r = pw_ref[1:2, lr]
        a_i = pw_ref[1:2, li]
        xs = []
        for lo in range(SCAN_RADIX):
            z_r = z_scr[jr, pl.ds(lo, ng, stride=SCAN_RADIX), :]
            z_i = z_scr[ji, pl.ds(lo, ng, stride=SCAN_RADIX), :]
            if lo == 0:
                x_r, x_i = z_r, z_i
            else:
                x_r, x_i = a_r * x_r - a_i * x_i + z_r, a_r * x_i + a_i * x_r + z_i
            xs.append((x_r, x_i))
        ga, gb = g_a.at[js], g_b.at[js]
        zeros = jnp.zeros((pad, 2 * LANES), F32)
        ga[0:pad, :] = zeros
        gb[0:pad, :] = zeros
        ga[pad:pad + ng, 0:LANES] = x_r
        ga[pad:pad + ng, LANES:2 * LANES] = x_i
        src, dst = ga, gb
        for k in range(n_lvl2):
            d = 1 << k
            t_r = src[pad:pad + ng, 0:LANES]
            t_i = src[pad:pad + ng, LANES:2 * LANES]
            s_r = src[pad - d:pad - d + ng, 0:LANES]
            s_i = src[pad - d:pad - d + ng, LANES:2 * LANES]
            p_r = pw_ref[SCAN_RADIX + k:SCAN_RADIX + k + 1, lr]
            p_i = pw_ref[SCAN_RADIX + k:SCAN_RADIX + k + 1, li]
            dst[pad:pad + ng, 0:LANES] = t_r + p_r * s_r - p_i * s_i
            dst[pad:pad + ng, LANES:2 * LANES] = t_i + p_r * s_i + p_i * s_r
            src, dst = dst, src
        sg_r = src[pad - 1:pad - 1 + ng, 0:LANES]
        sg_i = src[pad - 1:pad - 1 + ng, LANES:2 * LANES]
        for lo in range(SCAN_RADIX):
            if lo == 0:
                s_r, s_i = sg_r, sg_i
            else:
                p_r = pw_ref[lo:lo + 1, lr]
                p_i = pw_ref[lo:lo + 1, li]
                s_r = p_r * sg_r - p_i * sg_i + xs[lo - 1][0]
                s_i = p_r * sg_i + p_i * sg_r + xs[lo - 1][1]
            sp_scr[jr, pl.ds(lo, ng, stride=SCAN_RADIX), :] = s_r
            sp_scr[ji, pl.ds(lo, ng, stride=SCAN_RADIX), :] = s_i


def _ssm_emit(u_ref, w1_ref, w2_ref, pw_ref, y_ref, y_intras, bb, lhs_ref, z_scr, sp_scr, g_a, g_b,
              *, n_chunks, n_lvl2, pad):
    nsl = SLAB_STATES // LANES
    s_prev = jnp.concatenate([sp_scr[j] for j in range(2 * nsl)], axis=1).astype(BF16)
    y = jnp.concatenate(y_intras[bb], axis=1) + _dot(s_prev, w2_ref[...])
    for l in range(SSM_CHUNK):
        y_ref[pl.ds(l, n_chunks, stride=SSM_CHUNK), :] = y[:, l * LANES:(l + 1) * LANES]


def _ssm_main(pf3, w1, w2, pw, n_lvl2, layer):
    bsz, seq, _ = pf3.shape
    n_chunks = seq // SSM_CHUNK
    ns = SLAB_STATES
    nsl = ns // LANES
    ng = n_chunks // SCAN_RADIX
    pad = max(8, ng // 2)
    n_pw = pw.shape[2]
    nb = SSM_SEQS_PER_STEP if bsz % SSM_SEQS_PER_STEP == 0 else 1
    kern = functools.partial(_ssm_main_kernel, n_chunks=n_chunks, n_lvl2=n_lvl2, pad=pad, nb=nb)
    return pl.pallas_call(
        kern,
        grid=(SSM_SLABS, bsz // nb),
        in_specs=[
            pl.BlockSpec((nb, seq, LANES), lambda s, b: (b, 0, s)),
            pl.BlockSpec((None, None, CHUNK_COLS, 2 * ns + CHUNK_COLS), lambda s, b: (layer, s, 0, 0)),
            pl.BlockSpec((None, None, 2 * ns, CHUNK_COLS), lambda s, b: (layer, s, 0, 0)),
            pl.BlockSpec((None, None, n_pw, 2 * ns), lambda s, b: (layer, s, 0, 0)),
        ],
        out_specs=pl.BlockSpec((nb, seq, LANES), lambda s, b: (b, 0, s)),
        out_shape=jax.ShapeDtypeStruct((bsz, seq, SSM_WIDTH), F32),
        scratch_shapes=[pltpu.VMEM((n_chunks, CHUNK_COLS), BF16),
                        pltpu.VMEM((2 * nsl, n_chunks, LANES), F32),
                        pltpu.VMEM((2 * nsl, n_chunks, LANES), F32),
                        pltpu.VMEM((nsl, pad + ng, 2 * LANES), F32),
                        pltpu.VMEM((nsl, pad + ng, 2 * LANES), F32)] * nb,
        compiler_params=pltpu.CompilerParams(
            dimension_semantics=("arbitrary", "arbitrary"), vmem_limit_bytes=VMEM_LIMIT),
        name="ssm_main",
    )(pf3, w1, w2, pw)


def _attn_kernel(q_ref, k_ref, v_ref, tri1_ref, tri2_ref, o_ref, acc_ref, stay_ref, kv_scr,
                 *, qb, kw, nq, top, seq):
    i = pl.program_id(2)
    q0 = pl.multiple_of(i * (nq * qb), nq * qb)
    n_heads = 2 * PAIRS_PER_STEP

    @pl.when(i == 0)
    def _():
        is_a = jnp.concatenate([_head_mask()] * PAIRS_PER_STEP, axis=1)
        chunk = min(seq, 512)

        def fill(c, carry):
            r = pl.multiple_of(c * chunk, chunk)
            for j, ref in enumerate((k_ref, v_ref)):
                blk = ref[pl.ds(r, chunk), :]
                zero = jnp.zeros_like(blk)
                kv_scr[2 * j, pl.ds(r, chunk), :] = jnp.where(is_a, blk, zero)
                kv_scr[2 * j + 1, pl.ds(r, chunk), :] = jnp.where(is_a, zero, blk)
            return carry
        lax.fori_loop(0, seq // chunk, fill, 0)

    d_row = lax.broadcasted_iota(jnp.int32, (qb, 2 * qb), 0)
    d_key = lax.broadcasted_iota(jnp.int32, (qb, 2 * qb), 1) & (qb - 1)
    causal = d_key < d_row
    w_col = lax.broadcasted_iota(jnp.int32, (qb, kw), 1)
    tris = {qb: tri1_ref[...], kw: tri2_ref[...]}

    def rows_into(full, part, r0, nr):
        def rest(a, b):
            return jnp.zeros((b - a, part.shape[1]), part.dtype) if full is None else full[a:b]
        pieces = ([rest(0, r0)] if r0 > 0 else []) + [part] + ([rest(r0 + nr, qb)] if r0 + nr < qb else [])
        return pieces[0] if len(pieces) == 1 else jnp.concatenate(pieces, axis=0)

    def sweep(parts, stay):
        stay = [list(row) for row in stay]
        unswept = [[set(range(0, qb, ROW_GRANULE)) if st is None else set() for st in row] for row in stay]
        units = [(pi, p) for pi in range(len(parts)) for p in range(PAIRS_PER_STEP)]
        zs, sps, lefts, laters, ws = {}, {}, {}, {}, {}

        def stacked(j, k0, width, p):
            cols = slice(p * LANES, (p + 1) * LANES)
            return jnp.concatenate([kv_scr[2 * j, pl.ds(k0, width), cols],
                                    kv_scr[2 * j + 1, pl.ds(k0, width), cols]], axis=0)

        def scores(u):
            pi, p = u
            s, r0, nr, k0, width, _ = parts[pi]
            qp = q_ref[s * qb + r0:s * qb + r0 + nr, p * LANES:(p + 1) * LANES]
            zs[u] = lax.dot_general(qp, stacked(0, k0, width, p), (((1,), (1,)), ((), ())),
                                    preferred_element_type=F32)

        def stick(u):
            pi, p = u
            s, r0, nr, _, width, mask = parts[pi]
            z = zs.pop(u)
            sp = jnp.maximum(z, 0.0) + jnp.log(1.0 + jnp.exp2(-jnp.abs(z))) * LOG2E
            if mask is not None:
                sp = jnp.where(mask, sp, 0.0)
            base = z - sp
            touched = set(range(r0, r0 + nr, ROW_GRANULE))
            halves = []
            for hd in range(2):
                n = 2 * p + hd
                cols = slice(hd * width, (hd + 1) * width)
                tot = jnp.sum(sp[:, cols], axis=-1, keepdims=True)
                if touched <= unswept[s][n]:
                    halves.append(None)
                    rows = tot
                else:
                    seen = stay[s][n][r0:r0 + nr]
                    halves.append(base[:, cols] - seen)
                    rows = seen + tot
                unswept[s][n] = unswept[s][n] - touched
                stay[s][n] = rows if nr == qb else rows_into(stay[s][n], rows, r0, nr)
            sps[u] = sp.astype(BF16)
            lefts[u] = base if halves[0] is None else jnp.concatenate(halves, axis=1)

        def later(u):
            laters[u] = _dot(sps.pop(u), tris[parts[u[0]][4]])

        def weight(u):
            mask = parts[u[0]][5]
            w = jnp.exp2(lefts.pop(u) - laters.pop(u))
            if mask is not None:
                w = jnp.where(mask, w, 0.0)
            ws[u] = w.astype(BF16)

        for stage in (scores, stick, later, weight):
            for u in units:
                stage(u)
        pvs = [[None] * PAIRS_PER_STEP for _ in range(nq)]
        for pi, (s, r0, nr, k0, width, _) in enumerate(parts):
            for p in range(PAIRS_PER_STEP):
                d = rows_into(None, _dot(ws.pop((pi, p)), stacked(1, k0, width, p)), r0, nr)
                pvs[s][p] = d if pvs[s][p] is None else pvs[s][p] + d
        return pvs, stay

    def commit(pvs, stay, first):
        low = None
        for s in range(nq):
            for p in range(PAIRS_PER_STEP):
                acc_ref[s, p] = pvs[s][p] if first else acc_ref[s, p] + pvs[s][p]
            for n in range(n_heads):
                stay_ref[s, n] = jnp.broadcast_to(stay[s][n], (qb, LANES))
                low = stay[s][n] if low is None else jnp.minimum(low, stay[s][n])
        return jnp.min(low)

    fresh = [[None] * n_heads for _ in range(nq)]
    diags = [(s, 0, qb, pl.multiple_of(q0 + s * qb, qb), qb, causal) for s in range(nq)]

    def stay_now():
        return [[stay_ref[s, n][:, 0:1] for n in range(n_heads)] for s in range(nq)]

    def key_block(s, back):
        return pl.multiple_of(q0 + (s - back) * qb, qb)

    def main_step():
        near = [(s, 0, qb, key_block(s, 1), qb, None) for s in range(nq)]
        far_top = [(s, 0, top, key_block(s, 2), qb, None) for s in range(nq)]
        return commit(*sweep(diags + near + far_top, fresh), True)

    def far_rest():
        parts = [(s, top, qb - top, key_block(s, 2), qb, None) for s in range(nq)]
        return commit(*sweep(parts, stay_now()), False)

    def first_step():
        prevs = [(s, 0, qb, j * qb, qb, None) for s in range(nq) for j in reversed(range(s))]
        return commit(*sweep(diags + prevs, fresh), True)

    has_window = q0 >= kw
    low1 = lax.cond(has_window, main_step, first_step)
    low2 = lax.cond(jnp.logical_and(has_window, low1 < STAY_BITS_DONE), far_rest, lambda: low1)
    start1 = jnp.where(has_window, q0 + (nq - 1) * qb - kw, 0)

    def cond(c):
        start, low = c
        return jnp.logical_and(start > 0, low < STAY_BITS_DONE)

    def body(c):
        start, _ = c
        parts = []
        for s in range(nq):
            start_s = jnp.maximum(start - (nq - 1 - s) * qb, 0)
            k0 = pl.multiple_of(jnp.maximum(start_s - kw, 0), qb)
            valid = (k0 + w_col) < start_s
            parts.append((s, 0, qb, k0, kw, jnp.concatenate([valid, valid], axis=1)))
        return jnp.maximum(start - kw, 0), commit(*sweep(parts, stay_now()), False)

    lax.while_loop(cond, body, (start1, low2))
    for s in range(nq):
        for p in range(PAIRS_PER_STEP):
            o_ref[s * qb:(s + 1) * qb, p * LANES:(p + 1) * LANES] = acc_ref[s, p]


def _attention(qkv3):
    bsz, seq, _ = qkv3.shape
    qb = min(ATTN_QUERY_BLOCK, seq)
    kw = min(ATTN_KEY_WINDOW, seq)
    nq = ATTN_SUBBLOCKS
    blk = nq * qb
    assert seq % blk == 0 and kw == 2 * qb and blk >= kw
    w = PAIRS_PER_STEP * LANES
    n_grp = HEAD_PAIRS // PAIRS_PER_STEP
    kern = functools.partial(_attn_kernel, qb=qb, kw=kw, nq=nq, top=ATTN_TOP_ROWS, seq=seq)

    def pair_tri(n):
        return jnp.kron(jnp.eye(2, dtype=F32), jnp.tri(n, n, -1, dtype=F32)).astype(BF16)
    return pl.pallas_call(
        kern,
        grid=(bsz, n_grp, seq // blk),
        in_specs=[
            pl.BlockSpec((None, blk, w), lambda b, g, i: (b, i, g)),
            pl.BlockSpec((None, seq, w), lambda b, g, i: (b, 0, n_grp + g)),
            pl.BlockSpec((None, seq, w), lambda b, g, i: (b, 0, 2 * n_grp + g)),
            pl.BlockSpec((2 * qb, 2 * qb), lambda b, g, i: (0, 0)),
            pl.BlockSpec((2 * kw, 2 * kw), lambda b, g, i: (0, 0)),
        ],
        out_specs=pl.BlockSpec((None, blk, w), lambda b, g, i: (b, i, g)),
        out_shape=jax.ShapeDtypeStruct((bsz, seq, ATTN_WIDTH), F32),
        scratch_shapes=[pltpu.VMEM((nq, PAIRS_PER_STEP, qb, LANES), F32),
                        pltpu.VMEM((nq, 2 * PAIRS_PER_STEP, qb, LANES), F32),
                        pltpu.VMEM((4, seq, w), BF16)],
        compiler_params=pltpu.CompilerParams(
            dimension_semantics=("arbitrary", "arbitrary", "arbitrary"), vmem_limit_bytes=VMEM_LIMIT),
        name="attention",
    )(qkv3, qkv3, qkv3, pair_tri(qb), pair_tri(kw))


def _out_kernel(h_ref, ys_ref, u_ref, gs_ref, ga_ref, ya_ref, p_ref, d_ref, wgf_ref, bg_ref, wof_ref,
                pg_ref, wpgf_ref, wppf_ref, o_ref, wg_ref, wo1_ref, wo2_ref, wpg_ref, wpp_ref):
    @pl.when(pl.program_id(0) == 0)
    def _():
        wg_ref[...] = wgf_ref[...].astype(BF16)
        wo1_ref[...] = wof_ref[0:SSM_WIDTH, :].astype(BF16)
        wo2_ref[...] = wof_ref[SSM_WIDTH:SSM_WIDTH + ATTN_WIDTH, :].astype(BF16)
        wpg_ref[...] = wpgf_ref[...].astype(BF16)
        wpp_ref[...] = wppf_ref[...].astype(BF16)

    rows = h_ref.shape[0] // OUT_SUBTILES
    subs = [slice(i * rows, (i + 1) * rows) for i in range(OUT_SUBTILES)]
    zs = []
    for sl in subs:
        y = ys_ref[sl, :] + d_ref[...] * u_ref[sl, :]
        z = 0.5 * y * (1.0 + jnp.tanh(math.sqrt(2.0 / math.pi) * (y + 0.044715 * (y * y * y))))
        zs.append(z.astype(BF16))
    zzs = [_dot(z, wg_ref[...]) + bg_ref[...] for z in zs]
    y_ssms = []
    for sl, zz in zip(subs, zzs):
        gs = gs_ref[sl, :]
        y_ssm = (zz[:, 0:SSM_WIDTH] * _sigmoid(zz[:, SSM_WIDTH:2 * SSM_WIDTH])) * (gs * _sigmoid(gs))
        y_ssms.append(y_ssm.astype(BF16))
    y_atts = []
    for sl in subs:
        ga = ga_ref[sl, :]
        y_atts.append((ya_ref[sl, :] * (ga * _sigmoid(ga))).astype(BF16))
    hs = [h_ref[sl, :] + _dot(y_ssm, wo1_ref[...]) + _dot(y_att, wo2_ref[...])
          for sl, y_ssm, y_att in zip(subs, y_ssms, y_atts)]
    hns = []
    for h in hs:
        ms = jnp.mean(h * h, axis=-1, keepdims=True)
        hns.append(((h * lax.rsqrt(ms + RMS_EPS)) * pg_ref[...]).astype(BF16))
    gates = [_sigmoid(_dot(hn, wpg_ref[...])) for hn in hns]
    for sl, h, gate in zip(subs, hs, gates):
        o_ref[sl, :] = h + gate * _dot(p_ref[sl, :].astype(BF16), wpp_ref[...])


def _out(h, y_s, pf, y_att, p_all, d_all, w_glu_all, b_glu_all, w_out_all, ple_g_all, w_pg_all, w_pp_all, layer):
    t = h.shape[0]
    tm = min(TOKEN_TILE, t)
    return pl.pallas_call(
        _out_kernel,
        grid=(t // tm,),
        in_specs=[
            pl.BlockSpec((tm, D_MODEL), lambda i: (i, 0)),
            pl.BlockSpec((tm, SSM_WIDTH), lambda i: (i, 0)),
            pl.BlockSpec((tm, SSM_WIDTH), lambda i: (i, 0)),
            pl.BlockSpec((tm, SSM_WIDTH), lambda i: (i, 1)),
            pl.BlockSpec((tm, ATTN_WIDTH), lambda i: (i, 2)),
            pl.BlockSpec((tm, ATTN_WIDTH), lambda i: (i, 0)),
            pl.BlockSpec((None, tm, PLE_DIM), lambda i: (layer, i, 0)),
            _layer_spec((1, SSM_WIDTH), layer),
            _layer_spec((SSM_WIDTH, 2 * SSM_WIDTH), layer),
            _layer_spec((1, 2 * SSM_WIDTH), layer),
            _layer_spec((SSM_WIDTH + ATTN_WIDTH, D_MODEL), layer),
            _layer_spec((1, D_MODEL), layer),
            _layer_spec((D_MODEL, D_MODEL), layer),
            _layer_spec((PLE_DIM, D_MODEL), layer),
        ],
        out_specs=pl.BlockSpec((tm, D_MODEL), lambda i: (i, 0)),
        out_shape=jax.ShapeDtypeStruct((t, D_MODEL), F32),
        scratch_shapes=[pltpu.VMEM((SSM_WIDTH, 2 * SSM_WIDTH), BF16), pltpu.VMEM((SSM_WIDTH, D_MODEL), BF16),
                        pltpu.VMEM((ATTN_WIDTH, D_MODEL), BF16), pltpu.VMEM((D_MODEL, D_MODEL), BF16),
                        pltpu.VMEM((PLE_DIM, D_MODEL), BF16)],
        compiler_params=pltpu.CompilerParams(
            dimension_semantics=("arbitrary",), vmem_limit_bytes=VMEM_LIMIT),
        name="out",
    )(h, y_s, pf, pf, pf, y_att, p_all, d_all, w_glu_all, b_glu_all, w_out_all, ple_g_all, w_pg_all, w_pp_all)


def kernel(x, p, mix_norm_g, w_in, ssm_a_re, ssm_a_im, ssm_log_dt, ssm_b_re, ssm_b_im, ssm_c_re, ssm_c_im,
           ssm_d, ssm_w_glu, ssm_b_glu, q_norm_g, k_norm_g, w_out, ple_norm_g, w_ple_gate, w_ple_proj):
    bsz, seq, _ = x.shape
    depth = w_in.shape[0]
    t = bsz * seq
    n_chunks = seq // SSM_CHUNK
    assert seq % SSM_CHUNK == 0 and t % min(TOKEN_TILE, t) == 0
    n_groups = n_chunks // SCAN_RADIX
    assert n_chunks % SCAN_RADIX == 0 and n_groups & (n_groups - 1) == 0
    n_lvl2 = n_groups.bit_length() - 1
    n_pw = -(-(SCAN_RADIX + n_lvl2) // 8) * 8

    w1, w2, pw = _ssm_setup(ssm_a_re, ssm_a_im, ssm_log_dt, ssm_b_re, ssm_b_im, ssm_c_re, ssm_c_im, n_pw, n_lvl2)
    gq2 = jnp.tile(q_norm_g.reshape(depth, 1, HEAD_DIM), (1, 1, 2))
    gk2 = jnp.tile(k_norm_g.reshape(depth, 1, HEAD_DIM), (1, 1, 2))
    mix_g = mix_norm_g.reshape(depth, 1, D_MODEL)
    ple_g = ple_norm_g.reshape(depth, 1, D_MODEL)
    d_skip = ssm_d.reshape(depth, 1, SSM_WIDTH)
    b_glu = ssm_b_glu.reshape(depth, 1, 2 * SSM_WIDTH)
    p3 = p.reshape(depth, t, PLE_DIM)

    h = x.reshape(t, D_MODEL)
    for i in range(depth):
        pf, qkv = _inproj(h, mix_g, w_in, gq2, gk2, i)
        pf3 = pf.reshape(bsz, seq, F32_COLS)
        y_s = _ssm_main(pf3, w1, w2, pw, n_lvl2, i).reshape(t, SSM_WIDTH)
        y_att = _attention(qkv.reshape(bsz, seq, QKV_COLS)).reshape(t, ATTN_WIDTH)
        h = _out(h, y_s, pf, y_att, p3, d_skip, ssm_w_glu, b_glu, w_out, ple_g, w_ple_gate, w_ple_proj, i)
    return h.reshape(bsz, seq, D_MODEL)
```

```python
import functools
import math

import jax
import jax.numpy as jnp
from jax import lax
from jax.experimental import pallas as pl
from jax.experimental.pallas import tpu as pltpu

F32 = jnp.float32
BF16 = jnp.bfloat16

D_MODEL = 1024
PLE_DIM = 256
SSM_WIDTH = 512
SSM_GROUP = 16
SSM_STATE = 64
LANES = 128
MXU_TILE = 256
SLAB_GROUPS = LANES // SSM_GROUP
SSM_SLABS = SSM_WIDTH // LANES
SLAB_STATES = SLAB_GROUPS * SSM_STATE
SSM_CHUNK = 8
CHUNK_COLS = SSM_CHUNK * LANES
SSM_SEQS_PER_STEP = 1
ROW_GRANULE = 16
SCAN_RADIX = 8
ATTN_WIDTH = 512
HEAD_DIM = 64
HEAD_PAIRS = ATTN_WIDTH // LANES
PAIRS_PER_STEP = 4
IN_COLS = 3072
F32_COLS = 2 * SSM_WIDTH + ATTN_WIDTH
QKV_COLS = 3 * ATTN_WIDTH
BF_COLS = QKV_COLS + SSM_WIDTH + ATTN_WIDTH
RMS_EPS = 1e-6
LOG2E = 1.4426950408889634
ATTN_QUERY_BLOCK = 128
ATTN_SUBBLOCKS = 4
ATTN_TOP_ROWS = 32
ATTN_KEY_WINDOW = 256
STAY_BITS_DONE = 152.0
TOKEN_TILE = 512
OUT_SUBTILES = 2
VMEM_LIMIT = 56 * 1024 * 1024


def _sigmoid(x):
    return 1.0 / (1.0 + jnp.exp(-x))


def _dot(a, b):
    return jnp.dot(a, b, preferred_element_type=F32)


def _dot_exact(a, b):
    return jnp.dot(a, b, preferred_element_type=F32, precision=lax.Precision.HIGHEST)


def _head_mask():
    return lax.broadcasted_iota(jnp.int32, (1, LANES), 1) < HEAD_DIM


def _headnorm(x, g, is_a):
    x2 = x * x
    s_a = jnp.sum(jnp.where(is_a, x2, 0.0), axis=-1, keepdims=True)
    s_b = jnp.sum(jnp.where(is_a, 0.0, x2), axis=-1, keepdims=True)
    ms = jnp.where(is_a, s_a, s_b) * (1.0 / HEAD_DIM)
    return (x * lax.rsqrt(ms + RMS_EPS)) * g


def _inproj_kernel(x_ref, g_ref, w_ref, gq_ref, gk_ref, f_ref, qkv_ref, w_scr):
    @pl.when(pl.program_id(0) == 0)
    def _():
        a0 = 2 * SSM_WIDTH
        w_scr[:, 0:a0] = w_ref[:, 0:a0].astype(BF16)
        w_scr[:, a0:F32_COLS] = w_ref[:, a0 + QKV_COLS:IN_COLS].astype(BF16)
        w_scr[:, F32_COLS:IN_COLS] = w_ref[:, a0:a0 + QKV_COLS].astype(BF16)

    x = x_ref[...]
    ms = jnp.mean(x * x, axis=-1, keepdims=True)
    hn = (x * lax.rsqrt(ms + RMS_EPS)) * g_ref[...]
    res = _dot(hn.astype(BF16), w_scr[...])
    f_ref[...] = res[:, 0:SSM_WIDTH]
    qkv_ref[:, QKV_COLS:BF_COLS] = res[:, SSM_WIDTH:F32_COLS].astype(BF16)
    is_a = _head_mask()
    q_scale = (HEAD_DIM ** -0.5) * LOG2E
    for s in range(HEAD_PAIRS):
        c0 = F32_COLS + s * LANES
        q = _headnorm(res[:, c0:c0 + LANES], gq_ref[...], is_a) * q_scale
        qkv_ref[:, s * LANES:(s + 1) * LANES] = q.astype(BF16)
        c1 = c0 + ATTN_WIDTH
        k = _headnorm(res[:, c1:c1 + LANES], gk_ref[...], is_a)
        qkv_ref[:, ATTN_WIDTH + s * LANES:ATTN_WIDTH + (s + 1) * LANES] = k.astype(BF16)
    qkv_ref[:, 2 * ATTN_WIDTH:3 * ATTN_WIDTH] = res[:, F32_COLS + 2 * ATTN_WIDTH:IN_COLS].astype(BF16)


def _layer_spec(shape, layer):
    return pl.BlockSpec((None,) + shape, lambda i: (layer,) + (0,) * len(shape), pipeline_mode=pl.Buffered(1))


def _inproj(h, g_all, w_all, gq_all, gk_all, layer):
    t = h.shape[0]
    tm = min(TOKEN_TILE, t)
    return pl.pallas_call(
        _inproj_kernel,
        grid=(t // tm,),
        in_specs=[
            pl.BlockSpec((tm, D_MODEL), lambda i: (i, 0)),
            _layer_spec((1, D_MODEL), layer),
            _layer_spec((D_MODEL, IN_COLS), layer),
            _layer_spec((1, LANES), layer),
            _layer_spec((1, LANES), layer),
        ],
        out_specs=[pl.BlockSpec((tm, SSM_WIDTH), lambda i: (i, 0)),
                   pl.BlockSpec((tm, BF_COLS), lambda i: (i, 0))],
        out_shape=[jax.ShapeDtypeStruct((t, SSM_WIDTH), F32), jax.ShapeDtypeStruct((t, BF_COLS), BF16)],
        scratch_shapes=[pltpu.VMEM((D_MODEL, IN_COLS), BF16)],
        compiler_params=pltpu.CompilerParams(
            dimension_semantics=("arbitrary",), vmem_limit_bytes=VMEM_LIMIT),
        name="inproj",
    )(h, g_all, w_all, gq_all, gk_all)


def _ssm_setup_kernel(are_r, aim_r, ldt_r, bre_ref, bim_ref, cre_ref, cim_ref,
                      w1_ref, w2_ref, pw_ref, *, n_pw, n_lvl2):
    ns = SLAB_STATES
    a_re = are_r[...]
    a_im = aim_r[...]
    dt = jnp.exp(ldt_r[...])
    lam_r = a_re * dt
    lam_i = a_im * dt
    mag = jnp.exp(lam_r)
    ab_re = mag * jnp.cos(lam_i)
    ab_im = mag * jnp.sin(lam_i)
    num_re = ab_re - 1.0
    num_im = ab_im
    den = a_re * a_re + a_im * a_im
    f_re = (num_re * a_re + num_im * a_im) / den
    f_im = (num_im * a_re - num_re * a_im) / den
    b_re = bre_ref[...]
    b_im = bim_ref[...]
    bb_re = jnp.tile(f_re * b_re - f_im * b_im, (SLAB_GROUPS, 1))
    bb_im = jnp.tile(f_re * b_im + f_im * b_re, (SLAB_GROUPS, 1))
    grp_row = lax.broadcasted_iota(jnp.int32, (LANES, ns), 0) >> 4
    grp_col = lax.broadcasted_iota(jnp.int32, (LANES, ns), 1) >> 6
    same_in = grp_row == grp_col
    zin = jnp.zeros((LANES, ns), F32)
    bb_re = jnp.where(same_in, bb_re, zin)
    bb_im = jnp.where(same_in, bb_im, zin)

    e_in = (SSM_CHUNK - 1 - lax.broadcasted_iota(jnp.int32, (SSM_CHUNK, 1), 0)).astype(F32)
    mag_in = jnp.exp(lam_r * e_in)
    pin_re = mag_in * jnp.cos(lam_i * e_in)
    pin_im = mag_in * jnp.sin(lam_i * e_in)
    for lp in range(SSM_CHUNK):
        p_re = pin_re[lp:lp + 1, :]
        p_im = pin_im[lp:lp + 1, :]
        w1_ref[lp * LANES:(lp + 1) * LANES, 0:ns] = (p_re * bb_re - p_im * bb_im).astype(BF16)
        w1_ref[lp * LANES:(lp + 1) * LANES, ns:2 * ns] = (p_re * bb_im + p_im * bb_re).astype(BF16)

    grp_row_o = lax.broadcasted_iota(jnp.int32, (ns, LANES), 0) >> 6
    grp_col_o = lax.broadcasted_iota(jnp.int32, (ns, LANES), 1) >> 4
    same_out = grp_row_o == grp_col_o
    zout = jnp.zeros((ns, LANES), F32)
    c_re = jnp.where(same_out, cre_ref[...], zout)
    c_im = jnp.where(same_out, cim_ref[...], zout)

    lag = lax.broadcasted_iota(jnp.int32, (2 * SSM_CHUNK, 1), 0).astype(F32)
    mag_c = jnp.exp(lam_r * lag)
    fill = jnp.zeros((LANES - 2 * SSM_CHUNK, ns), F32)
    pc_re = jnp.concatenate([mag_c * jnp.cos(lam_i * lag), fill], axis=0).T
    pc_im = jnp.concatenate([mag_c * jnp.sin(lam_i * lag), fill], axis=0).T
    c_pow = []
    for l in range(SSM_CHUNK + 1):
        q_re = jnp.broadcast_to(pc_re[:, l:l + 1], (ns, LANES))
        q_im = jnp.broadcast_to(pc_im[:, l:l + 1], (ns, LANES))
        c_pow.append((c_re * q_re - c_im * q_im, -c_re * q_im - c_im * q_re))
    kf = []
    for l in range(SSM_CHUNK):
        w2_ref[0:ns, l * LANES:(l + 1) * LANES] = c_pow[l + 1][0].astype(BF16)
        w2_ref[ns:2 * ns, l * LANES:(l + 1) * LANES] = c_pow[l + 1][1].astype(BF16)
        kf.append((_dot_exact(bb_re, c_pow[l][0]) + _dot_exact(bb_im, c_pow[l][1])).astype(BF16))
    zblk = jnp.zeros((LANES, LANES), BF16)
    for lp in range(SSM_CHUNK):
        for l in range(SSM_CHUNK):
            w1_ref[lp * LANES:(lp + 1) * LANES, 2 * ns + l * LANES:2 * ns + (l + 1) * LANES] = (
                kf[l - lp] if l >= lp else zblk)

    prow = lax.broadcasted_iota(jnp.int32, (n_pw, 1), 0)
    dbl = jnp.clip(prow - SCAN_RADIX, 0, max(n_lvl2 - 1, 0))
    e_pw = jnp.where(prow < SCAN_RADIX, SSM_CHUNK * prow,
                     jnp.left_shift(SSM_CHUNK * SCAN_RADIX, dbl)).astype(F32)
    mag_pw = jnp.exp(lam_r * e_pw)
    pw_ref[:, 0:ns] = mag_pw * jnp.cos(lam_i * e_pw)
    pw_ref[:, ns:2 * ns] = mag_pw * jnp.sin(lam_i * e_pw)


def _ssm_setup(a_re, a_im, log_dt, b_re, b_im, c_re, c_im, n_pw, n_lvl2):
    depth = a_re.shape[0]
    g, p, h = a_re.shape[1], SSM_STATE, SSM_GROUP
    ns = SLAB_STATES
    ldt = jnp.broadcast_to(log_dt[:, :, None], (depth, g, p))

    def row(v):
        return v.reshape(depth, SSM_SLABS, 1, ns)

    def b_lay(v):
        return v.reshape(depth, SSM_SLABS, SLAB_GROUPS, p, h).transpose(0, 1, 4, 2, 3).reshape(
            depth, SSM_SLABS, h, ns)

    def c_lay(v):
        vt = v.reshape(depth, SSM_SLABS, SLAB_GROUPS, h, p).transpose(0, 1, 2, 4, 3).reshape(
            depth, SSM_SLABS, ns, h)
        return jnp.tile(vt, (1, 1, 1, SLAB_GROUPS))

    def spec(shape):
        return pl.BlockSpec((None, None) + shape, lambda d, q: (d, q) + (0,) * len(shape))

    return pl.pallas_call(
        functools.partial(_ssm_setup_kernel, n_pw=n_pw, n_lvl2=n_lvl2),
        grid=(depth, SSM_SLABS),
        in_specs=[spec((1, ns))] * 3 + [spec((h, ns))] * 2 + [spec((ns, LANES))] * 2,
        out_specs=[spec((CHUNK_COLS, 2 * ns + CHUNK_COLS)), spec((2 * ns, CHUNK_COLS)), spec((n_pw, 2 * ns))],
        out_shape=[
            jax.ShapeDtypeStruct((depth, SSM_SLABS, CHUNK_COLS, 2 * ns + CHUNK_COLS), BF16),
            jax.ShapeDtypeStruct((depth, SSM_SLABS, 2 * ns, CHUNK_COLS), BF16),
            jax.ShapeDtypeStruct((depth, SSM_SLABS, n_pw, 2 * ns), F32),
        ],
        compiler_params=pltpu.CompilerParams(
            dimension_semantics=("arbitrary", "arbitrary"), vmem_limit_bytes=VMEM_LIMIT),
        name="ssm_setup",
    )(row(a_re), row(a_im), row(ldt), b_lay(b_re), b_lay(b_im), c_lay(c_re), c_lay(c_im))


def _ssm_main_kernel(u_all, w1_ref, w2_ref, pw_ref, y_all, *scratch, n_chunks, n_lvl2, pad, nb):
    per = len(scratch) // nb
    y_intras = [None] * nb
    for phase in (_ssm_project, _ssm_scan, _ssm_emit):
        for bb in range(nb):
            phase(u_all.at[bb], w1_ref, w2_ref, pw_ref, y_all.at[bb], y_intras, bb,
                  *scratch[bb * per:(bb + 1) * per], n_chunks=n_chunks, n_lvl2=n_lvl2, pad=pad)


def _ssm_project(u_ref, w1_ref, w2_ref, pw_ref, y_ref, y_intras, bb, lhs_ref, z_scr, sp_scr, g_a, g_b,
                 *, n_chunks, n_lvl2, pad):
    ns = SLAB_STATES
    nsl = ns // LANES
    for l in range(SSM_CHUNK):
        lhs_ref[:, l * LANES:(l + 1) * LANES] = u_ref[pl.ds(l, n_chunks, stride=SSM_CHUNK), :].astype(BF16)
    zin = _dot(lhs_ref[...], w1_ref[:, 0:2 * ns])
    for j in range(2 * nsl):
        z_scr[j] = zin[:, j * LANES:(j + 1) * LANES]
    y_intras[bb] = [_dot(lhs_ref[:, 0:(c + 1) * MXU_TILE],
                         w1_ref[0:(c + 1) * MXU_TILE, 2 * ns + c * MXU_TILE:2 * ns + (c + 1) * MXU_TILE])
                    for c in range(CHUNK_COLS // MXU_TILE)]


def _ssm_scan(u_ref, w1_ref, w2_ref, pw_ref, y_ref, y_intras, bb, lhs_ref, z_scr, sp_scr, g_a, g_b,
              *, n_chunks, n_lvl2, pad):
    ns = SLAB_STATES
    nsl = ns // LANES
    ng = n_chunks // SCAN_RADIX
    for js in range(nsl):
        jr, ji = js, nsl + js
        lr = slice(js * LANES, (js + 1) * LANES)
        li = slice(ns + js * LANES, ns + (js + 1) * LANES)
        a_r = pw_ref[1:2, lr]
        a_i = pw_ref[1:2, li]
        xs = []
        for lo in range(SCAN_RADIX):
            z_r = z_scr[jr, pl.ds(lo, ng, stride=SCAN_RADIX), :]
            z_i = z_scr[ji, pl.ds(lo, ng, stride=SCAN_RADIX), :]
            if lo == 0:
                x_r, x_i = z_r, z_i
            else:
                x_r, x_i = a_r * x_r - a_i * x_i + z_r, a_r * x_i + a_i * x_r + z_i
            xs.append((x_r, x_i))
        ga, gb = g_a.at[js], g_b.at[js]
        zeros = jnp.zeros((pad, 2 * LANES), F32)
        ga[0:pad, :] = zeros
        gb[0:pad, :] = zeros
        ga[pad:pad + ng, 0:LANES] = x_r
        ga[pad:pad + ng, LANES:2 * LANES] = x_i
        src, dst = ga, gb
        for k in range(n_lvl2):
            d = 1 << k
            t_r = src[pad:pad + ng, 0:LANES]
            t_i = src[pad:pad + ng, LANES:2 * LANES]
            s_r = src[pad - d:pad - d + ng, 0:LANES]
            s_i = src[pad - d:pad - d + ng, LANES:2 * LANES]
            p_r = pw_ref[SCAN_RADIX + k:SCAN_RADIX + k + 1, lr]
            p_i = pw_ref[SCAN_RADIX + k:SCAN_RADIX + k + 1, li]
            dst[pad:pad + ng, 0:LANES] = t_r + p_r * s_r - p_i * s_i
            dst[pad:pad + ng, LANES:2 * LANES] = t_i + p_r * s_i + p_i * s_r
            src, dst = dst, src
        sg_r = src[pad - 1:pad - 1 + ng, 0:LANES]
        sg_i = src[pad - 1:pad - 1 + ng, LANES:2 * LANES]
        for lo in range(SCAN_RADIX):
            if lo == 0:
                s_r, s_i = sg_r, sg_i
            else:
                p_r = pw_ref[lo:lo + 1, lr]
                p_i = pw_ref[lo:lo + 1, li]
                s_r = p_r * sg_r - p_i * sg_i + xs[lo - 1][0]
                s_i = p_r * sg_i + p_i * sg_r + xs[lo - 1][1]
            sp_scr[jr, pl.ds(lo, ng, stride=SCAN_RADIX), :] = s_r
            sp_scr[ji, pl.ds(lo, ng, stride=SCAN_RADIX), :] = s_i


def _ssm_emit(u_ref, w1_ref, w2_ref, pw_ref, y_ref, y_intras, bb, lhs_ref, z_scr, sp_scr, g_a, g_b,
              *, n_chunks, n_lvl2, pad):
    nsl = SLAB_STATES // LANES
    s_prev = jnp.concatenate([sp_scr[j] for j in range(2 * nsl)], axis=1).astype(BF16)
    y = jnp.concatenate(y_intras[bb], axis=1) + _dot(s_prev, w2_ref[...])
    for l in range(SSM_CHUNK):
        y_ref[pl.ds(l, n_chunks, stride=SSM_CHUNK), :] = y[:, l * LANES:(l + 1) * LANES]


def _ssm_main(pf3, w1, w2, pw, n_lvl2, layer):
    bsz, seq, _ = pf3.shape
    n_chunks = seq // SSM_CHUNK
    ns = SLAB_STATES
    nsl = ns // LANES
    ng = n_chunks // SCAN_RADIX
    pad = max(8, ng // 2)
    n_pw = pw.shape[2]
    nb = SSM_SEQS_PER_STEP if bsz % SSM_SEQS_PER_STEP == 0 else 1
    kern = functools.partial(_ssm_main_kernel, n_chunks=n_chunks, n_lvl2=n_lvl2, pad=pad, nb=nb)
    return pl.pallas_call(
        kern,
        grid=(SSM_SLABS, bsz // nb),
        in_specs=[
            pl.BlockSpec((nb, seq, LANES), lambda s, b: (b, 0, s)),
            pl.BlockSpec((None, None, CHUNK_COLS, 2 * ns + CHUNK_COLS), lambda s, b: (layer, s, 0, 0)),
            pl.BlockSpec((None, None, 2 * ns, CHUNK_COLS), lambda s, b: (layer, s, 0, 0)),
            pl.BlockSpec((None, None, n_pw, 2 * ns), lambda s, b: (layer, s, 0, 0)),
        ],
        out_specs=pl.BlockSpec((nb, seq, LANES), lambda s, b: (b, 0, s)),
        out_shape=jax.ShapeDtypeStruct((bsz, seq, SSM_WIDTH), F32),
        scratch_shapes=[pltpu.VMEM((n_chunks, CHUNK_COLS), BF16),
                        pltpu.VMEM((2 * nsl, n_chunks, LANES), F32),
                        pltpu.VMEM((2 * nsl, n_chunks, LANES), F32),
                        pltpu.VMEM((nsl, pad + ng, 2 * LANES), F32),
                        pltpu.VMEM((nsl, pad + ng, 2 * LANES), F32)] * nb,
        compiler_params=pltpu.CompilerParams(
            dimension_semantics=("arbitrary", "arbitrary"), vmem_limit_bytes=VMEM_LIMIT),
        name="ssm_main",
    )(pf3, w1, w2, pw)


def _attn_kernel(q_ref, k_ref, v_ref, tri1_ref, tri2_ref, o_ref, acc_ref, stay_ref, kv_scr,
                 *, qb, kw, nq, top, seq):
    i = pl.program_id(2)
    q0 = pl.multiple_of(i * (nq * qb), nq * qb)
    n_heads = 2 * PAIRS_PER_STEP

    @pl.when(i == 0)
    def _():
        is_a = jnp.concatenate([_head_mask()] * PAIRS_PER_STEP, axis=1)
        chunk = min(seq, 512)

        def fill(c, carry):
            r = pl.multiple_of(c * chunk, chunk)
            for j, ref in enumerate((k_ref, v_ref)):
                blk = ref[pl.ds(r, chunk), :]
                zero = jnp.zeros_like(blk)
                kv_scr[2 * j, pl.ds(r, chunk), :] = jnp.where(is_a, blk, zero)
                kv_scr[2 * j + 1, pl.ds(r, chunk), :] = jnp.where(is_a, zero, blk)
            return carry
        lax.fori_loop(0, seq // chunk, fill, 0)

    d_row = lax.broadcasted_iota(jnp.int32, (qb, 2 * qb), 0)
    d_key = lax.broadcasted_iota(jnp.int32, (qb, 2 * qb), 1) & (qb - 1)
    causal = d_key < d_row
    w_col = lax.broadcasted_iota(jnp.int32, (qb, kw), 1)
    tris = {qb: tri1_ref[...], kw: tri2_ref[...]}

    def rows_into(full, part, r0, nr):
        def rest(a, b):
            return jnp.zeros((b - a, part.shape[1]), part.dtype) if full is None else full[a:b]
        pieces = ([rest(0, r0)] if r0 > 0 else []) + [part] + ([rest(r0 + nr, qb)] if r0 + nr < qb else [])
        return pieces[0] if len(pieces) == 1 else jnp.concatenate(pieces, axis=0)

    def sweep(parts, stay):
        stay = [list(row) for row in stay]
        unswept = [[set(range(0, qb, ROW_GRANULE)) if st is None else set() for st in row] for row in stay]
        units = [(pi, p) for pi in range(len(parts)) for p in range(PAIRS_PER_STEP)]
        zs, sps, lefts, laters, ws = {}, {}, {}, {}, {}

        def stacked(j, k0, width, p):
            cols = slice(p * LANES, (p + 1) * LANES)
            return jnp.concatenate([kv_scr[2 * j, pl.ds(k0, width), cols],
                                    kv_scr[2 * j + 1, pl.ds(k0, width), cols]], axis=0)

        def scores(u):
            pi, p = u
            s, r0, nr, k0, width, _ = parts[pi]
            qp = q_ref[s * qb + r0:s * qb + r0 + nr, p * LANES:(p + 1) * LANES]
            zs[u] = lax.dot_general(qp, stacked(0, k0, width, p), (((1,), (1,)), ((), ())),
                                    preferred_element_type=F32)

        def stick(u):
            pi, p = u
            s, r0, nr, _, width, mask = parts[pi]
            z = zs.pop(u)
            sp = jnp.maximum(z, 0.0) + jnp.log(1.0 + jnp.exp2(-jnp.abs(z))) * LOG2E
            if mask is not None:
                sp = jnp.where(mask, sp, 0.0)
            base = z - sp
            touched = set(range(r0, r0 + nr, ROW_GRANULE))
            halves = []
            for hd in range(2):
                n = 2 * p + hd
                cols = slice(hd * width, (hd + 1) * width)
                tot = jnp.sum(sp[:, cols], axis=-1, keepdims=True)
                if touched <= unswept[s][n]:
                    halves.append(None)
                    rows = tot
                else:
                    seen = stay[s][n][r0:r0 + nr]
                    halves.append(base[:, cols] - seen)
                    rows = seen + tot
                unswept[s][n] = unswept[s][n] - touched
                stay[s][n] = rows if nr == qb else rows_into(stay[s][n], rows, r0, nr)
            sps[u] = sp.astype(BF16)
            lefts[u] = base if halves[0] is None else jnp.concatenate(halves, axis=1)

        def later(u):
            laters[u] = _dot(sps.pop(u), tris[parts[u[0]][4]])

        def weight(u):
            mask = parts[u[0]][5]
            w = jnp.exp2(lefts.pop(u) - laters.pop(u))
            if mask is not None:
                w = jnp.where(mask, w, 0.0)
            ws[u] = w.astype(BF16)

        for stage in (scores, stick, later, weight):
            for u in units:
                stage(u)
        pvs = [[None] * PAIRS_PER_STEP for _ in range(nq)]
        for pi, (s, r0, nr, k0, width, _) in enumerate(parts):
            for p in range(PAIRS_PER_STEP):
                d = rows_into(None, _dot(ws.pop((pi, p)), stacked(1, k0, width, p)), r0, nr)
                pvs[s][p] = d if pvs[s][p] is None else pvs[s][p] + d
        return pvs, stay

    def commit(pvs, stay, first):
        low = None
        for s in range(nq):
            for p in range(PAIRS_PER_STEP):
                acc_ref[s, p] = pvs[s][p] if first else acc_ref[s, p] + pvs[s][p]
            for n in range(n_heads):
                stay_ref[s, n] = jnp.broadcast_to(stay[s][n], (qb, LANES))
                low = stay[s][n] if low is None else jnp.minimum(low, stay[s][n])
        return jnp.min(low)

    fresh = [[None] * n_heads for _ in range(nq)]
    diags = [(s, 0, qb, pl.multiple_of(q0 + s * qb, qb), qb, causal) for s in range(nq)]

    def stay_now():
        return [[stay_ref[s, n][:, 0:1] for n in range(n_heads)] for s in range(nq)]

    def key_block(s, back):
        return pl.multiple_of(q0 + (s - back) * qb, qb)

    def main_step():
        near = [(s, 0, qb, key_block(s, 1), qb, None) for s in range(nq)]
        far_top = [(s, 0, top, key_block(s, 2), qb, None) for s in range(nq)]
        return commit(*sweep(diags + near + far_top, fresh), True)

    def far_rest():
        parts = [(s, top, qb - top, key_block(s, 2), qb, None) for s in range(nq)]
        return commit(*sweep(parts, stay_now()), False)

    def first_step():
        prevs = [(s, 0, qb, j * qb, qb, None) for s in range(nq) for j in reversed(range(s))]
        return commit(*sweep(diags + prevs, fresh), True)

    has_window = q0 >= kw
    low1 = lax.cond(has_window, main_step, first_step)
    low2 = lax.cond(jnp.logical_and(has_window, low1 < STAY_BITS_DONE), far_rest, lambda: low1)
    start1 = jnp.where(has_window, q0 + (nq - 1) * qb - kw, 0)

    def cond(c):
        start, low = c
        return jnp.logical_and(start > 0, low < STAY_BITS_DONE)

    def body(c):
        start, _ = c
        parts = []
        for s in range(nq):
            start_s = jnp.maximum(start - (nq - 1 - s) * qb, 0)
            k0 = pl.multiple_of(jnp.maximum(start_s - kw, 0), qb)
            valid = (k0 + w_col) < start_s
            parts.append((s, 0, qb, k0, kw, jnp.concatenate([valid, valid], axis=1)))
        return jnp.maximum(start - kw, 0), commit(*sweep(parts, stay_now()), False)

    lax.while_loop(cond, body, (start1, low2))
    for s in range(nq):
        for p in range(PAIRS_PER_STEP):
            o_ref[s * qb:(s + 1) * qb, p * LANES:(p + 1) * LANES] = acc_ref[s, p].astype(BF16)


def _attention(qkv3):
    bsz, seq, _ = qkv3.shape
    qb = min(ATTN_QUERY_BLOCK, seq)
    kw = min(ATTN_KEY_WINDOW, seq)
    nq = ATTN_SUBBLOCKS
    blk = nq * qb
    assert seq % blk == 0 and kw == 2 * qb and blk >= kw
    w = PAIRS_PER_STEP * LANES
    n_grp = HEAD_PAIRS // PAIRS_PER_STEP
    kern = functools.partial(_attn_kernel, qb=qb, kw=kw, nq=nq, top=ATTN_TOP_ROWS, seq=seq)

    def pair_tri(n):
        return jnp.kron(jnp.eye(2, dtype=F32), jnp.tri(n, n, -1, dtype=F32)).astype(BF16)
    return pl.pallas_call(
        kern,
        grid=(bsz, n_grp, seq // blk),
        in_specs=[
            pl.BlockSpec((None, blk, w), lambda b, g, i: (b, i, g)),
            pl.BlockSpec((None, seq, w), lambda b, g, i: (b, 0, n_grp + g)),
            pl.BlockSpec((None, seq, w), lambda b, g, i: (b, 0, 2 * n_grp + g)),
            pl.BlockSpec((2 * qb, 2 * qb), lambda b, g, i: (0, 0)),
            pl.BlockSpec((2 * kw, 2 * kw), lambda b, g, i: (0, 0)),
        ],
        out_specs=pl.BlockSpec((None, blk, w), lambda b, g, i: (b, i, g)),
        out_shape=jax.ShapeDtypeStruct((bsz, seq, ATTN_WIDTH), BF16),
        scratch_shapes=[pltpu.VMEM((nq, PAIRS_PER_STEP, qb, LANES), F32),
                        pltpu.VMEM((nq, 2 * PAIRS_PER_STEP, qb, LANES), F32),
                        pltpu.VMEM((4, seq, w), BF16)],
        compiler_params=pltpu.CompilerParams(
            dimension_semantics=("arbitrary", "arbitrary", "arbitrary"), vmem_limit_bytes=VMEM_LIMIT),
        name="attention",
    )(qkv3, qkv3, qkv3, pair_tri(qb), pair_tri(kw))


def _out_kernel(h_ref, ys_ref, u_ref, gs_ref, ga_ref, ya_ref, p_ref, d_ref, wgf_ref, bg_ref, wof_ref,
                pg_ref, wpgf_ref, wppf_ref, o_ref, wg_ref, wo1_ref, wo2_ref, wpg_ref, wpp_ref):
    @pl.when(pl.program_id(0) == 0)
    def _():
        wg_ref[...] = wgf_ref[...].astype(BF16)
        wo1_ref[...] = wof_ref[0:SSM_WIDTH, :].astype(BF16)
        wo2_ref[...] = wof_ref[SSM_WIDTH:SSM_WIDTH + ATTN_WIDTH, :].astype(BF16)
        wpg_ref[...] = wpgf_ref[...].astype(BF16)
        wpp_ref[...] = wppf_ref[...].astype(BF16)

    rows = h_ref.shape[0] // OUT_SUBTILES
    subs = [slice(i * rows, (i + 1) * rows) for i in range(OUT_SUBTILES)]
    zs = []
    for sl in subs:
        y = ys_ref[sl, :] + d_ref[...] * u_ref[sl, :]
        z = 0.5 * y * (1.0 + jnp.tanh(math.sqrt(2.0 / math.pi) * (y + 0.044715 * (y * y * y))))
        zs.append(z.astype(BF16))
    zzs = [_dot(z, wg_ref[...]) + bg_ref[...] for z in zs]
    y_ssms = []
    for sl, zz in zip(subs, zzs):
        gs = gs_ref[sl, :].astype(F32)
        y_ssm = (zz[:, 0:SSM_WIDTH] * _sigmoid(zz[:, SSM_WIDTH:2 * SSM_WIDTH])) * (gs * _sigmoid(gs))
        y_ssms.append(y_ssm.astype(BF16))
    y_atts = []
    for sl in subs:
        ga = ga_ref[sl, :].astype(F32)
        y_atts.append((ya_ref[sl, :].astype(F32) * (ga * _sigmoid(ga))).astype(BF16))
    hs = [h_ref[sl, :] + _dot(y_ssm, wo1_ref[...]) + _dot(y_att, wo2_ref[...])
          for sl, y_ssm, y_att in zip(subs, y_ssms, y_atts)]
    hns = []
    for h in hs:
        ms = jnp.mean(h * h, axis=-1, keepdims=True)
        hns.append(((h * lax.rsqrt(ms + RMS_EPS)) * pg_ref[...]).astype(BF16))
    gates = [_sigmoid(_dot(hn, wpg_ref[...])) for hn in hns]
    for sl, h, gate in zip(subs, hs, gates):
        o_ref[sl, :] = h + gate * _dot(p_ref[sl, :].astype(BF16), wpp_ref[...])


def _out(h, y_s, u, bf, y_att, p_all, d_all, w_glu_all, b_glu_all, w_out_all, ple_g_all, w_pg_all, w_pp_all, layer):
    t = h.shape[0]
    tm = min(TOKEN_TILE, t)
    return pl.pallas_call(
        _out_kernel,
        grid=(t // tm,),
        in_specs=[
            pl.BlockSpec((tm, D_MODEL), lambda i: (i, 0)),
            pl.BlockSpec((tm, SSM_WIDTH), lambda i: (i, 0)),
            pl.BlockSpec((tm, SSM_WIDTH), lambda i: (i, 0)),
            pl.BlockSpec((tm, SSM_WIDTH), lambda i: (i, QKV_COLS // SSM_WIDTH)),
            pl.BlockSpec((tm, ATTN_WIDTH), lambda i: (i, QKV_COLS // ATTN_WIDTH + 1)),
            pl.BlockSpec((tm, ATTN_WIDTH), lambda i: (i, 0)),
            pl.BlockSpec((None, tm, PLE_DIM), lambda i: (layer, i, 0)),
            _layer_spec((1, SSM_WIDTH), layer),
            _layer_spec((SSM_WIDTH, 2 * SSM_WIDTH), layer),
            _layer_spec((1, 2 * SSM_WIDTH), layer),
            _layer_spec((SSM_WIDTH + ATTN_WIDTH, D_MODEL), layer),
            _layer_spec((1, D_MODEL), layer),
            _layer_spec((D_MODEL, D_MODEL), layer),
            _layer_spec((PLE_DIM, D_MODEL), layer),
        ],
        out_specs=pl.BlockSpec((tm, D_MODEL), lambda i: (i, 0)),
        out_shape=jax.ShapeDtypeStruct((t, D_MODEL), F32),
        scratch_shapes=[pltpu.VMEM((SSM_WIDTH, 2 * SSM_WIDTH), BF16), pltpu.VMEM((SSM_WIDTH, D_MODEL), BF16),
                        pltpu.VMEM((ATTN_WIDTH, D_MODEL), BF16), pltpu.VMEM((D_MODEL, D_MODEL), BF16),
                        pltpu.VMEM((PLE_DIM, D_MODEL), BF16)],
        compiler_params=pltpu.CompilerParams(
            dimension_semantics=("arbitrary",), vmem_limit_bytes=VMEM_LIMIT),
        name="out",
    )(h, y_s, u, bf, bf, y_att, p_all, d_all, w_glu_all, b_glu_all, w_out_all, ple_g_all, w_pg_all, w_pp_all)


def kernel(x, p, mix_norm_g, w_in, ssm_a_re, ssm_a_im, ssm_log_dt, ssm_b_re, ssm_b_im, ssm_c_re, ssm_c_im,
           ssm_d, ssm_w_glu, ssm_b_glu, q_norm_g, k_norm_g, w_out, ple_norm_g, w_ple_gate, w_ple_proj):
    bsz, seq, _ = x.shape
    depth = w_in.shape[0]
    t = bsz * seq
    n_chunks = seq // SSM_CHUNK
    assert seq % SSM_CHUNK == 0 and t % min(TOKEN_TILE, t) == 0
    n_groups = n_chunks // SCAN_RADIX
    assert n_chunks % SCAN_RADIX == 0 and n_groups & (n_groups - 1) == 0
    n_lvl2 = n_groups.bit_length() - 1
    n_pw = -(-(SCAN_RADIX + n_lvl2) // 8) * 8

    w1, w2, pw = _ssm_setup(ssm_a_re, ssm_a_im, ssm_log_dt, ssm_b_re, ssm_b_im, ssm_c_re, ssm_c_im, n_pw, n_lvl2)
    gq2 = jnp.tile(q_norm_g.reshape(depth, 1, HEAD_DIM), (1, 1, 2))
    gk2 = jnp.tile(k_norm_g.reshape(depth, 1, HEAD_DIM), (1, 1, 2))
    mix_g = mix_norm_g.reshape(depth, 1, D_MODEL)
    ple_g = ple_norm_g.reshape(depth, 1, D_MODEL)
    d_skip = ssm_d.reshape(depth, 1, SSM_WIDTH)
    b_glu = ssm_b_glu.reshape(depth, 1, 2 * SSM_WIDTH)
    p3 = p.reshape(depth, t, PLE_DIM)

    h = x.reshape(t, D_MODEL)
    for i in range(depth):
        u, bf = _inproj(h, mix_g, w_in, gq2, gk2, i)
        y_s = _ssm_main(u.reshape(bsz, seq, SSM_WIDTH), w1, w2, pw, n_lvl2, i).reshape(t, SSM_WIDTH)
        y_att = _attention(bf.reshape(bsz, seq, BF_COLS)).reshape(t, ATTN_WIDTH)
        h = _out(h, y_s, u, bf, y_att, p3, d_skip, ssm_w_glu, b_glu, w_out, ple_g, w_ple_gate, w_ple_proj, i)
    return h.reshape(bsz, seq, D_MODEL)
```

```python
import functools
import math

import jax
import jax.numpy as jnp
from jax import lax
from jax.experimental import pallas as pl
from jax.experimental.pallas import tpu as pltpu

F32 = jnp.float32
BF16 = jnp.bfloat16

D_MODEL = 1024
PLE_DIM = 256
SSM_WIDTH = 512
SSM_GROUP = 16
SSM_STATE = 64
LANES = 128
MXU_TILE = 256
SLAB_GROUPS = LANES // SSM_GROUP
SSM_SLABS = SSM_WIDTH // LANES
SLAB_STATES = SLAB_GROUPS * SSM_STATE
SSM_CHUNK = 8
CHUNK_COLS = SSM_CHUNK * LANES
SSM_SEQS_PER_STEP = 1
ROW_GRANULE = 16
SCAN_RADIX = 4
ATTN_WIDTH = 512
HEAD_DIM = 64
HEAD_PAIRS = ATTN_WIDTH // LANES
PAIRS_PER_STEP = 4
IN_COLS = 3072
F32_COLS = 2 * SSM_WIDTH + ATTN_WIDTH
QKV_COLS = 3 * ATTN_WIDTH
BF_COLS = QKV_COLS + SSM_WIDTH + ATTN_WIDTH
RMS_EPS = 1e-6
LOG2E = 1.4426950408889634
ATTN_QUERY_BLOCK = 128
ATTN_SUBBLOCKS = 4
ATTN_TOP_ROWS = 32
ATTN_KEY_WINDOW = 256
STAY_BITS_DONE = 152.0
TOKEN_TILE = 512
OUT_SUBTILES = 2
VMEM_LIMIT = 56 * 1024 * 1024


def _sigmoid(x):
    return 1.0 / (1.0 + jnp.exp(-x))


def _dot(a, b):
    return jnp.dot(a, b, preferred_element_type=F32)


def _dot_exact(a, b):
    return jnp.dot(a, b, preferred_element_type=F32, precision=lax.Precision.HIGHEST)


def _head_mask():
    return lax.broadcasted_iota(jnp.int32, (1, LANES), 1) < HEAD_DIM


def _headnorm(x, g, is_a):
    x2 = x * x
    s_a = jnp.sum(jnp.where(is_a, x2, 0.0), axis=-1, keepdims=True)
    s_b = jnp.sum(jnp.where(is_a, 0.0, x2), axis=-1, keepdims=True)
    ms = jnp.where(is_a, s_a, s_b) * (1.0 / HEAD_DIM)
    return (x * lax.rsqrt(ms + RMS_EPS)) * g


def _inproj_kernel(x_ref, g_ref, w_ref, gq_ref, gk_ref, f_ref, qkv_ref, w_scr):
    @pl.when(pl.program_id(0) == 0)
    def _():
        a0 = 2 * SSM_WIDTH
        w_scr[:, 0:a0] = w_ref[:, 0:a0].astype(BF16)
        w_scr[:, a0:F32_COLS] = w_ref[:, a0 + QKV_COLS:IN_COLS].astype(BF16)
        w_scr[:, F32_COLS:IN_COLS] = w_ref[:, a0:a0 + QKV_COLS].astype(BF16)

    x = x_ref[...]
    ms = jnp.mean(x * x, axis=-1, keepdims=True)
    hn = (x * lax.rsqrt(ms + RMS_EPS)) * g_ref[...]
    res = _dot(hn.astype(BF16), w_scr[...])
    f_ref[...] = res[:, 0:SSM_WIDTH]
    qkv_ref[:, QKV_COLS:BF_COLS] = res[:, SSM_WIDTH:F32_COLS].astype(BF16)
    is_a = _head_mask()
    q_scale = (HEAD_DIM ** -0.5) * LOG2E
    for s in range(HEAD_PAIRS):
        c0 = F32_COLS + s * LANES
        q = _headnorm(res[:, c0:c0 + LANES], gq_ref[...], is_a) * q_scale
        qkv_ref[:, s * LANES:(s + 1) * LANES] = q.astype(BF16)
        c1 = c0 + ATTN_WIDTH
        k = _headnorm(res[:, c1:c1 + LANES], gk_ref[...], is_a)
        qkv_ref[:, ATTN_WIDTH + s * LANES:ATTN_WIDTH + (s + 1) * LANES] = k.astype(BF16)
    qkv_ref[:, 2 * ATTN_WIDTH:3 * ATTN_WIDTH] = res[:, F32_COLS + 2 * ATTN_WIDTH:IN_COLS].astype(BF16)


def _layer_spec(shape, layer):
    return pl.BlockSpec((None,) + shape, lambda i: (layer,) + (0,) * len(shape), pipeline_mode=pl.Buffered(1))


def _inproj(h, g_all, w_all, gq_all, gk_all, layer):
    t = h.shape[0]
    tm = min(TOKEN_TILE, t)
    return pl.pallas_call(
        _inproj_kernel,
        grid=(t // tm,),
        in_specs=[
            pl.BlockSpec((tm, D_MODEL), lambda i: (i, 0)),
            _layer_spec((1, D_MODEL), layer),
            _layer_spec((D_MODEL, IN_COLS), layer),
            _layer_spec((1, LANES), layer),
            _layer_spec((1, LANES), layer),
        ],
        out_specs=[pl.BlockSpec((tm, SSM_WIDTH), lambda i: (i, 0)),
                   pl.BlockSpec((tm, BF_COLS), lambda i: (i, 0))],
        out_shape=[jax.ShapeDtypeStruct((t, SSM_WIDTH), F32), jax.ShapeDtypeStruct((t, BF_COLS), BF16)],
        scratch_shapes=[pltpu.VMEM((D_MODEL, IN_COLS), BF16)],
        compiler_params=pltpu.CompilerParams(
            dimension_semantics=("arbitrary",), vmem_limit_bytes=VMEM_LIMIT),
        name="inproj",
    )(h, g_all, w_all, gq_all, gk_all)


def _ssm_setup_kernel(are_r, aim_r, ldt_r, bre_ref, bim_ref, cre_ref, cim_ref,
                      w1_ref, w2_ref, pw_ref, *, n_pw, n_lvl2):
    ns = SLAB_STATES
    a_re = are_r[...]
    a_im = aim_r[...]
    dt = jnp.exp(ldt_r[...])
    lam_r = a_re * dt
    lam_i = a_im * dt
    mag = jnp.exp(lam_r)
    ab_re = mag * jnp.cos(lam_i)
    ab_im = mag * jnp.sin(lam_i)
    num_re = ab_re - 1.0
    num_im = ab_im
    den = a_re * a_re + a_im * a_im
    f_re = (num_re * a_re + num_im * a_im) / den
    f_im = (num_im * a_re - num_re * a_im) / den
    b_re = bre_ref[...]
    b_im = bim_ref[...]
    bb_re = jnp.tile(f_re * b_re - f_im * b_im, (SLAB_GROUPS, 1))
    bb_im = jnp.tile(f_re * b_im + f_im * b_re, (SLAB_GROUPS, 1))
    grp_row = lax.broadcasted_iota(jnp.int32, (LANES, ns), 0) >> 4
    grp_col = lax.broadcasted_iota(jnp.int32, (LANES, ns), 1) >> 6
    same_in = grp_row == grp_col
    zin = jnp.zeros((LANES, ns), F32)
    bb_re = jnp.where(same_in, bb_re, zin)
    bb_im = jnp.where(same_in, bb_im, zin)

    e_in = (SSM_CHUNK - 1 - lax.broadcasted_iota(jnp.int32, (SSM_CHUNK, 1), 0)).astype(F32)
    mag_in = jnp.exp(lam_r * e_in)
    pin_re = mag_in * jnp.cos(lam_i * e_in)
    pin_im = mag_in * jnp.sin(lam_i * e_in)
    for lp in range(SSM_CHUNK):
        p_re = pin_re[lp:lp + 1, :]
        p_im = pin_im[lp:lp + 1, :]
        w1_ref[lp * LANES:(lp + 1) * LANES, 0:ns] = (p_re * bb_re - p_im * bb_im).astype(BF16)
        w1_ref[lp * LANES:(lp + 1) * LANES, ns:2 * ns] = (p_re * bb_im + p_im * bb_re).astype(BF16)

    grp_row_o = lax.broadcasted_iota(jnp.int32, (ns, LANES), 0) >> 6
    grp_col_o = lax.broadcasted_iota(jnp.int32, (ns, LANES), 1) >> 4
    same_out = grp_row_o == grp_col_o
    zout = jnp.zeros((ns, LANES), F32)
    c_re = jnp.where(same_out, cre_ref[...], zout)
    c_im = jnp.where(same_out, cim_ref[...], zout)

    lag = lax.broadcasted_iota(jnp.int32, (2 * SSM_CHUNK, 1), 0).astype(F32)
    mag_c = jnp.exp(lam_r * lag)
    fill = jnp.zeros((LANES - 2 * SSM_CHUNK, ns), F32)
    pc_re = jnp.concatenate([mag_c * jnp.cos(lam_i * lag), fill], axis=0).T
    pc_im = jnp.concatenate([mag_c * jnp.sin(lam_i * lag), fill], axis=0).T
    c_pow = []
    for l in range(SSM_CHUNK + 1):
        q_re = jnp.broadcast_to(pc_re[:, l:l + 1], (ns, LANES))
        q_im = jnp.broadcast_to(pc_im[:, l:l + 1], (ns, LANES))
        c_pow.append((c_re * q_re - c_im * q_im, -c_re * q_im - c_im * q_re))
    kf = []
    for l in range(SSM_CHUNK):
        w2_ref[0:ns, l * LANES:(l + 1) * LANES] = c_pow[l + 1][0].astype(BF16)
        w2_ref[ns:2 * ns, l * LANES:(l + 1) * LANES] = c_pow[l + 1][1].astype(BF16)
        kf.append((_dot_exact(bb_re, c_pow[l][0]) + _dot_exact(bb_im, c_pow[l][1])).astype(BF16))
    zblk = jnp.zeros((LANES, LANES), BF16)
    for lp in range(SSM_CHUNK):
        for l in range(SSM_CHUNK):
            w1_ref[lp * LANES:(lp + 1) * LANES, 2 * ns + l * LANES:2 * ns + (l + 1) * LANES] = (
                kf[l - lp] if l >= lp else zblk)

    prow = lax.broadcasted_iota(jnp.int32, (n_pw, 1), 0)
    dbl = jnp.clip(prow - SCAN_RADIX, 0, max(n_lvl2 - 1, 0))
    e_pw = jnp.where(prow < SCAN_RADIX, SSM_CHUNK * prow,
                     jnp.left_shift(SSM_CHUNK * SCAN_RADIX, dbl)).astype(F32)
    mag_pw = jnp.exp(lam_r * e_pw)
    pw_ref[:, 0:ns] = mag_pw * jnp.cos(lam_i * e_pw)
    pw_ref[:, ns:2 * ns] = mag_pw * jnp.sin(lam_i * e_pw)


def _ssm_setup(a_re, a_im, log_dt, b_re, b_im, c_re, c_im, n_pw, n_lvl2):
    depth = a_re.shape[0]
    g, p, h = a_re.shape[1], SSM_STATE, SSM_GROUP
    ns = SLAB_STATES
    ldt = jnp.broadcast_to(log_dt[:, :, None], (depth, g, p))

    def row(v):
        return v.reshape(depth, SSM_SLABS, 1, ns)

    def b_lay(v):
        return v.reshape(depth, SSM_SLABS, SLAB_GROUPS, p, h).transpose(0, 1, 4, 2, 3).reshape(
            depth, SSM_SLABS, h, ns)

    def c_lay(v):
        vt = v.reshape(depth, SSM_SLABS, SLAB_GROUPS, h, p).transpose(0, 1, 2, 4, 3).reshape(
            depth, SSM_SLABS, ns, h)
        return jnp.tile(vt, (1, 1, 1, SLAB_GROUPS))

    def spec(shape):
        return pl.BlockSpec((None, None) + shape, lambda d, q: (d, q) + (0,) * len(shape))

    return pl.pallas_call(
        functools.partial(_ssm_setup_kernel, n_pw=n_pw, n_lvl2=n_lvl2),
        grid=(depth, SSM_SLABS),
        in_specs=[spec((1, ns))] * 3 + [spec((h, ns))] * 2 + [spec((ns, LANES))] * 2,
        out_specs=[spec((CHUNK_COLS, 2 * ns + CHUNK_COLS)), spec((2 * ns, CHUNK_COLS)), spec((n_pw, 2 * ns))],
        out_shape=[
            jax.ShapeDtypeStruct((depth, SSM_SLABS, CHUNK_COLS, 2 * ns + CHUNK_COLS), BF16),
            jax.ShapeDtypeStruct((depth, SSM_SLABS, 2 * ns, CHUNK_COLS), BF16),
            jax.ShapeDtypeStruct((depth, SSM_SLABS, n_pw, 2 * ns), F32),
        ],
        compiler_params=pltpu.CompilerParams(
            dimension_semantics=("arbitrary", "arbitrary"), vmem_limit_bytes=VMEM_LIMIT),
        name="ssm_setup",
    )(row(a_re), row(a_im), row(ldt), b_lay(b_re), b_lay(b_im), c_lay(c_re), c_lay(c_im))


def _ssm_main_kernel(u_all, w1_ref, w2_ref, pw_ref, y_all, *scratch, n_chunks, n_lvl2, pad, nb):
    per = len(scratch) // nb
    y_intras = [None] * nb
    for phase in (_ssm_project, _ssm_scan, _ssm_emit):
        for bb in range(nb):
            phase(u_all.at[bb], w1_ref, w2_ref, pw_ref, y_all.at[bb], y_intras, bb,
                  *scratch[bb * per:(bb + 1) * per], n_chunks=n_chunks, n_lvl2=n_lvl2, pad=pad)


def _ssm_project(u_ref, w1_ref, w2_ref, pw_ref, y_ref, y_intras, bb, lhs_ref, z_scr, sp_scr, g_a, g_b,
                 *, n_chunks, n_lvl2, pad):
    ns = SLAB_STATES
    nsl = ns // LANES
    for l in range(SSM_CHUNK):
        lhs_ref[:, l * LANES:(l + 1) * LANES] = u_ref[pl.ds(l, n_chunks, stride=SSM_CHUNK), :].astype(BF16)
    zin = _dot(lhs_ref[...], w1_ref[:, 0:2 * ns])
    for j in range(2 * nsl):
        z_scr[j] = zin[:, j * LANES:(j + 1) * LANES]
    y_intras[bb] = [_dot(lhs_ref[:, 0:(c + 1) * MXU_TILE],
                         w1_ref[0:(c + 1) * MXU_TILE, 2 * ns + c * MXU_TILE:2 * ns + (c + 1) * MXU_TILE])
                    for c in range(CHUNK_COLS // MXU_TILE)]


def _ssm_scan(u_ref, w1_ref, w2_ref, pw_ref, y_ref, y_intras, bb, lhs_ref, z_scr, sp_scr, g_a, g_b,
              *, n_chunks, n_lvl2, pad):
    ns = SLAB_STATES
    nsl = ns // LANES
    ng = n_chunks // SCAN_RADIX
    for js in range(nsl):
        jr, ji = js, nsl + js
        lr = slice(js * LANES, (js + 1) * LANES)
        li = slice(ns + js * LANES, ns + (js + 1) * LANES)
        a_r = pw_ref[1:2, lr]
        a_i = pw_ref[1:2, li]
        xs = []
        for lo in range(SCAN_RADIX):
            z_r = z_scr[jr, pl.ds(lo, ng, stride=SCAN_RADIX), :]
            z_i = z_scr[ji, pl.ds(lo, ng, stride=SCAN_RADIX), :]
            if lo == 0:
                x_r, x_i = z_r, z_i
            else:
                x_r, x_i = a_r * x_r - a_i * x_i + z_r, a_r * x_i + a_i * x_r + z_i
            xs.append((x_r, x_i))
        ga, gb = g_a.at[js], g_b.at[js]
        zeros = jnp.zeros((pad, 2 * LANES), F32)
        ga[0:pad, :] = zeros
        gb[0:pad, :] = zeros
        ga[pad:pad + ng, 0:LANES] = x_r
        ga[pad:pad + ng, LANES:2 * LANES] = x_i
        src, dst = ga, gb
        for k in range(n_lvl2):
            d = 1 << k
            t_r = src[pad:pad + ng, 0:LANES]
            t_i = src[pad:pad + ng, LANES:2 * LANES]
            s_r = src[pad - d:pad - d + ng, 0:LANES]
            s_i = src[pad - d:pad - d + ng, LANES:2 * LANES]
            p_r = pw_ref[SCAN_RADIX + k:SCAN_RADIX + k + 1, lr]
            p_i = pw_ref[SCAN_RADIX + k:SCAN_RADIX + k + 1, li]
            dst[pad:pad + ng, 0:LANES] = t_r + p_r * s_r - p_i * s_i
            dst[pad:pad + ng, LANES:2 * LANES] = t_i + p_r * s_i + p_i * s_r
            src, dst = dst, src
        sg_r = src[pad - 1:pad - 1 + ng, 0:LANES]
        sg_i = src[pad - 1:pad - 1 + ng, LANES:2 * LANES]
        for lo in range(SCAN_RADIX):
            if lo == 0:
                s_r, s_i = sg_r, sg_i
            else:
                p_r = pw_ref[lo:lo + 1, lr]
                p_i = pw_ref[lo:lo + 1, li]
                s_r = p_r * sg_r - p_i * sg_i + xs[lo - 1][0]
                s_i = p_r * sg_i + p_i * sg_r + xs[lo - 1][1]
            sp_scr[jr, pl.ds(lo, ng, stride=SCAN_RADIX), :] = s_r
            sp_scr[ji, pl.ds(lo, ng, stride=SCAN_RADIX), :] = s_i


def _ssm_emit(u_ref, w1_ref, w2_ref, pw_ref, y_ref, y_intras, bb, lhs_ref, z_scr, sp_scr, g_a, g_b,
              *, n_chunks, n_lvl2, pad):
    nsl = SLAB_STATES // LANES
    s_prev = jnp.concatenate([sp_scr[j] for j in range(2 * nsl)], axis=1).astype(BF16)
    y = jnp.concatenate(y_intras[bb], axis=1) + _dot(s_prev, w2_ref[...])
    for l in range(SSM_CHUNK):
        y_ref[pl.ds(l, n_chunks, stride=SSM_CHUNK), :] = y[:, l * LANES:(l + 1) * LANES]


def _ssm_main(pf3, w1, w2, pw, n_lvl2, layer):
    bsz, seq, _ = pf3.shape
    n_chunks = seq // SSM_CHUNK
    ns = SLAB_STATES
    nsl = ns // LANES
    ng = n_chunks // SCAN_RADIX
    pad = max(8, ng // 2)
    n_pw = pw.shape[2]
    nb = SSM_SEQS_PER_STEP if bsz % SSM_SEQS_PER_STEP == 0 else 1
    kern = functools.partial(_ssm_main_kernel, n_chunks=n_chunks, n_lvl2=n_lvl2, pad=pad, nb=nb)
    return pl.pallas_call(
        kern,
        grid=(SSM_SLABS, bsz // nb),
        in_specs=[
            pl.BlockSpec((nb, seq, LANES), lambda s, b: (b, 0, s)),
            pl.BlockSpec((None, None, CHUNK_COLS, 2 * ns + CHUNK_COLS), lambda s, b: (layer, s, 0, 0)),
            pl.BlockSpec((None, None, 2 * ns, CHUNK_COLS), lambda s, b: (layer, s, 0, 0)),
            pl.BlockSpec((None, None, n_pw, 2 * ns), lambda s, b: (layer, s, 0, 0)),
        ],
        out_specs=pl.BlockSpec((nb, seq, LANES), lambda s, b: (b, 0, s)),
        out_shape=jax.ShapeDtypeStruct((bsz, seq, SSM_WIDTH), F32),
        scratch_shapes=[pltpu.VMEM((n_chunks, CHUNK_COLS), BF16),
                        pltpu.VMEM((2 * nsl, n_chunks, LANES), F32),
                        pltpu.VMEM((2 * nsl, n_chunks, LANES), F32),
                        pltpu.VMEM((nsl, pad + ng, 2 * LANES), F32),
                        pltpu.VMEM((nsl, pad + ng, 2 * LANES), F32)] * nb,
        compiler_params=pltpu.CompilerParams(
            dimension_semantics=("arbitrary", "arbitrary"), vmem_limit_bytes=VMEM_LIMIT),
        name="ssm_main",
    )(pf3, w1, w2, pw)


def _attn_kernel(q_ref, k_ref, v_ref, tri1_ref, tri2_ref, o_ref, acc_ref, stay_ref, kv_scr,
                 *, qb, kw, nq, top, seq):
    i = pl.program_id(2)
    q0 = pl.multiple_of(i * (nq * qb), nq * qb)
    n_heads = 2 * PAIRS_PER_STEP

    @pl.when(i == 0)
    def _():
        is_a = jnp.concatenate([_head_mask()] * PAIRS_PER_STEP, axis=1)
        chunk = min(seq, 512)

        def fill(c, carry):
            r = pl.multiple_of(c * chunk, chunk)
            for j, ref in enumerate((k_ref, v_ref)):
                blk = ref[pl.ds(r, chunk), :]
                zero = jnp.zeros_like(blk)
                kv_scr[2 * j, pl.ds(r, chunk), :] = jnp.where(is_a, blk, zero)
                kv_scr[2 * j + 1, pl.ds(r, chunk), :] = jnp.where(is_a, zero, blk)
            return carry
        lax.fori_loop(0, seq // chunk, fill, 0)

    d_row = lax.broadcasted_iota(jnp.int32, (qb, 2 * qb), 0)
    d_key = lax.broadcasted_iota(jnp.int32, (qb, 2 * qb), 1) & (qb - 1)
    causal = d_key < d_row
    w_col = lax.broadcasted_iota(jnp.int32, (qb, kw), 1)
    tris = {qb: tri1_ref[...], kw: tri2_ref[...]}

    def rows_into(full, part, r0, nr):
        def rest(a, b):
            return jnp.zeros((b - a, part.shape[1]), part.dtype) if full is None else full[a:b]
        pieces = ([rest(0, r0)] if r0 > 0 else []) + [part] + ([rest(r0 + nr, qb)] if r0 + nr < qb else [])
        return pieces[0] if len(pieces) == 1 else jnp.concatenate(pieces, axis=0)

    def sweep(parts, stay):
        stay = [list(row) for row in stay]
        unswept = [[set(range(0, qb, ROW_GRANULE)) if st is None else set() for st in row] for row in stay]
        units = [(pi, p) for pi in range(len(parts)) for p in range(PAIRS_PER_STEP)]
        zs, sps, lefts, laters, ws = {}, {}, {}, {}, {}

        def stacked(j, k0, width, p):
            cols = slice(p * LANES, (p + 1) * LANES)
            return jnp.concatenate([kv_scr[2 * j, pl.ds(k0, width), cols],
                                    kv_scr[2 * j + 1, pl.ds(k0, width), cols]], axis=0)

        def scores(u):
            pi, p = u
            s, r0, nr, k0, width, _ = parts[pi]
            qp = q_ref[s * qb + r0:s * qb + r0 + nr, p * LANES:(p + 1) * LANES]
            zs[u] = lax.dot_general(qp, stacked(0, k0, width, p), (((1,), (1,)), ((), ())),
                                    preferred_element_type=F32)

        def stick(u):
            pi, p = u
            s, r0, nr, _, width, mask = parts[pi]
            z = zs.pop(u)
            sp = jnp.maximum(z, 0.0) + jnp.log(1.0 + jnp.exp2(-jnp.abs(z))) * LOG2E
            if mask is not None:
                sp = jnp.where(mask, sp, 0.0)
            base = z - sp
            touched = set(range(r0, r0 + nr, ROW_GRANULE))
            halves = []
            for hd in range(2):
                n = 2 * p + hd
                cols = slice(hd * width, (hd + 1) * width)
                tot = jnp.sum(sp[:, cols], axis=-1, keepdims=True)
                if touched <= unswept[s][n]:
                    halves.append(None)
                    rows = tot
                else:
                    seen = stay[s][n][r0:r0 + nr]
                    halves.append(base[:, cols] - seen)
                    rows = seen + tot
                unswept[s][n] = unswept[s][n] - touched
                stay[s][n] = rows if nr == qb else rows_into(stay[s][n], rows, r0, nr)
            sps[u] = sp.astype(BF16)
            lefts[u] = base if halves[0] is None else jnp.concatenate(halves, axis=1)

        def later(u):
            laters[u] = _dot(sps.pop(u), tris[parts[u[0]][4]])

        def weight(u):
            mask = parts[u[0]][5]
            w = jnp.exp2(lefts.pop(u) - laters.pop(u))
            if mask is not None:
                w = jnp.where(mask, w, 0.0)
            ws[u] = w.astype(BF16)

        for stage in (scores, stick, later, weight):
            for u in units:
                stage(u)
        pvs = [[None] * PAIRS_PER_STEP for _ in range(nq)]
        for pi, (s, r0, nr, k0, width, _) in enumerate(parts):
            for p in range(PAIRS_PER_STEP):
                d = rows_into(None, _dot(ws.pop((pi, p)), stacked(1, k0, width, p)), r0, nr)
                pvs[s][p] = d if pvs[s][p] is None else pvs[s][p] + d
        return pvs, stay

    def commit(pvs, stay, first):
        low = None
        for s in range(nq):
            for p in range(PAIRS_PER_STEP):
                acc_ref[s, p] = pvs[s][p] if first else acc_ref[s, p] + pvs[s][p]
            for n in range(n_heads):
                stay_ref[s, n] = jnp.broadcast_to(stay[s][n], (qb, LANES))
                low = stay[s][n] if low is None else jnp.minimum(low, stay[s][n])
        return jnp.min(low)

    fresh = [[None] * n_heads for _ in range(nq)]
    diags = [(s, 0, qb, pl.multiple_of(q0 + s * qb, qb), qb, causal) for s in range(nq)]

    def stay_now():
        return [[stay_ref[s, n][:, 0:1] for n in range(n_heads)] for s in range(nq)]

    def key_block(s, back):
        return pl.multiple_of(q0 + (s - back) * qb, qb)

    def main_step():
        near = [(s, 0, qb, key_block(s, 1), qb, None) for s in range(nq)]
        far_top = [(s, 0, top, key_block(s, 2), qb, None) for s in range(nq)]
        return commit(*sweep(diags + near + far_top, fresh), True)

    def far_rest():
        parts = [(s, top, qb - top, key_block(s, 2), qb, None) for s in range(nq)]
        return commit(*sweep(parts, stay_now()), False)

    def first_step():
        prevs = [(s, 0, qb, j * qb, qb, None) for s in range(nq) for j in reversed(range(s))]
        return commit(*sweep(diags + prevs, fresh), True)

    has_window = q0 >= kw
    low1 = lax.cond(has_window, main_step, first_step)
    low2 = lax.cond(jnp.logical_and(has_window, low1 < STAY_BITS_DONE), far_rest, lambda: low1)
    start1 = jnp.where(has_window, q0 + (nq - 1) * qb - kw, 0)

    def cond(c):
        start, low = c
        return jnp.logical_and(start > 0, low < STAY_BITS_DONE)

    def body(c):
        start, _ = c
        parts = []
        for s in range(nq):
            start_s = jnp.maximum(start - (nq - 1 - s) * qb, 0)
            k0 = pl.multiple_of(jnp.maximum(start_s - kw, 0), qb)
            valid = (k0 + w_col) < start_s
            parts.append((s, 0, qb, k0, kw, jnp.concatenate([valid, valid], axis=1)))
        return jnp.maximum(start - kw, 0), commit(*sweep(parts, stay_now()), False)

    lax.while_loop(cond, body, (start1, low2))
    for s in range(nq):
        for p in range(PAIRS_PER_STEP):
            o_ref[s * qb:(s + 1) * qb, p * LANES:(p + 1) * LANES] = acc_ref[s, p].astype(BF16)


def _attention(qkv3):
    bsz, seq, _ = qkv3.shape
    qb = min(ATTN_QUERY_BLOCK, seq)
    kw = min(ATTN_KEY_WINDOW, seq)
    nq = ATTN_SUBBLOCKS
    blk = nq * qb
    assert seq % blk == 0 and kw == 2 * qb and blk >= kw
    w = PAIRS_PER_STEP * LANES
    n_grp = HEAD_PAIRS // PAIRS_PER_STEP
    kern = functools.partial(_attn_kernel, qb=qb, kw=kw, nq=nq, top=ATTN_TOP_ROWS, seq=seq)

    def pair_tri(n):
        return jnp.kron(jnp.eye(2, dtype=F32), jnp.tri(n, n, -1, dtype=F32)).astype(BF16)
    return pl.pallas_call(
        kern,
        grid=(bsz, n_grp, seq // blk),
        in_specs=[
            pl.BlockSpec((None, blk, w), lambda b, g, i: (b, i, g)),
            pl.BlockSpec((None, seq, w), lambda b, g, i: (b, 0, n_grp + g)),
            pl.BlockSpec((None, seq, w), lambda b, g, i: (b, 0, 2 * n_grp + g)),
            pl.BlockSpec((2 * qb, 2 * qb), lambda b, g, i: (0, 0)),
            pl.BlockSpec((2 * kw, 2 * kw), lambda b, g, i: (0, 0)),
        ],
        out_specs=pl.BlockSpec((None, blk, w), lambda b, g, i: (b, i, g)),
        out_shape=jax.ShapeDtypeStruct((bsz, seq, ATTN_WIDTH), BF16),
        scratch_shapes=[pltpu.VMEM((nq, PAIRS_PER_STEP, qb, LANES), F32),
                        pltpu.VMEM((nq, 2 * PAIRS_PER_STEP, qb, LANES), F32),
                        pltpu.VMEM((4, seq, w), BF16)],
        compiler_params=pltpu.CompilerParams(
            dimension_semantics=("arbitrary", "arbitrary", "arbitrary"), vmem_limit_bytes=VMEM_LIMIT),
        name="attention",
    )(qkv3, qkv3, qkv3, pair_tri(qb), pair_tri(kw))


def _out_kernel(h_ref, ys_ref, u_ref, gs_ref, ga_ref, ya_ref, p_ref, d_ref, wgf_ref, bg_ref, wof_ref,
                pg_ref, wpgf_ref, wppf_ref, o_ref, wg_ref, wo1_ref, wo2_ref, wpg_ref, wpp_ref):
    @pl.when(pl.program_id(0) == 0)
    def _():
        wg_ref[...] = wgf_ref[...].astype(BF16)
        wo1_ref[...] = wof_ref[0:SSM_WIDTH, :].astype(BF16)
        wo2_ref[...] = wof_ref[SSM_WIDTH:SSM_WIDTH + ATTN_WIDTH, :].astype(BF16)
        wpg_ref[...] = wpgf_ref[...].astype(BF16)
        wpp_ref[...] = wppf_ref[...].astype(BF16)

    rows = h_ref.shape[0] // OUT_SUBTILES
    subs = [slice(i * rows, (i + 1) * rows) for i in range(OUT_SUBTILES)]
    zs = []
    for sl in subs:
        y = ys_ref[sl, :] + d_ref[...] * u_ref[sl, :]
        z = 0.5 * y * (1.0 + jnp.tanh(math.sqrt(2.0 / math.pi) * (y + 0.044715 * (y * y * y))))
        zs.append(z.astype(BF16))
    zzs = [_dot(z, wg_ref[...]) + bg_ref[...] for z in zs]
    y_ssms = []
    for sl, zz in zip(subs, zzs):
        gs = gs_ref[sl, :].astype(F32)
        y_ssm = (zz[:, 0:SSM_WIDTH] * _sigmoid(zz[:, SSM_WIDTH:2 * SSM_WIDTH])) * (gs * _sigmoid(gs))
        y_ssms.append(y_ssm.astype(BF16))
    y_atts = []
    for sl in subs:
        ga = ga_ref[sl, :].astype(F32)
        y_atts.append((ya_ref[sl, :].astype(F32) * (ga * _sigmoid(ga))).astype(BF16))
    hs = [h_ref[sl, :] + _dot(y_ssm, wo1_ref[...]) + _dot(y_att, wo2_ref[...])
          for sl, y_ssm, y_att in zip(subs, y_ssms, y_atts)]
    hns = []
    for h in hs:
        ms = jnp.mean(h * h, axis=-1, keepdims=True)
        hns.append(((h * lax.rsqrt(ms + RMS_EPS)) * pg_ref[...]).astype(BF16))
    gates = [_sigmoid(_dot(hn, wpg_ref[...])) for hn in hns]
    for sl, h, gate in zip(subs, hs, gates):
        o_ref[sl, :] = h + gate * _dot(p_ref[sl, :].astype(BF16), wpp_ref[...])


def _out(h, y_s, u, bf, y_att, p_all, d_all, w_glu_all, b_glu_all, w_out_all, ple_g_all, w_pg_all, w_pp_all, layer):
    t = h.shape[0]
    tm = min(TOKEN_TILE, t)
    return pl.pallas_call(
        _out_kernel,
        grid=(t // tm,),
        in_specs=[
            pl.BlockSpec((tm, D_MODEL), lambda i: (i, 0)),
            pl.BlockSpec((tm, SSM_WIDTH), lambda i: (i, 0)),
            pl.BlockSpec((tm, SSM_WIDTH), lambda i: (i, 0)),
            pl.BlockSpec((tm, SSM_WIDTH), lambda i: (i, QKV_COLS // SSM_WIDTH)),
            pl.BlockSpec((tm, ATTN_WIDTH), lambda i: (i, QKV_COLS // ATTN_WIDTH + 1)),
            pl.BlockSpec((tm, ATTN_WIDTH), lambda i: (i, 0)),
            pl.BlockSpec((None, tm, PLE_DIM), lambda i: (layer, i, 0)),
            _layer_spec((1, SSM_WIDTH), layer),
            _layer_spec((SSM_WIDTH, 2 * SSM_WIDTH), layer),
            _layer_spec((1, 2 * SSM_WIDTH), layer),
            _layer_spec((SSM_WIDTH + ATTN_WIDTH, D_MODEL), layer),
            _layer_spec((1, D_MODEL), layer),
            _layer_spec((D_MODEL, D_MODEL), layer),
            _layer_spec((PLE_DIM, D_MODEL), layer),
        ],
        out_specs=pl.BlockSpec((tm, D_MODEL), lambda i: (i, 0)),
        out_shape=jax.ShapeDtypeStruct((t, D_MODEL), F32),
        scratch_shapes=[pltpu.VMEM((SSM_WIDTH, 2 * SSM_WIDTH), BF16), pltpu.VMEM((SSM_WIDTH, D_MODEL), BF16),
                        pltpu.VMEM((ATTN_WIDTH, D_MODEL), BF16), pltpu.VMEM((D_MODEL, D_MODEL), BF16),
                        pltpu.VMEM((PLE_DIM, D_MODEL), BF16)],
        compiler_params=pltpu.CompilerParams(
            dimension_semantics=("arbitrary",), vmem_limit_bytes=VMEM_LIMIT),
        name="out",
    )(h, y_s, u, bf, bf, y_att, p_all, d_all, w_glu_all, b_glu_all, w_out_all, ple_g_all, w_pg_all, w_pp_all)


def kernel(x, p, mix_norm_g, w_in, ssm_a_re, ssm_a_im, ssm_log_dt, ssm_b_re, ssm_b_im, ssm_c_re, ssm_c_im,
           ssm_d, ssm_w_glu, ssm_b_glu, q_norm_g, k_norm_g, w_out, ple_norm_g, w_ple_gate, w_ple_proj):
    bsz, seq, _ = x.shape
    depth = w_in.shape[0]
    t = bsz * seq
    n_chunks = seq // SSM_CHUNK
    assert seq % SSM_CHUNK == 0 and t % min(TOKEN_TILE, t) == 0
    n_groups = n_chunks // SCAN_RADIX
    assert n_chunks % SCAN_RADIX == 0 and n_groups & (n_groups - 1) == 0
    n_lvl2 = n_groups.bit_length() - 1
    n_pw = -(-(SCAN_RADIX + n_lvl2) // 8) * 8

    w1, w2, pw = _ssm_setup(ssm_a_re, ssm_a_im, ssm_log_dt, ssm_b_re, ssm_b_im, ssm_c_re, ssm_c_im, n_pw, n_lvl2)
    gq2 = jnp.tile(q_norm_g.reshape(depth, 1, HEAD_DIM), (1, 1, 2))
    gk2 = jnp.tile(k_norm_g.reshape(depth, 1, HEAD_DIM), (1, 1, 2))
    mix_g = mix_norm_g.reshape(depth, 1, D_MODEL)
    ple_g = ple_norm_g.reshape(depth, 1, D_MODEL)
    d_skip = ssm_d.reshape(depth, 1, SSM_WIDTH)
    b_glu = ssm_b_glu.reshape(depth, 1, 2 * SSM_WIDTH)
    p3 = p.reshape(depth, t, PLE_DIM)

    h = x.reshape(t, D_MODEL)
    for i in range(depth):
        u, bf = _inproj(h, mix_g, w_in, gq2, gk2, i)
        y_s = _ssm_main(u.reshape(bsz, seq, SSM_WIDTH), w1, w2, pw, n_lvl2, i).reshape(t, SSM_WIDTH)
        y_att = _attention(bf.reshape(bsz, seq, BF_COLS)).reshape(t, ATTN_WIDTH)
        h = _out(h, y_s, u, bf, y_att, p3, d_skip, ssm_w_glu, b_glu, w_out, ple_g, w_ple_gate, w_ple_proj, i)
    return h.reshape(bsz, seq, D_MODEL)
```

```python
import functools
import math

import jax
import jax.numpy as jnp
from jax import lax
from jax.experimental import pallas as pl
from jax.experimental.pallas import tpu as pltpu

F32 = jnp.float32
BF16 = jnp.bfloat16

D_MODEL = 1024
PLE_DIM = 256
SSM_WIDTH = 512
SSM_GROUP = 16
SSM_STATE = 64
LANES = 128
MXU_TILE = 256
SLAB_GROUPS = LANES // SSM_GROUP
SSM_SLABS = SSM_WIDTH // LANES
SLAB_STATES = SLAB_GROUPS * SSM_STATE
SSM_CHUNK = 8
CHUNK_COLS = SSM_CHUNK * LANES
SSM_SEQS_PER_STEP = 1
ROW_GRANULE = 16
SCAN_RADIX = 4
ATTN_WIDTH = 512
HEAD_DIM = 64
HEAD_PAIRS = ATTN_WIDTH // LANES
PAIRS_PER_STEP = 4
IN_COLS = 3072
F32_COLS = 2 * SSM_WIDTH + ATTN_WIDTH
QKV_COLS = 3 * ATTN_WIDTH
BF_COLS = QKV_COLS + SSM_WIDTH + ATTN_WIDTH
RMS_EPS = 1e-6
LOG2E = 1.4426950408889634
ATTN_QUERY_BLOCK = 128
ATTN_SUBBLOCKS = 4
ATTN_TOP_ROWS = 32
ATTN_KEY_WINDOW = 256
STAY_BITS_DONE = 152.0
TOKEN_TILE = 512
OUT_SUBTILES = 2
VMEM_LIMIT = 56 * 1024 * 1024


def _sigmoid(x):
    return 1.0 / (1.0 + jnp.exp(-x))


def _dot(a, b):
    return jnp.dot(a, b, preferred_element_type=F32)


def _dot_exact(a, b):
    a_hi = a.astype(BF16)
    b_hi = b.astype(BF16)
    a_lo = (a - a_hi.astype(F32)).astype(BF16)
    b_lo = (b - b_hi.astype(F32)).astype(BF16)
    return _dot(a_hi, b_hi) + _dot(a_lo, b_hi) + _dot(a_hi, b_lo)


def _head_mask():
    return lax.broadcasted_iota(jnp.int32, (1, LANES), 1) < HEAD_DIM


def _headnorm(x, g, is_a):
    x2 = x * x
    s_a = jnp.sum(jnp.where(is_a, x2, 0.0), axis=-1, keepdims=True)
    s_b = jnp.sum(jnp.where(is_a, 0.0, x2), axis=-1, keepdims=True)
    ms = jnp.where(is_a, s_a, s_b) * (1.0 / HEAD_DIM)
    return (x * lax.rsqrt(ms + RMS_EPS)) * g


def _inproj_kernel(x_ref, g_ref, w_ref, gq_ref, gk_ref, f_ref, qkv_ref, w_scr):
    @pl.when(pl.program_id(0) == 0)
    def _():
        a0 = 2 * SSM_WIDTH
        w_scr[:, 0:a0] = w_ref[:, 0:a0].astype(BF16)
        w_scr[:, a0:F32_COLS] = w_ref[:, a0 + QKV_COLS:IN_COLS].astype(BF16)
        w_scr[:, F32_COLS:IN_COLS] = w_ref[:, a0:a0 + QKV_COLS].astype(BF16)

    x = x_ref[...]
    ms = jnp.mean(x * x, axis=-1, keepdims=True)
    hn = (x * lax.rsqrt(ms + RMS_EPS)) * g_ref[...]
    res = _dot(hn.astype(BF16), w_scr[...])
    f_ref[...] = res[:, 0:SSM_WIDTH]
    qkv_ref[:, QKV_COLS:BF_COLS] = res[:, SSM_WIDTH:F32_COLS].astype(BF16)
    is_a = _head_mask()
    q_scale = (HEAD_DIM ** -0.5) * LOG2E
    for s in range(HEAD_PAIRS):
        c0 = F32_COLS + s * LANES
        q = _headnorm(res[:, c0:c0 + LANES], gq_ref[...], is_a) * q_scale
        qkv_ref[:, s * LANES:(s + 1) * LANES] = q.astype(BF16)
        c1 = c0 + ATTN_WIDTH
        k = _headnorm(res[:, c1:c1 + LANES], gk_ref[...], is_a)
        qkv_ref[:, ATTN_WIDTH + s * LANES:ATTN_WIDTH + (s + 1) * LANES] = k.astype(BF16)
    qkv_ref[:, 2 * ATTN_WIDTH:3 * ATTN_WIDTH] = res[:, F32_COLS + 2 * ATTN_WIDTH:IN_COLS].astype(BF16)


def _layer_spec(shape, layer):
    return pl.BlockSpec((None,) + shape, lambda i: (layer,) + (0,) * len(shape), pipeline_mode=pl.Buffered(1))


def _inproj(h, g_all, w_all, gq_all, gk_all, layer):
    t = h.shape[0]
    tm = min(TOKEN_TILE, t)
    return pl.pallas_call(
        _inproj_kernel,
        grid=(t // tm,),
        in_specs=[
            pl.BlockSpec((tm, D_MODEL), lambda i: (i, 0)),
            _layer_spec((1, D_MODEL), layer),
            _layer_spec((D_MODEL, IN_COLS), layer),
            _layer_spec((1, LANES), layer),
            _layer_spec((1, LANES), layer),
        ],
        out_specs=[pl.BlockSpec((tm, SSM_WIDTH), lambda i: (i, 0)),
                   pl.BlockSpec((tm, BF_COLS), lambda i: (i, 0))],
        out_shape=[jax.ShapeDtypeStruct((t, SSM_WIDTH), F32), jax.ShapeDtypeStruct((t, BF_COLS), BF16)],
        scratch_shapes=[pltpu.VMEM((D_MODEL, IN_COLS), BF16)],
        compiler_params=pltpu.CompilerParams(
            dimension_semantics=("arbitrary",), vmem_limit_bytes=VMEM_LIMIT),
        name="inproj",
    )(h, g_all, w_all, gq_all, gk_all)


def _ssm_setup_kernel(are_r, aim_r, ldt_r, bre_ref, bim_ref, cre_ref, cim_ref,
                      w1_ref, w2_ref, pw_ref, *, n_pw, n_lvl2):
    ns = SLAB_STATES
    a_re = are_r[...]
    a_im = aim_r[...]
    dt = jnp.exp(ldt_r[...])
    lam_r = a_re * dt
    lam_i = a_im * dt
    mag = jnp.exp(lam_r)
    ab_re = mag * jnp.cos(lam_i)
    ab_im = mag * jnp.sin(lam_i)
    num_re = ab_re - 1.0
    num_im = ab_im
    den = a_re * a_re + a_im * a_im
    f_re = (num_re * a_re + num_im * a_im) / den
    f_im = (num_im * a_re - num_re * a_im) / den
    b_re = bre_ref[...]
    b_im = bim_ref[...]
    bb_re = jnp.tile(f_re * b_re - f_im * b_im, (SLAB_GROUPS, 1))
    bb_im = jnp.tile(f_re * b_im + f_im * b_re, (SLAB_GROUPS, 1))
    grp_row = lax.broadcasted_iota(jnp.int32, (LANES, ns), 0) >> 4
    grp_col = lax.broadcasted_iota(jnp.int32, (LANES, ns), 1) >> 6
    same_in = grp_row == grp_col
    zin = jnp.zeros((LANES, ns), F32)
    bb_re = jnp.where(same_in, bb_re, zin)
    bb_im = jnp.where(same_in, bb_im, zin)

    e_in = (SSM_CHUNK - 1 - lax.broadcasted_iota(jnp.int32, (SSM_CHUNK, 1), 0)).astype(F32)
    mag_in = jnp.exp(lam_r * e_in)
    pin_re = mag_in * jnp.cos(lam_i * e_in)
    pin_im = mag_in * jnp.sin(lam_i * e_in)
    for lp in range(SSM_CHUNK):
        p_re = pin_re[lp:lp + 1, :]
        p_im = pin_im[lp:lp + 1, :]
        w1_ref[lp * LANES:(lp + 1) * LANES, 0:ns] = (p_re * bb_re - p_im * bb_im).astype(BF16)
        w1_ref[lp * LANES:(lp + 1) * LANES, ns:2 * ns] = (p_re * bb_im + p_im * bb_re).astype(BF16)

    grp_row_o = lax.broadcasted_iota(jnp.int32, (ns, LANES), 0) >> 6
    grp_col_o = lax.broadcasted_iota(jnp.int32, (ns, LANES), 1) >> 4
    same_out = grp_row_o == grp_col_o
    zout = jnp.zeros((ns, LANES), F32)
    c_re = jnp.where(same_out, cre_ref[...], zout)
    c_im = jnp.where(same_out, cim_ref[...], zout)

    lag = lax.broadcasted_iota(jnp.int32, (2 * SSM_CHUNK, 1), 0).astype(F32)
    mag_c = jnp.exp(lam_r * lag)
    fill = jnp.zeros((LANES - 2 * SSM_CHUNK, ns), F32)
    pc_re = jnp.concatenate([mag_c * jnp.cos(lam_i * lag), fill], axis=0).T
    pc_im = jnp.concatenate([mag_c * jnp.sin(lam_i * lag), fill], axis=0).T
    c_pow = []
    for l in range(SSM_CHUNK + 1):
        q_re = jnp.broadcast_to(pc_re[:, l:l + 1], (ns, LANES))
        q_im = jnp.broadcast_to(pc_im[:, l:l + 1], (ns, LANES))
        c_pow.append((c_re * q_re - c_im * q_im, -c_re * q_im - c_im * q_re))
    kf = []
    for l in range(SSM_CHUNK):
        w2_ref[0:ns, l * LANES:(l + 1) * LANES] = c_pow[l + 1][0].astype(BF16)
        w2_ref[ns:2 * ns, l * LANES:(l + 1) * LANES] = c_pow[l + 1][1].astype(BF16)
        kf.append((_dot_exact(bb_re, c_pow[l][0]) + _dot_exact(bb_im, c_pow[l][1])).astype(BF16))
    zblk = jnp.zeros((LANES, LANES), BF16)
    for lp in range(SSM_CHUNK):
        for l in range(SSM_CHUNK):
            w1_ref[lp * LANES:(lp + 1) * LANES, 2 * ns + l * LANES:2 * ns + (l + 1) * LANES] = (
                kf[l - lp] if l >= lp else zblk)

    prow = lax.broadcasted_iota(jnp.int32, (n_pw, 1), 0)
    dbl = jnp.clip(prow - SCAN_RADIX, 0, max(n_lvl2 - 1, 0))
    e_pw = jnp.where(prow < SCAN_RADIX, SSM_CHUNK * prow,
                     jnp.left_shift(SSM_CHUNK * SCAN_RADIX, dbl)).astype(F32)
    mag_pw = jnp.exp(lam_r * e_pw)
    pw_ref[:, 0:ns] = mag_pw * jnp.cos(lam_i * e_pw)
    pw_ref[:, ns:2 * ns] = mag_pw * jnp.sin(lam_i * e_pw)


def _ssm_setup(a_re, a_im, log_dt, b_re, b_im, c_re, c_im, n_pw, n_lvl2):
    depth = a_re.shape[0]
    g, p, h = a_re.shape[1], SSM_STATE, SSM_GROUP
    ns = SLAB_STATES
    ldt = jnp.broadcast_to(log_dt[:, :, None], (depth, g, p))

    def row(v):
        return v.reshape(depth, SSM_SLABS, 1, ns)

    def b_lay(v):
        return v.reshape(depth, SSM_SLABS, SLAB_GROUPS, p, h).transpose(0, 1, 4, 2, 3).reshape(
            depth, SSM_SLABS, h, ns)

    def c_lay(v):
        vt = v.reshape(depth, SSM_SLABS, SLAB_GROUPS, h, p).transpose(0, 1, 2, 4, 3).reshape(
            depth, SSM_SLABS, ns, h)
        return jnp.tile(vt, (1, 1, 1, SLAB_GROUPS))

    def spec(shape):
        return pl.BlockSpec((None, None) + shape, lambda d, q: (d, q) + (0,) * len(shape))

    return pl.pallas_call(
        functools.partial(_ssm_setup_kernel, n_pw=n_pw, n_lvl2=n_lvl2),
        grid=(depth, SSM_SLABS),
        in_specs=[spec((1, ns))] * 3 + [spec((h, ns))] * 2 + [spec((ns, LANES))] * 2,
        out_specs=[spec((CHUNK_COLS, 2 * ns + CHUNK_COLS)), spec((2 * ns, CHUNK_COLS)), spec((n_pw, 2 * ns))],
        out_shape=[
            jax.ShapeDtypeStruct((depth, SSM_SLABS, CHUNK_COLS, 2 * ns + CHUNK_COLS), BF16),
            jax.ShapeDtypeStruct((depth, SSM_SLABS, 2 * ns, CHUNK_COLS), BF16),
            jax.ShapeDtypeStruct((depth, SSM_SLABS, n_pw, 2 * ns), F32),
        ],
        compiler_params=pltpu.CompilerParams(
            dimension_semantics=("arbitrary", "arbitrary"), vmem_limit_bytes=VMEM_LIMIT),
        name="ssm_setup",
    )(row(a_re), row(a_im), row(ldt), b_lay(b_re), b_lay(b_im), c_lay(c_re), c_lay(c_im))


def _ssm_main_kernel(u_all, w1_ref, w2_ref, pw_ref, y_all, *scratch, n_chunks, n_lvl2, pad, nb):
    per = len(scratch) // nb
    y_intras = [None] * nb
    for phase in (_ssm_project, _ssm_scan, _ssm_emit):
        for bb in range(nb):
            phase(u_all.at[bb], w1_ref, w2_ref, pw_ref, y_all.at[bb], y_intras, bb,
                  *scratch[bb * per:(bb + 1) * per], n_chunks=n_chunks, n_lvl2=n_lvl2, pad=pad)


def _ssm_project(u_ref, w1_ref, w2_ref, pw_ref, y_ref, y_intras, bb, lhs_ref, z_scr, sp_scr, g_a, g_b,
                 *, n_chunks, n_lvl2, pad):
    ns = SLAB_STATES
    nsl = ns // LANES
    for l in range(SSM_CHUNK):
        lhs_ref[:, l * LANES:(l + 1) * LANES] = u_ref[pl.ds(l, n_chunks, stride=SSM_CHUNK), :].astype(BF16)
    zin = _dot(lhs_ref[...], w1_ref[:, 0:2 * ns])
    for j in range(2 * nsl):
        z_scr[j] = zin[:, j * LANES:(j + 1) * LANES]
    y_intras[bb] = [_dot(lhs_ref[:, 0:(c + 1) * MXU_TILE],
                         w1_ref[0:(c + 1) * MXU_TILE, 2 * ns + c * MXU_TILE:2 * ns + (c + 1) * MXU_TILE])
                    for c in range(CHUNK_COLS // MXU_TILE)]


def _ssm_scan(u_ref, w1_ref, w2_ref, pw_ref, y_ref, y_intras, bb, lhs_ref, z_scr, sp_scr, g_a, g_b,
              *, n_chunks, n_lvl2, pad):
    ns = SLAB_STATES
    nsl = ns // LANES
    ng = n_chunks // SCAN_RADIX
    for js in range(nsl):
        jr, ji = js, nsl + js
        lr = slice(js * LANES, (js + 1) * LANES)
        li = slice(ns + js * LANES, ns + (js + 1) * LANES)
        a_r = pw_ref[1:2, lr]
        a_i = pw_ref[1:2, li]
        xs = []
        for lo in range(SCAN_RADIX):
            z_r = z_scr[jr, pl.ds(lo, ng, stride=SCAN_RADIX), :]
            z_i = z_scr[ji, pl.ds(lo, ng, stride=SCAN_RADIX), :]
            if lo == 0:
                x_r, x_i = z_r, z_i
            else:
                x_r, x_i = a_r * x_r - a_i * x_i + z_r, a_r * x_i + a_i * x_r + z_i
            xs.append((x_r, x_i))
        ga, gb = g_a.at[js], g_b.at[js]
        zeros = jnp.zeros((pad, 2 * LANES), F32)
        ga[0:pad, :] = zeros
        gb[0:pad, :] = zeros
        ga[pad:pad + ng, 0:LANES] = x_r
        ga[pad:pad + ng, LANES:2 * LANES] = x_i
        src, dst = ga, gb
        for k in range(n_lvl2):
            d = 1 << k
            t_r = src[pad:pad + ng, 0:LANES]
            t_i = src[pad:pad + ng, LANES:2 * LANES]
            s_r = src[pad - d:pad - d + ng, 0:LANES]
            s_i = src[pad - d:pad - d + ng, LANES:2 * LANES]
            p_r = pw_ref[SCAN_RADIX + k:SCAN_RADIX + k + 1, lr]
            p_i = pw_ref[SCAN_RADIX + k:SCAN_RADIX + k + 1, li]
            dst[pad:pad + ng, 0:LANES] = t_r + p_r * s_r - p_i * s_i
            dst[pad:pad + ng, LANES:2 * LANES] = t_i + p_r * s_i + p_i * s_r
            src, dst = dst, src
        sg_r = src[pad - 1:pad - 1 + ng, 0:LANES]
        sg_i = src[pad - 1:pad - 1 + ng, LANES:2 * LANES]
        for lo in range(SCAN_RADIX):
            if lo == 0:
                s_r, s_i = sg_r, sg_i
            else:
                p_r = pw_ref[lo:lo + 1, lr]
                p_i = pw_ref[lo:lo + 1, li]
                s_r = p_r * sg_r - p_i * sg_i + xs[lo - 1][0]
                s_i = p_r * sg_i + p_i * sg_r + xs[lo - 1][1]
            sp_scr[jr, pl.ds(lo, ng, stride=SCAN_RADIX), :] = s_r
            sp_scr[ji, pl.ds(lo, ng, stride=SCAN_RADIX), :] = s_i


def _ssm_emit(u_ref, w1_ref, w2_ref, pw_ref, y_ref, y_intras, bb, lhs_ref, z_scr, sp_scr, g_a, g_b,
              *, n_chunks, n_lvl2, pad):
    nsl = SLAB_STATES // LANES
    s_prev = jnp.concatenate([sp_scr[j] for j in range(2 * nsl)], axis=1).astype(BF16)
    y = jnp.concatenate(y_intras[bb], axis=1) + _dot(s_prev, w2_ref[...])
    for l in range(SSM_CHUNK):
        y_ref[pl.ds(l, n_chunks, stride=SSM_CHUNK), :] = y[:, l * LANES:(l + 1) * LANES]


def _ssm_main(pf3, w1, w2, pw, n_lvl2, layer):
    bsz, seq, _ = pf3.shape
    n_chunks = seq // SSM_CHUNK
    ns = SLAB_STATES
    nsl = ns // LANES
    ng = n_chunks // SCAN_RADIX
    pad = max(8, ng // 2)
    n_pw = pw.shape[2]
    nb = SSM_SEQS_PER_STEP if bsz % SSM_SEQS_PER_STEP == 0 else 1
    kern = functools.partial(_ssm_main_kernel, n_chunks=n_chunks, n_lvl2=n_lvl2, pad=pad, nb=nb)
    return pl.pallas_call(
        kern,
        grid=(SSM_SLABS, bsz // nb),
        in_specs=[
            pl.BlockSpec((nb, seq, LANES), lambda s, b: (b, 0, s)),
            pl.BlockSpec((None, None, CHUNK_COLS, 2 * ns + CHUNK_COLS), lambda s, b: (layer, s, 0, 0)),
            pl.BlockSpec((None, None, 2 * ns, CHUNK_COLS), lambda s, b: (layer, s, 0, 0)),
            pl.BlockSpec((None, None, n_pw, 2 * ns), lambda s, b: (layer, s, 0, 0)),
        ],
        out_specs=pl.BlockSpec((nb, seq, LANES), lambda s, b: (b, 0, s)),
        out_shape=jax.ShapeDtypeStruct((bsz, seq, SSM_WIDTH), F32),
        scratch_shapes=[pltpu.VMEM((n_chunks, CHUNK_COLS), BF16),
                        pltpu.VMEM((2 * nsl, n_chunks, LANES), F32),
                        pltpu.VMEM((2 * nsl, n_chunks, LANES), F32),
                        pltpu.VMEM((nsl, pad + ng, 2 * LANES), F32),
                        pltpu.VMEM((nsl, pad + ng, 2 * LANES), F32)] * nb,
        compiler_params=pltpu.CompilerParams(
            dimension_semantics=("arbitrary", "arbitrary"), vmem_limit_bytes=VMEM_LIMIT),
        name="ssm_main",
    )(pf3, w1, w2, pw)


def _attn_kernel(q_ref, k_ref, v_ref, tri1_ref, tri2_ref, o_ref, acc_ref, stay_ref, kv_scr,
                 *, qb, kw, nq, top, seq):
    i = pl.program_id(2)
    q0 = pl.multiple_of(i * (nq * qb), nq * qb)
    n_heads = 2 * PAIRS_PER_STEP

    @pl.when(i == 0)
    def _():
        is_a = jnp.concatenate([_head_mask()] * PAIRS_PER_STEP, axis=1)
        chunk = min(seq, 512)

        def fill(c, carry):
            r = pl.multiple_of(c * chunk, chunk)
            for j, ref in enumerate((k_ref, v_ref)):
                blk = ref[pl.ds(r, chunk), :]
                zero = jnp.zeros_like(blk)
                kv_scr[2 * j, pl.ds(r, chunk), :] = jnp.where(is_a, blk, zero)
                kv_scr[2 * j + 1, pl.ds(r, chunk), :] = jnp.where(is_a, zero, blk)
            return carry
        lax.fori_loop(0, seq // chunk, fill, 0)

    d_row = lax.broadcasted_iota(jnp.int32, (qb, 2 * qb), 0)
    d_key = lax.broadcasted_iota(jnp.int32, (qb, 2 * qb), 1) & (qb - 1)
    causal = d_key < d_row
    w_col = lax.broadcasted_iota(jnp.int32, (qb, kw), 1)
    tris = {qb: tri1_ref[...], kw: tri2_ref[...]}

    def rows_into(full, part, r0, nr):
        def rest(a, b):
            return jnp.zeros((b - a, part.shape[1]), part.dtype) if full is None else full[a:b]
        pieces = ([rest(0, r0)] if r0 > 0 else []) + [part] + ([rest(r0 + nr, qb)] if r0 + nr < qb else [])
        return pieces[0] if len(pieces) == 1 else jnp.concatenate(pieces, axis=0)

    def sweep(parts, stay):
        stay = [list(row) for row in stay]
        unswept = [[set(range(0, qb, ROW_GRANULE)) if st is None else set() for st in row] for row in stay]
        units = [(pi, p) for pi in range(len(parts)) for p in range(PAIRS_PER_STEP)]
        zs, sps, lefts, laters, ws = {}, {}, {}, {}, {}

        def stacked(j, k0, width, p):
            cols = slice(p * LANES, (p + 1) * LANES)
            return jnp.concatenate([kv_scr[2 * j, pl.ds(k0, width), cols],
                                    kv_scr[2 * j + 1, pl.ds(k0, width), cols]], axis=0)

        def scores(u):
            pi, p = u
            s, r0, nr, k0, width, _ = parts[pi]
            qp = q_ref[s * qb + r0:s * qb + r0 + nr, p * LANES:(p + 1) * LANES]
            zs[u] = lax.dot_general(qp, stacked(0, k0, width, p), (((1,), (1,)), ((), ())),
                                    preferred_element_type=F32)

        def stick(u):
            pi, p = u
            s, r0, nr, _, width, mask = parts[pi]
            z = zs.pop(u)
            sp = jnp.maximum(z, 0.0) + jnp.log(1.0 + jnp.exp2(-jnp.abs(z))) * LOG2E
            if mask is not None:
                sp = jnp.where(mask, sp, 0.0)
            base = z - sp
            touched = set(range(r0, r0 + nr, ROW_GRANULE))
            halves = []
            for hd in range(2):
                n = 2 * p + hd
                cols = slice(hd * width, (hd + 1) * width)
                tot = jnp.sum(sp[:, cols], axis=-1, keepdims=True)
                if touched <= unswept[s][n]:
                    halves.append(None)
                    rows = tot
                else:
                    seen = stay[s][n][r0:r0 + nr]
                    halves.append(base[:, cols] - seen)
                    rows = seen + tot
                unswept[s][n] = unswept[s][n] - touched
                stay[s][n] = rows if nr == qb else rows_into(stay[s][n], rows, r0, nr)
            sps[u] = sp.astype(BF16)
            lefts[u] = base if halves[0] is None else jnp.concatenate(halves, axis=1)

        def later(u):
            laters[u] = _dot(sps.pop(u), tris[parts[u[0]][4]])

        def weight(u):
            mask = parts[u[0]][5]
            w = jnp.exp2(lefts.pop(u) - laters.pop(u))
            if mask is not None:
                w = jnp.where(mask, w, 0.0)
            ws[u] = w.astype(BF16)

        for stage in (scores, stick, later, weight):
            for u in units:
                stage(u)
        pvs = [[None] * PAIRS_PER_STEP for _ in range(nq)]
        for pi, (s, r0, nr, k0, width, _) in enumerate(parts):
            for p in range(PAIRS_PER_STEP):
                d = rows_into(None, _dot(ws.pop((pi, p)), stacked(1, k0, width, p)), r0, nr)
                pvs[s][p] = d if pvs[s][p] is None else pvs[s][p] + d
        return pvs, stay

    def commit(pvs, stay, first):
        low = None
        for s in range(nq):
            for p in range(PAIRS_PER_STEP):
                acc_ref[s, p] = pvs[s][p] if first else acc_ref[s, p] + pvs[s][p]
            for n in range(n_heads):
                stay_ref[s, n] = jnp.broadcast_to(stay[s][n], (qb, LANES))
                low = stay[s][n] if low is None else jnp.minimum(low, stay[s][n])
        return jnp.min(low)

    fresh = [[None] * n_heads for _ in range(nq)]
    diags = [(s, 0, qb, pl.multiple_of(q0 + s * qb, qb), qb, causal) for s in range(nq)]

    def stay_now():
        return [[stay_ref[s, n][:, 0:1] for n in range(n_heads)] for s in range(nq)]

    def key_block(s, back):
        return pl.multiple_of(q0 + (s - back) * qb, qb)

    def main_step():
        near = [(s, 0, qb, key_block(s, 1), qb, None) for s in range(nq)]
        far_top = [(s, 0, top, key_block(s, 2), qb, None) for s in range(nq)]
        return commit(*sweep(diags + near + far_top, fresh), True)

    def far_rest():
        parts = [(s, top, qb - top, key_block(s, 2), qb, None) for s in range(nq)]
        return commit(*sweep(parts, stay_now()), False)

    def first_step():
        prevs = [(s, 0, qb, j * qb, qb, None) for s in range(nq) for j in reversed(range(s))]
        return commit(*sweep(diags + prevs, fresh), True)

    has_window = q0 >= kw
    low1 = lax.cond(has_window, main_step, first_step)
    low2 = lax.cond(jnp.logical_and(has_window, low1 < STAY_BITS_DONE), far_rest, lambda: low1)
    start1 = jnp.where(has_window, q0 + (nq - 1) * qb - kw, 0)

    def cond(c):
        start, low = c
        return jnp.logical_and(start > 0, low < STAY_BITS_DONE)

    def body(c):
        start, _ = c
        parts = []
        for s in range(nq):
            start_s = jnp.maximum(start - (nq - 1 - s) * qb, 0)
            k0 = pl.multiple_of(jnp.maximum(start_s - kw, 0), qb)
            valid = (k0 + w_col) < start_s
            parts.append((s, 0, qb, k0, kw, jnp.concatenate([valid, valid], axis=1)))
        return jnp.maximum(start - kw, 0), commit(*sweep(parts, stay_now()), False)

    lax.while_loop(cond, body, (start1, low2))
    for s in range(nq):
        for p in range(PAIRS_PER_STEP):
            o_ref[s * qb:(s + 1) * qb, p * LANES:(p + 1) * LANES] = acc_ref[s, p].astype(BF16)


def _attention(qkv3):
    bsz, seq, _ = qkv3.shape
    qb = min(ATTN_QUERY_BLOCK, seq)
    kw = min(ATTN_KEY_WINDOW, seq)
    nq = ATTN_SUBBLOCKS
    blk = nq * qb
    assert seq % blk == 0 and kw == 2 * qb and blk >= kw
    w = PAIRS_PER_STEP * LANES
    n_grp = HEAD_PAIRS // PAIRS_PER_STEP
    kern = functools.partial(_attn_kernel, qb=qb, kw=kw, nq=nq, top=ATTN_TOP_ROWS, seq=seq)

    def pair_tri(n):
        return jnp.kron(jnp.eye(2, dtype=F32), jnp.tri(n, n, -1, dtype=F32)).astype(BF16)
    return pl.pallas_call(
        kern,
        grid=(bsz, n_grp, seq // blk),
        in_specs=[
            pl.BlockSpec((None, blk, w), lambda b, g, i: (b, i, g)),
            pl.BlockSpec((None, seq, w), lambda b, g, i: (b, 0, n_grp + g)),
            pl.BlockSpec((None, seq, w), lambda b, g, i: (b, 0, 2 * n_grp + g)),
            pl.BlockSpec((2 * qb, 2 * qb), lambda b, g, i: (0, 0)),
            pl.BlockSpec((2 * kw, 2 * kw), lambda b, g, i: (0, 0)),
        ],
        out_specs=pl.BlockSpec((None, blk, w), lambda b, g, i: (b, i, g)),
        out_shape=jax.ShapeDtypeStruct((bsz, seq, ATTN_WIDTH), BF16),
        scratch_shapes=[pltpu.VMEM((nq, PAIRS_PER_STEP, qb, LANES), F32),
                        pltpu.VMEM((nq, 2 * PAIRS_PER_STEP, qb, LANES), F32),
                        pltpu.VMEM((4, seq, w), BF16)],
        compiler_params=pltpu.CompilerParams(
            dimension_semantics=("arbitrary", "arbitrary", "arbitrary"), vmem_limit_bytes=VMEM_LIMIT),
        name="attention",
    )(qkv3, qkv3, qkv3, pair_tri(qb), pair_tri(kw))


def _out_kernel(h_ref, ys_ref, u_ref, gs_ref, ga_ref, ya_ref, p_ref, d_ref, wgf_ref, bg_ref, wof_ref,
                pg_ref, wpgf_ref, wppf_ref, o_ref, wg_ref, wo1_ref, wo2_ref, wpg_ref, wpp_ref):
    @pl.when(pl.program_id(0) == 0)
    def _():
        wg_ref[...] = wgf_ref[...].astype(BF16)
        wo1_ref[...] = wof_ref[0:SSM_WIDTH, :].astype(BF16)
        wo2_ref[...] = wof_ref[SSM_WIDTH:SSM_WIDTH + ATTN_WIDTH, :].astype(BF16)
        wpg_ref[...] = wpgf_ref[...].astype(BF16)
        wpp_ref[...] = wppf_ref[...].astype(BF16)

    rows = h_ref.shape[0] // OUT_SUBTILES
    subs = [slice(i * rows, (i + 1) * rows) for i in range(OUT_SUBTILES)]
    zs = []
    for sl in subs:
        y = ys_ref[sl, :] + d_ref[...] * u_ref[sl, :]
        z = 0.5 * y * (1.0 + jnp.tanh(math.sqrt(2.0 / math.pi) * (y + 0.044715 * (y * y * y))))
        zs.append(z.astype(BF16))
    zzs = [_dot(z, wg_ref[...]) + bg_ref[...] for z in zs]
    y_ssms = []
    for sl, zz in zip(subs, zzs):
        gs = gs_ref[sl, :].astype(F32)
        y_ssm = (zz[:, 0:SSM_WIDTH] * _sigmoid(zz[:, SSM_WIDTH:2 * SSM_WIDTH])) * (gs * _sigmoid(gs))
        y_ssms.append(y_ssm.astype(BF16))
    y_atts = []
    for sl in subs:
        ga = ga_ref[sl, :].astype(F32)
        y_atts.append((ya_ref[sl, :].astype(F32) * (ga * _sigmoid(ga))).astype(BF16))
    hs = [h_ref[sl, :] + _dot(y_ssm, wo1_ref[...]) + _dot(y_att, wo2_ref[...])
          for sl, y_ssm, y_att in zip(subs, y_ssms, y_atts)]
    hns = []
    for h in hs:
        ms = jnp.mean(h * h, axis=-1, keepdims=True)
        hns.append(((h * lax.rsqrt(ms + RMS_EPS)) * pg_ref[...]).astype(BF16))
    gates = [_sigmoid(_dot(hn, wpg_ref[...])) for hn in hns]
    for sl, h, gate in zip(subs, hs, gates):
        o_ref[sl, :] = h + gate * _dot(p_ref[sl, :].astype(BF16), wpp_ref[...])


def _out(h, y_s, u, bf, y_att, p_all, d_all, w_glu_all, b_glu_all, w_out_all, ple_g_all, w_pg_all, w_pp_all, layer):
    t = h.shape[0]
    tm = min(TOKEN_TILE, t)
    return pl.pallas_call(
        _out_kernel,
        grid=(t // tm,),
        in_specs=[
            pl.BlockSpec((tm, D_MODEL), lambda i: (i, 0)),
            pl.BlockSpec((tm, SSM_WIDTH), lambda i: (i, 0)),
            pl.BlockSpec((tm, SSM_WIDTH), lambda i: (i, 0)),
            pl.BlockSpec((tm, SSM_WIDTH), lambda i: (i, QKV_COLS // SSM_WIDTH)),
            pl.BlockSpec((tm, ATTN_WIDTH), lambda i: (i, QKV_COLS // ATTN_WIDTH + 1)),
            pl.BlockSpec((tm, ATTN_WIDTH), lambda i: (i, 0)),
            pl.BlockSpec((None, tm, PLE_DIM), lambda i: (layer, i, 0)),
            _layer_spec((1, SSM_WIDTH), layer),
            _layer_spec((SSM_WIDTH, 2 * SSM_WIDTH), layer),
            _layer_spec((1, 2 * SSM_WIDTH), layer),
            _layer_spec((SSM_WIDTH + ATTN_WIDTH, D_MODEL), layer),
            _layer_spec((1, D_MODEL), layer),
            _layer_spec((D_MODEL, D_MODEL), layer),
            _layer_spec((PLE_DIM, D_MODEL), layer),
        ],
        out_specs=pl.BlockSpec((tm, D_MODEL), lambda i: (i, 0)),
        out_shape=jax.ShapeDtypeStruct((t, D_MODEL), F32),
        scratch_shapes=[pltpu.VMEM((SSM_WIDTH, 2 * SSM_WIDTH), BF16), pltpu.VMEM((SSM_WIDTH, D_MODEL), BF16),
                        pltpu.VMEM((ATTN_WIDTH, D_MODEL), BF16), pltpu.VMEM((D_MODEL, D_MODEL), BF16),
                        pltpu.VMEM((PLE_DIM, D_MODEL), BF16)],
        compiler_params=pltpu.CompilerParams(
            dimension_semantics=("arbitrary",), vmem_limit_bytes=VMEM_LIMIT),
        name="out",
    )(h, y_s, u, bf, bf, y_att, p_all, d_all, w_glu_all, b_glu_all, w_out_all, ple_g_all, w_pg_all, w_pp_all)


def kernel(x, p, mix_norm_g, w_in, ssm_a_re, ssm_a_im, ssm_log_dt, ssm_b_re, ssm_b_im, ssm_c_re, ssm_c_im,
           ssm_d, ssm_w_glu, ssm_b_glu, q_norm_g, k_norm_g, w_out, ple_norm_g, w_ple_gate, w_ple_proj):
    bsz, seq, _ = x.shape
    depth = w_in.shape[0]
    t = bsz * seq
    n_chunks = seq // SSM_CHUNK
    assert seq % SSM_CHUNK == 0 and t % min(TOKEN_TILE, t) == 0
    n_groups = n_chunks // SCAN_RADIX
    assert n_chunks % SCAN_RADIX == 0 and n_groups & (n_groups - 1) == 0
    n_lvl2 = n_groups.bit_length() - 1
    n_pw = -(-(SCAN_RADIX + n_lvl2) // 8) * 8

    w1, w2, pw = _ssm_setup(ssm_a_re, ssm_a_im, ssm_log_dt, ssm_b_re, ssm_b_im, ssm_c_re, ssm_c_im, n_pw, n_lvl2)
    gq2 = jnp.tile(q_norm_g.reshape(depth, 1, HEAD_DIM), (1, 1, 2))
    gk2 = jnp.tile(k_norm_g.reshape(depth, 1, HEAD_DIM), (1, 1, 2))
    mix_g = mix_norm_g.reshape(depth, 1, D_MODEL)
    ple_g = ple_norm_g.reshape(depth, 1, D_MODEL)
    d_skip = ssm_d.reshape(depth, 1, SSM_WIDTH)
    b_glu = ssm_b_glu.reshape(depth, 1, 2 * SSM_WIDTH)
    p3 = p.reshape(depth, t, PLE_DIM)

    h = x.reshape(t, D_MODEL)
    for i in range(depth):
        u, bf = _inproj(h, mix_g, w_in, gq2, gk2, i)
        y_s = _ssm_main(u.reshape(bsz, seq, SSM_WIDTH), w1, w2, pw, n_lvl2, i).reshape(t, SSM_WIDTH)
        y_att = _attention(bf.reshape(bsz, seq, BF_COLS)).reshape(t, ATTN_WIDTH)
        h = _out(h, y_s, u, bf, y_att, p3, d_skip, ssm_w_glu, b_glu, w_out, ple_g, w_ple_gate, w_ple_proj, i)
    return h.reshape(bsz, seq, D_MODEL)
```
